```python
import jax, jax.numpy as jnp
from jax import lax
import numpy as np


D_MODEL = 1024
BATCH = 4
SEQ = 4096
DEPTH = 1
DEC_BATCH = 8
DEC_SEQ = 64
PAST_LEN = 2048

CHUNK = 64
Q_BLOCK = 128
SB_HEAD_DIM = 64
SB_WIDTH = D_MODEL // 2
SB_HEADS = SB_WIDTH // SB_HEAD_DIM
SGU_GROUP_DIM = 64
SGU_WIDTH = D_MODEL - SB_WIDTH
SGU_GROUPS = SGU_WIDTH // SGU_GROUP_DIM
SGU_CHUNK = 128
D_MIX = SB_WIDTH + SGU_WIDTH
D_IN = 3 * SB_WIDTH + 2 * SGU_WIDTH
D_FF = 2816
FFN_RES = 0.5
EPS = 1e-6

kernel_name = 'stickbreak_sgu_macaron_stream_step'


def rms_norm(x, g):
    xf = x.astype(jnp.float32)
    y = xf * lax.rsqrt(jnp.mean(xf * xf, axis=-1, keepdims=True) + EPS)
    return (y * g.astype(jnp.float32)).astype(x.dtype)


def layer_norm(x, g, b):
    xf = x.astype(jnp.float32)
    mu = jnp.mean(xf, axis=-1, keepdims=True)
    xc = xf - mu
    y = xc * lax.rsqrt(jnp.mean(xc * xc, axis=-1, keepdims=True) + EPS)
    return (y * g.astype(jnp.float32) + b.astype(jnp.float32)).astype(x.dtype)


def macaron_ffn(x, pre_g, post_g, w_gate, w_up, w_down):
    h = rms_norm(x, pre_g)
    f = (jax.nn.silu(h @ w_gate) * (h @ w_up)) @ w_down
    return x + FFN_RES * rms_norm(f, post_g)


def stick_breaking_block(q, k, v, q_pos, k_pos):
    z = jnp.einsum('bqhd,bkhd->bhqk', q.astype(jnp.float32), k.astype(jnp.float32)) * (SB_HEAD_DIM ** -0.5)
    mask = k_pos[None, :] < q_pos[:, None]
    log_1m = jnp.where(mask, jax.nn.log_sigmoid(-z), 0.0)
    suffix = lax.cumsum(log_1m, axis=3, reverse=True)
    between = jnp.concatenate([suffix[..., 1:], jnp.zeros_like(suffix[..., :1])], axis=-1)
    a = jnp.where(mask, jnp.exp(jax.nn.log_sigmoid(z) + between), 0.0)
    return jnp.einsum('bhqk,bkhd->bqhd', a, v.astype(jnp.float32)).astype(v.dtype)


def stick_breaking_prompt(q, k, v):
    B, S, H, d = q.shape
    n_blk = S // Q_BLOCK
    k_pos = jnp.arange(S)
    q_blocks = q.reshape(B, n_blk, Q_BLOCK, H, d).swapaxes(0, 1)

    def one_block(args):
        q_blk, i = args
        q_pos = i * Q_BLOCK + jnp.arange(Q_BLOCK)
        return stick_breaking_block(q_blk, k, v, q_pos, k_pos)

    o = lax.map(one_block, (q_blocks, jnp.arange(n_blk)))
    return o.swapaxes(0, 1).reshape(B, S, H, d)


def stick_breaking_sample(q, k_new, v_new, past_k, past_v):
    P = past_k.shape[1]
    T = q.shape[1]
    k_all = jnp.concatenate([past_k.astype(k_new.dtype), k_new], axis=1)
    v_all = jnp.concatenate([past_v.astype(v_new.dtype), v_new], axis=1)
    q_pos = P + jnp.arange(T)
    k_pos = jnp.arange(P + T)
    return stick_breaking_block(q, k_all, v_all, q_pos, k_pos)


def sgu_mask(n):
    i = jnp.arange(n)
    return (i[None, :] // CHUNK) <= (i[:, None] // CHUNK)


def sgu_prompt(u, g, w_s, b_s):
    B, S = u.shape[:2]
    n = S // SGU_CHUNK
    gc = g.reshape(B, n, SGU_CHUNK, SGU_GROUPS, SGU_GROUP_DIM)
    w = w_s * sgu_mask(SGU_CHUNK)[None].astype(w_s.dtype)
    mixed = jnp.einsum('gij,bcjgd->bcigd', w, gc) + b_s.T[None, None, :, :, None]
    return u * mixed.reshape(B, S, SGU_WIDTH)


def sgu_sample(u, g, w_s, b_s):
    B, T = u.shape[:2]
    w = (w_s * sgu_mask(SGU_CHUNK)[None].astype(w_s.dtype))[:, :T, :T]
    mixed = jnp.einsum('gij,bjgd->bigd', w, g) + b_s[:, :T].T[None, :, :, None]
    return u * mixed.reshape(B, T, SGU_WIDTH)


def token_mix(x, p, past_k, past_v):
    B, T, _ = x.shape
    h = rms_norm(x, p['mix_pre_g'])
    z = h @ p['w_in']
    q, k, v, u, g = jnp.split(z, [SB_WIDTH, 2 * SB_WIDTH, 3 * SB_WIDTH, 3 * SB_WIDTH + SGU_WIDTH], axis=-1)
    q = q.reshape(B, T, SB_HEADS, SB_HEAD_DIM)
    k = k.reshape(B, T, SB_HEADS, SB_HEAD_DIM)
    v = v.reshape(B, T, SB_HEADS, SB_HEAD_DIM)
    u = jax.nn.gelu(u, approximate=False)
    g = layer_norm(jax.nn.gelu(g, approximate=False), p['sgu_ln_g'], p['sgu_ln_b'])
    g = g.reshape(B, T, SGU_GROUPS, SGU_GROUP_DIM)
    if past_k is None:
        o_sb = stick_breaking_prompt(q, k, v)
        o_sgu = sgu_prompt(u, g, p['sgu_w_s'], p['sgu_b_s'])
    else:
        o_sb = stick_breaking_sample(q, k, v, past_k, past_v)
        o_sgu = sgu_sample(u, g, p['sgu_w_s'], p['sgu_b_s'])
    merged = jnp.concatenate([rms_norm(o_sb.reshape(B, T, SB_WIDTH), p['g_out_sb']),
                              rms_norm(o_sgu, p['g_out_sgu'])], axis=-1)
    x = x + rms_norm(merged @ p['w_out'], p['mix_post_g'])
    return x, k, v, g


def encoder_layer(x, p, past_k, past_v):
    x = macaron_ffn(x, p['ffn1_pre_g'], p['ffn1_post_g'], p['ffn1_w_gate'], p['ffn1_w_up'], p['ffn1_w_down'])
    x, k, v, g = token_mix(x, p, past_k, past_v)
    x = macaron_ffn(x, p['ffn2_pre_g'], p['ffn2_post_g'], p['ffn2_w_gate'], p['ffn2_w_up'], p['ffn2_w_down'])
    return x, k, v, g


def setup_inputs(seed: int = 0) -> dict:
    key = jax.random.key(seed)
    ks = jax.random.split(key, 24)
    L = DEPTH

    def nrm(k, shape, scale):
        return scale * jax.random.normal(k, shape, jnp.float32)

    def gain(k, n):
        return 1.0 + 0.02 * jax.random.normal(k, (L, n), jnp.float32)

    return {
        'x_prompt': nrm(ks[0], (BATCH, SEQ, D_MODEL), 1.0),
        'x_sample': nrm(ks[1], (DEC_BATCH, DEC_SEQ, D_MODEL), 1.0),
        'cache_k_sb': nrm(ks[2], (L, DEC_BATCH, PAST_LEN, SB_HEADS, SB_HEAD_DIM), 1.0),
        'cache_v_sb': nrm(ks[3], (L, DEC_BATCH, PAST_LEN, SB_HEADS, SB_HEAD_DIM), 1.0),
        'ffn1_pre_g': gain(ks[4], D_MODEL),
        'ffn1_post_g': gain(ks[5], D_MODEL),
        'ffn1_w_gate': nrm(ks[6], (L, D_MODEL, D_FF), D_MODEL ** -0.5),
        'ffn1_w_up': nrm(ks[7], (L, D_MODEL, D_FF), D_MODEL ** -0.5),
        'ffn1_w_down': nrm(ks[8], (L, D_FF, D_MODEL), D_FF ** -0.5),
        'mix_pre_g': gain(ks[9], D_MODEL),
        'mix_post_g': gain(ks[10], D_MODEL),
        'w_in': nrm(ks[11], (L, D_MODEL, D_IN), D_MODEL ** -0.5),
        'sgu_ln_g': gain(ks[12], SGU_WIDTH),
        'sgu_ln_b': nrm(ks[13], (L, SGU_WIDTH), 0.02),
        'sgu_w_s': nrm(ks[14], (L, SGU_GROUPS, SGU_CHUNK, SGU_CHUNK), SGU_CHUNK ** -0.5),
        'sgu_b_s': 1.0 + nrm(ks[15], (L, SGU_GROUPS, SGU_CHUNK), 0.02),
        'g_out_sb': gain(ks[16], SB_WIDTH),
        'g_out_sgu': gain(ks[17], SGU_WIDTH),
        'w_out': nrm(ks[18], (L, D_MIX, D_MODEL), D_MIX ** -0.5),
        'ffn2_pre_g': gain(ks[19], D_MODEL),
        'ffn2_post_g': gain(ks[20], D_MODEL),
        'ffn2_w_gate': nrm(ks[21], (L, D_MODEL, D_FF), D_MODEL ** -0.5),
        'ffn2_w_up': nrm(ks[22], (L, D_MODEL, D_FF), D_MODEL ** -0.5),
        'ffn2_w_down': nrm(ks[23], (L, D_FF, D_MODEL), D_FF ** -0.5),
    }


def reference(x_prompt, x_sample, cache_k_sb, cache_v_sb,
              ffn1_pre_g, ffn1_post_g, ffn1_w_gate, ffn1_w_up, ffn1_w_down,
              mix_pre_g, mix_post_g, w_in, sgu_ln_g, sgu_ln_b, sgu_w_s, sgu_b_s,
              g_out_sb, g_out_sgu, w_out,
              ffn2_pre_g, ffn2_post_g, ffn2_w_gate, ffn2_w_up, ffn2_w_down):
    params = {
        'ffn1_pre_g': ffn1_pre_g, 'ffn1_post_g': ffn1_post_g,
        'ffn1_w_gate': ffn1_w_gate, 'ffn1_w_up': ffn1_w_up, 'ffn1_w_down': ffn1_w_down,
        'mix_pre_g': mix_pre_g, 'mix_post_g': mix_post_g, 'w_in': w_in,
        'sgu_ln_g': sgu_ln_g, 'sgu_ln_b': sgu_ln_b, 'sgu_w_s': sgu_w_s, 'sgu_b_s': sgu_b_s,
        'g_out_sb': g_out_sb, 'g_out_sgu': g_out_sgu, 'w_out': w_out,
        'ffn2_pre_g': ffn2_pre_g, 'ffn2_post_g': ffn2_post_g,
        'ffn2_w_gate': ffn2_w_gate, 'ffn2_w_up': ffn2_w_up, 'ffn2_w_down': ffn2_w_down,
    }
    xp = x_prompt
    xs = x_sample
    k_p, v_p, k_s, v_s, g_s = [], [], [], [], []
    for l in range(DEPTH):
        p = {name: arr[l] for name, arr in params.items()}
        xp, kp, vp, _ = encoder_layer(xp, p, None, None)
        xs, kq, vq, gq = encoder_layer(xs, p, cache_k_sb[l], cache_v_sb[l])
        k_p.append(kp)
        v_p.append(vp)
        k_s.append(kq)
        v_s.append(vq)
        g_s.append(gq)
    return (xp, xs, jnp.stack(k_p), jnp.stack(v_p), jnp.stack(k_s), jnp.stack(v_s), jnp.stack(g_s))
```

```python
import functools

import jax
import jax.numpy as jnp
from jax import lax
from jax.experimental import pallas as pl
from jax.experimental.pallas import tpu as pltpu

D_MODEL = 1024
D_FF = 2816
SB_WIDTH = 512
SGU_WIDTH = 512
D_IN = 3 * SB_WIDTH + 2 * SGU_WIDTH
HEAD_DIM = 64
SGU_CHUNK = 128
FFN_RES = 0.5
EPS = 1e-6

LANES = 128
N_PAIRS = SB_WIDTH // LANES
KEY_BLOCK = 128
FF_CHUNK = 1408
ROW_TILE = 512
VMEM_LIMIT = 56 * 1024 * 1024

LOG_STICK_CUTOFF = -105.0

F32 = jnp.float32
BF16 = jnp.bfloat16


def _rms(x, g):
    return x * lax.rsqrt(jnp.mean(x * x, axis=-1, keepdims=True) + EPS) * g


def _gelu(x):
    return 0.5 * x * (1.0 + lax.erf(x * (0.5 ** 0.5)))


def _const_spec(shape):
    return pl.BlockSpec(shape, lambda *_: (0,) * len(shape), pipeline_mode=pl.Buffered(1))


def _params(n_axes):
    return pltpu.CompilerParams(dimension_semantics=("arbitrary",) * n_axes,
                                vmem_limit_bytes=VMEM_LIMIT)


def _ffn_kernel(x_ref, pre_ref, post_ref, wg_ref, wu_ref, wd_ref, o_ref, act_ref):
    x = x_ref[...]
    h = _rms(x, pre_ref[...]).astype(BF16)
    for c in range(D_FF // FF_CHUNK):
        cols = slice(c * FF_CHUNK, (c + 1) * FF_CHUNK)
        gate = jnp.dot(h, wg_ref[:, cols], preferred_element_type=F32)
        up = jnp.dot(h, wu_ref[:, cols], preferred_element_type=F32)
        act_ref[:, cols] = (gate * jax.nn.sigmoid(gate) * up).astype(BF16)
    f = jnp.dot(act_ref[...], wd_ref[...], preferred_element_type=F32)
    o_ref[...] = x + FFN_RES * _rms(f, post_ref[...])


def _ffn(x, pre_g, post_g, wg, wu, wd):
    n = x.shape[0]
    tm = min(ROW_TILE, n)
    row = pl.BlockSpec((tm, D_MODEL), lambda i: (i, 0))
    return pl.pallas_call(
        _ffn_kernel,
        grid=(n // tm,),
        in_specs=[row, _const_spec((1, D_MODEL)), _const_spec((1, D_MODEL)),
                  _const_spec((D_MODEL, D_FF)), _const_spec((D_MODEL, D_FF)),
                  _const_spec((D_FF, D_MODEL))],
        out_specs=row,
        out_shape=jax.ShapeDtypeStruct((n, D_MODEL), F32),
        scratch_shapes=[pltpu.VMEM((tm, D_FF), BF16)],
        compiler_params=_params(1),
        name="ffn",
    )(x, pre_g, post_g, wg, wu, wd)


def _inproj_kernel(x_ref, g_ref, w_ref, lng_ref, lnb_ref,
                   q_ref, k_ref, v_ref, kb_ref, vb_ref, u_ref, gn_ref):
    h = _rms(x_ref[...], g_ref[...]).astype(BF16)
    z = jnp.dot(h, w_ref[...], preferred_element_type=F32)
    w = SB_WIDTH
    q_ref[...] = (z[:, 0:w] * (HEAD_DIM ** -0.5)).astype(BF16)
    k = z[:, w:2 * w]
    v = z[:, 2 * w:3 * w]
    k_ref[...] = k
    v_ref[...] = v
    kb_ref[...] = k.astype(BF16)
    vb_ref[...] = v.astype(BF16)
    u_ref[...] = _gelu(z[:, 3 * w:3 * w + SGU_WIDTH])
    ge = _gelu(z[:, 3 * w + SGU_WIDTH:])
    xc = ge - jnp.mean(ge, axis=-1, keepdims=True)
    y = xc * lax.rsqrt(jnp.mean(xc * xc, axis=-1, keepdims=True) + EPS)
    gn_ref[...] = y * lng_ref[...] + lnb_ref[...]


def _inproj(x, g, w_in, ln_g, ln_b):
    n = x.shape[0]
    tm = min(ROW_TILE, n)
    half = pl.BlockSpec((tm, SB_WIDTH), lambda i: (i, 0))
    f32_half = jax.ShapeDtypeStruct((n, SB_WIDTH), F32)
    bf_half = jax.ShapeDtypeStruct((n, SB_WIDTH), BF16)
    return pl.pallas_call(
        _inproj_kernel,
        grid=(n // tm,),
        in_specs=[pl.BlockSpec((tm, D_MODEL), lambda i: (i, 0)), _const_spec((1, D_MODEL)),
                  _const_spec((D_MODEL, D_IN)), _const_spec((1, SGU_WIDTH)),
                  _const_spec((1, SGU_WIDTH))],
        out_specs=[half] * 7,
        out_shape=[bf_half, f32_half, f32_half, bf_half, bf_half, f32_half, f32_half],
        compiler_params=_params(1),
        name="inproj",
    )(x, g, w_in, ln_g, ln_b)


def _pair_rows(x, lane_lo):
    zero = jnp.zeros_like(x)
    return jnp.concatenate([jnp.where(lane_lo, x, zero), jnp.where(lane_lo, zero, x)], axis=0)


def _sb_block(qq, kblk, vblk, carry, suffix_ones, lane_lo, mask):
    z = lax.dot_general(qq, kblk, (((1,), (1,)), ((), ())), preferred_element_type=F32)
    log_1m = -(jnp.maximum(z, 0.0) + jnp.log(1.0 + jnp.exp(-jnp.abs(z))))
    if mask is not None:
        log_1m = jnp.where(mask, log_1m, 0.0)
    hi = log_1m.astype(BF16)
    lo = (log_1m - hi.astype(F32)).astype(BF16)
    sums = jnp.dot(jnp.concatenate([hi, lo], axis=1), suffix_ones, preferred_element_type=F32)
    between = sums[:, :KEY_BLOCK] + carry
    a = jnp.exp(z + log_1m + between)
    if mask is not None:
        a = jnp.where(mask, a, 0.0)
    a = a.astype(BF16)
    t = a.shape[0] // 2
    o = jnp.dot(jnp.concatenate([a[:t], a[t:]], axis=1), _pair_rows(vblk, lane_lo),
                preferred_element_type=F32)
    return o, carry + sums[:, KEY_BLOCK:]


def _suffix_ones():
    r = lax.broadcasted_iota(jnp.int32, (2 * KEY_BLOCK, 2 * KEY_BLOCK), 0) % KEY_BLOCK
    c = lax.broadcasted_iota(jnp.int32, (2 * KEY_BLOCK, 2 * KEY_BLOCK), 1)
    return jnp.where((c >= KEY_BLOCK) | (r > c), 1.0, 0.0).astype(BF16)


def _sb_kernel(q_ref, kd_ref, vd_ref, kp_ref, vp_ref, o_ref, carry_ref, *, tq, n_past_fn):
    lane_lo = lax.broadcasted_iota(jnp.int32, (1, LANES), 1) < HEAD_DIM
    suffix_ones = _suffix_ones()
    row = lax.broadcasted_iota(jnp.int32, (2 * tq, KEY_BLOCK), 0) % tq
    col = lax.broadcasted_iota(jnp.int32, (2 * tq, KEY_BLOCK), 1)
    causal = col < row
    pad = KEY_BLOCK - tq

    def diag_tile(ref, cols):
        x = ref[:, cols]
        if pad:
            x = jnp.concatenate([x, jnp.zeros((pad, LANES), x.dtype)], axis=0)
        return x

    def qq_tile(p):
        return _pair_rows(q_ref[:, p * LANES:(p + 1) * LANES], lane_lo)

    stick = []
    for p in range(N_PAIRS):
        cols = slice(p * LANES, (p + 1) * LANES)
        o, carry = _sb_block(qq_tile(p), diag_tile(kd_ref, cols), diag_tile(vd_ref, cols),
                             jnp.zeros((2 * tq, LANES), F32), suffix_ones, lane_lo, causal)
        o_ref[:, cols] = o
        carry_ref[p] = carry
        stick.append(jnp.max(carry))
    longest = functools.reduce(jnp.maximum, stick)

    def body(state):
        j, _ = state
        start = pl.multiple_of(j * KEY_BLOCK, KEY_BLOCK)
        stick = []
        for p in range(N_PAIRS):
            cols = slice(p * LANES, (p + 1) * LANES)
            o, carry = _sb_block(qq_tile(p), kp_ref[0, pl.ds(start, KEY_BLOCK), cols],
                                 vp_ref[0, pl.ds(start, KEY_BLOCK), cols],
                                 carry_ref[p], suffix_ones, lane_lo, None)
            o_ref[:, cols] += o
            carry_ref[p] = carry
            stick.append(jnp.max(carry))
        return j - 1, functools.reduce(jnp.maximum, stick)

    lax.while_loop(lambda s: (s[0] >= 0) & (s[1] > LOG_STICK_CUTOFF), body,
                   (n_past_fn() - 1, longest))


def _sb_prompt(q, kb, vb, batch, seq):
    nq = seq // KEY_BLOCK
    blk = pl.BlockSpec((KEY_BLOCK, SB_WIDTH), lambda b, i: (b * nq + i, 0))
    past = pl.BlockSpec((1, seq, SB_WIDTH), lambda b, i: (b, 0, 0))
    kb3 = kb.reshape(batch, seq, SB_WIDTH)
    vb3 = vb.reshape(batch, seq, SB_WIDTH)
    return pl.pallas_call(
        functools.partial(_sb_kernel, tq=KEY_BLOCK, n_past_fn=lambda: pl.program_id(1)),
        grid=(batch, nq),
        in_specs=[blk, blk, blk, past, past],
        out_specs=blk,
        out_shape=jax.ShapeDtypeStruct((batch * seq, SB_WIDTH), F32),
        scratch_shapes=[pltpu.VMEM((N_PAIRS, 2 * KEY_BLOCK, LANES), F32)],
        compiler_params=_params(2),
        name="sb_prompt",
    )(q, kb, vb, kb3, vb3)


def _sb_sample(q, kb, vb, past_k, past_v, batch, t_new):
    past_len = past_k.shape[1]
    blk = pl.BlockSpec((t_new, SB_WIDTH), lambda b: (b, 0))
    past = pl.BlockSpec((1, past_len, SB_WIDTH), lambda b: (b, 0, 0))
    return pl.pallas_call(
        functools.partial(_sb_kernel, tq=t_new, n_past_fn=lambda: past_len // KEY_BLOCK),
        grid=(batch,),
        in_specs=[blk, blk, blk, past, past],
        out_specs=blk,
        out_shape=jax.ShapeDtypeStruct((batch * t_new, SB_WIDTH), F32),
        scratch_shapes=[pltpu.VMEM((N_PAIRS, 2 * t_new, LANES), F32)],
        compiler_params=_params(1),
        name="sb_sample",
    )(q, kb, vb, past_k, past_v)


def _mix_kernel(x_ref, osb_ref, u_ref, gn_ref, ws_ref, bias_ref, gsb_ref, gsgu_ref, wout_ref,
                post_ref, y_ref, sgu_ref, *, chunk):
    lane_lo = lax.broadcasted_iota(jnp.int32, (1, LANES), 1) < HEAD_DIM
    tm = x_ref.shape[0]
    for p in range(N_PAIRS):
        cols = slice(p * LANES, (p + 1) * LANES)
        for c in range(tm // chunk):
            rows = slice(c * chunk, (c + 1) * chunk)
            gg = _pair_rows(gn_ref[rows, cols].astype(BF16), lane_lo)
            mixed = jnp.dot(ws_ref[p], gg, preferred_element_type=F32) + bias_ref[:, cols]
            sgu_ref[rows, cols] = u_ref[rows, cols] * mixed
    merged = jnp.concatenate([_rms(osb_ref[...], gsb_ref[...]).astype(BF16),
                              _rms(sgu_ref[...], gsgu_ref[...]).astype(BF16)], axis=1)
    y = jnp.dot(merged, wout_ref[...], preferred_element_type=F32)
    y_ref[...] = x_ref[...] + _rms(y, post_ref[...])


def _mix(x, o_sb, u, gn, ws_pairs, bias, g_sb, g_sgu, w_out, post_g, chunk):
    n = x.shape[0]
    tm = min(ROW_TILE, n)
    row = pl.BlockSpec((tm, D_MODEL), lambda i: (i, 0))
    half = pl.BlockSpec((tm, SB_WIDTH), lambda i: (i, 0))
    return pl.pallas_call(
        functools.partial(_mix_kernel, chunk=chunk),
        grid=(n // tm,),
        in_specs=[row, half, half, half, _const_spec(ws_pairs.shape), _const_spec(bias.shape),
                  _const_spec((1, SB_WIDTH)), _const_spec((1, SGU_WIDTH)),
                  _const_spec((D_MODEL, D_MODEL)), _const_spec((1, D_MODEL))],
        out_specs=row,
        out_shape=jax.ShapeDtypeStruct((n, D_MODEL), F32),
        scratch_shapes=[pltpu.VMEM((tm, SGU_WIDTH), F32)],
        compiler_params=_params(1),
        name="mix",
    )(x, o_sb, u, gn, ws_pairs, bias, g_sb, g_sgu, w_out, post_g)


def _sgu_operands(w_s, b_s, chunk):
    i = jnp.arange(SGU_CHUNK)
    mask = (i[None, :] // 64) <= (i[:, None] // 64)
    w = (w_s * mask[None].astype(w_s.dtype))[:, :chunk, :chunk]
    groups = w.shape[0]
    ws_pairs = w.reshape(groups // 2, 2, chunk, chunk).transpose(0, 2, 1, 3)
    ws_pairs = ws_pairs.reshape(groups // 2, chunk, 2 * chunk).astype(BF16)
    bias = jnp.repeat(b_s[:, :chunk].T, SGU_WIDTH // groups, axis=1)
    return ws_pairs, bias


def _layer(x3, p, past_k, past_v):
    batch, t, _ = x3.shape
    x = x3.reshape(batch * t, D_MODEL)
    x = _ffn(x, p['ffn1_pre_g'], p['ffn1_post_g'], p['ffn1_w_gate'], p['ffn1_w_up'], p['ffn1_w_down'])
    q, k, v, kb, vb, u, gn = _inproj(x, p['mix_pre_g'], p['w_in'], p['sgu_ln_g'], p['sgu_ln_b'])
    if past_k is None:
        o_sb = _sb_prompt(q, kb, vb, batch, t)
        chunk = SGU_CHUNK
    else:
        o_sb = _sb_sample(q, kb, vb, past_k, past_v, batch, t)
        chunk = t
    ws_pairs, bias = _sgu_operands(p['sgu_w_s'], p['sgu_b_s'], chunk)
    x = _mix(x, o_sb, u, gn, ws_pairs, bias, p['g_out_sb'], p['g_out_sgu'], p['w_out'],
             p['mix_post_g'], chunk)
    x = _ffn(x, p['ffn2_pre_g'], p['ffn2_post_g'], p['ffn2_w_gate'], p['ffn2_w_up'], p['ffn2_w_down'])
    heads = SB_WIDTH // HEAD_DIM
    return (x.reshape(batch, t, D_MODEL), k.reshape(batch, t, heads, HEAD_DIM),
            v.reshape(batch, t, heads, HEAD_DIM), gn.reshape(batch, t, heads, HEAD_DIM))


def kernel(x_prompt, x_sample, cache_k_sb, cache_v_sb, ffn1_pre_g, ffn1_post_g, ffn1_w_gate, ffn1_w_up, ffn1_w_down, mix_pre_g, mix_post_g, w_in, sgu_ln_g, sgu_ln_b, sgu_w_s, sgu_b_s, g_out_sb, g_out_sgu, w_out, ffn2_pre_g, ffn2_post_g, ffn2_w_gate, ffn2_w_up, ffn2_w_down):
    depth = w_in.shape[0]
    assert depth == 1
    matrices = dict(ffn1_w_gate=ffn1_w_gate, ffn1_w_up=ffn1_w_up, ffn1_w_down=ffn1_w_down,
                    w_in=w_in, w_out=w_out,
                    ffn2_w_gate=ffn2_w_gate, ffn2_w_up=ffn2_w_up, ffn2_w_down=ffn2_w_down)
    vectors = dict(ffn1_pre_g=ffn1_pre_g, ffn1_post_g=ffn1_post_g, mix_pre_g=mix_pre_g,
                   mix_post_g=mix_post_g, sgu_ln_g=sgu_ln_g, sgu_ln_b=sgu_ln_b,
                   g_out_sb=g_out_sb, g_out_sgu=g_out_sgu,
                   ffn2_pre_g=ffn2_pre_g, ffn2_post_g=ffn2_post_g)
    p = {name: w[0].astype(BF16) for name, w in matrices.items()}
    p.update({name: g[0][None, :] for name, g in vectors.items()})
    p['sgu_w_s'] = sgu_w_s[0]
    p['sgu_b_s'] = sgu_b_s[0]

    dec_batch, past_len = cache_k_sb.shape[1], cache_k_sb.shape[2]
    past_k = cache_k_sb[0].reshape(dec_batch, past_len, SB_WIDTH).astype(BF16)
    past_v = cache_v_sb[0].reshape(dec_batch, past_len, SB_WIDTH).astype(BF16)

    yp, kp, vp, _ = _layer(x_prompt, p, None, None)
    ys, ks, vs, gs = _layer(x_sample, p, past_k, past_v)
    return (yp, ys, kp[None], vp[None], ks[None], vs[None], gs[None])
```

```python
import functools

import jax
import jax.numpy as jnp
from jax import lax
from jax.experimental import pallas as pl
from jax.experimental.pallas import tpu as pltpu

D_MODEL = 1024
D_FF = 2816
SB_WIDTH = 512
SGU_WIDTH = 512
D_IN = 3 * SB_WIDTH + 2 * SGU_WIDTH
HEAD_DIM = 64
SGU_CHUNK = 128
FFN_RES = 0.5
EPS = 1e-6

LANES = 128
N_PAIRS = SB_WIDTH // LANES
KEY_BLOCK = 128
JOINT_PAST_BLOCKS = 2
FF_CHUNK = 1408
ROW_TILE = 512
VMEM_LIMIT = 56 * 1024 * 1024

LOG2_E = 1.4426950408889634
USED_STICK_CUTOFF = 105.0 * LOG2_E

F32 = jnp.float32
BF16 = jnp.bfloat16


def _rms(x, g):
    return x * lax.rsqrt(jnp.mean(x * x, axis=-1, keepdims=True) + EPS) * g


def _gelu(x):
    return 0.5 * x * (1.0 + lax.erf(x * (0.5 ** 0.5)))


def _const_spec(shape):
    return pl.BlockSpec(shape, lambda *_: (0,) * len(shape), pipeline_mode=pl.Buffered(1))


def _params(n_axes):
    return pltpu.CompilerParams(dimension_semantics=("arbitrary",) * n_axes,
                                vmem_limit_bytes=VMEM_LIMIT)


def _ffn_kernel(x_ref, pre_ref, post_ref, wg_ref, wu_ref, wd_ref, o_ref, act_ref):
    x = x_ref[...]
    h = _rms(x, pre_ref[...]).astype(BF16)
    for c in range(D_FF // FF_CHUNK):
        cols = slice(c * FF_CHUNK, (c + 1) * FF_CHUNK)
        gate = jnp.dot(h, wg_ref[:, cols], preferred_element_type=F32)
        up = jnp.dot(h, wu_ref[:, cols], preferred_element_type=F32)
        act_ref[:, cols] = (gate * jax.nn.sigmoid(gate) * up).astype(BF16)
    f = jnp.dot(act_ref[...], wd_ref[...], preferred_element_type=F32)
    o_ref[...] = x + FFN_RES * _rms(f, post_ref[...])


def _ffn(x, pre_g, post_g, wg, wu, wd):
    n = x.shape[0]
    tm = min(ROW_TILE, n)
    row = pl.BlockSpec((tm, D_MODEL), lambda i: (i, 0))
    return pl.pallas_call(
        _ffn_kernel,
        grid=(n // tm,),
        in_specs=[row, _const_spec((1, D_MODEL)), _const_spec((1, D_MODEL)),
                  _const_spec((D_MODEL, D_FF)), _const_spec((D_MODEL, D_FF)),
                  _const_spec((D_FF, D_MODEL))],
        out_specs=row,
        out_shape=jax.ShapeDtypeStruct((n, D_MODEL), F32),
        scratch_shapes=[pltpu.VMEM((tm, D_FF), BF16)],
        compiler_params=_params(1),
        name="ffn",
    )(x, pre_g, post_g, wg, wu, wd)


def _inproj_kernel(x_ref, g_ref, w_ref, lng_ref, lnb_ref,
                   q_ref, k_ref, v_ref, kb_ref, vb_ref, u_ref, gn_ref):
    h = _rms(x_ref[...], g_ref[...]).astype(BF16)
    z = jnp.dot(h, w_ref[...], preferred_element_type=F32)
    w = SB_WIDTH
    q_ref[...] = (z[:, 0:w] * (HEAD_DIM ** -0.5 * LOG2_E)).astype(BF16)
    k = z[:, w:2 * w]
    v = z[:, 2 * w:3 * w]
    k_ref[...] = k
    v_ref[...] = v
    kb_ref[...] = k.astype(BF16)
    vb_ref[...] = v.astype(BF16)
    u_ref[...] = _gelu(z[:, 3 * w:3 * w + SGU_WIDTH])
    ge = _gelu(z[:, 3 * w + SGU_WIDTH:])
    xc = ge - jnp.mean(ge, axis=-1, keepdims=True)
    y = xc * lax.rsqrt(jnp.mean(xc * xc, axis=-1, keepdims=True) + EPS)
    gn_ref[...] = y * lng_ref[...] + lnb_ref[...]


def _inproj(x, g, w_in, ln_g, ln_b):
    n = x.shape[0]
    tm = min(ROW_TILE, n)
    half = pl.BlockSpec((tm, SB_WIDTH), lambda i: (i, 0))
    f32_half = jax.ShapeDtypeStruct((n, SB_WIDTH), F32)
    bf_half = jax.ShapeDtypeStruct((n, SB_WIDTH), BF16)
    return pl.pallas_call(
        _inproj_kernel,
        grid=(n // tm,),
        in_specs=[pl.BlockSpec((tm, D_MODEL), lambda i: (i, 0)), _const_spec((1, D_MODEL)),
                  _const_spec((D_MODEL, D_IN)), _const_spec((1, SGU_WIDTH)),
                  _const_spec((1, SGU_WIDTH))],
        out_specs=[half] * 7,
        out_shape=[bf_half, f32_half, f32_half, bf_half, bf_half, f32_half, f32_half],
        compiler_params=_params(1),
        name="inproj",
    )(x, g, w_in, ln_g, ln_b)


def _pair_rows(x, lane_lo):
    zero = jnp.zeros_like(x)
    return jnp.concatenate([jnp.where(lane_lo, x, zero), jnp.where(lane_lo, zero, x)], axis=0)


def _sb_blocks(qqs, kblks, vblks, carries, suffix_ones, lane_lo, masks):
    n_blocks = len(masks)
    zs = [[lax.dot_general(qq, kblk, (((1,), (1,)), ((), ())), preferred_element_type=F32)
           for kblk in kblks_p] for qq, kblks_p in zip(qqs, kblks)]
    useds, splits = [], []
    for zs_p in zs:
        useds.append([])
        splits.append([])
        for z, mask in zip(zs_p, masks):
            used = jnp.maximum(z, 0.0) + jnp.log(1.0 + jnp.exp2(-jnp.abs(z))) * LOG2_E
            if mask is not None:
                used = jnp.where(mask, used, 0.0)
            hi = used.astype(BF16)
            lo = (used - hi.astype(F32)).astype(BF16)
            useds[-1].append(used)
            splits[-1].append(jnp.concatenate([hi, lo], axis=1))
    sums = [[jnp.dot(s, suffix_ones, preferred_element_type=F32) for s in splits_p]
            for splits_p in splits]
    outs, new_carries = [], []
    for zs_p, useds_p, sums_p, vblks_p, carry in zip(zs, useds, sums, vblks, carries):
        weights = []
        for z, used, s, mask in zip(zs_p, useds_p, sums_p, masks):
            a = jnp.exp2(z - used - (s[:, :KEY_BLOCK] + carry))
            if mask is not None:
                a = jnp.where(mask, a, 0.0)
            a = a.astype(BF16)
            t = a.shape[0] // 2
            weights += [a[:t], a[t:]]
            carry = carry + s[:, KEY_BLOCK:]
        values = jnp.concatenate([_pair_rows(vblk, lane_lo) for vblk in vblks_p], axis=0)
        outs.append(jnp.dot(jnp.concatenate(weights, axis=1), values, preferred_element_type=F32))
        new_carries.append(carry)
    assert len(outs) == N_PAIRS and len(zs[0]) == n_blocks
    return outs, new_carries


def _suffix_ones():
    r = lax.broadcasted_iota(jnp.int32, (2 * KEY_BLOCK, 2 * KEY_BLOCK), 0) % KEY_BLOCK
    c = lax.broadcasted_iota(jnp.int32, (2 * KEY_BLOCK, 2 * KEY_BLOCK), 1)
    return jnp.where((c >= KEY_BLOCK) | (r > c), 1.0, 0.0).astype(BF16)


def _sb_kernel(q_ref, kd_ref, vd_ref, kp_ref, vp_ref, o_ref, carry_ref, *, tq, n_past_fn):
    lane_lo = lax.broadcasted_iota(jnp.int32, (1, LANES), 1) < HEAD_DIM
    suffix_ones = _suffix_ones()
    row = lax.broadcasted_iota(jnp.int32, (2 * tq, KEY_BLOCK), 0) % tq
    col = lax.broadcasted_iota(jnp.int32, (2 * tq, KEY_BLOCK), 1)
    causal = col < row
    pad = KEY_BLOCK - tq

    def diag_tile(ref, cols):
        x = ref[:, cols]
        if pad:
            x = jnp.concatenate([x, jnp.zeros((pad, LANES), x.dtype)], axis=0)
        return x

    pair_cols = [slice(p * LANES, (p + 1) * LANES) for p in range(N_PAIRS)]

    def qq_tiles():
        return [_pair_rows(q_ref[:, cols], lane_lo) for cols in pair_cols]

    def past_tiles(ref, j):
        keys = pl.ds(pl.multiple_of(j * KEY_BLOCK, KEY_BLOCK), KEY_BLOCK)
        return [ref[0, keys, cols] for cols in pair_cols]

    def finish(outs, carries, first):
        for p, cols in enumerate(pair_cols):
            if first:
                o_ref[:, cols] = outs[p]
            else:
                o_ref[:, cols] += outs[p]
            carry_ref[p] = carries[p]

    def first_step(n_joint):
        kblks = [[diag_tile(kd_ref, cols)] for cols in pair_cols]
        vblks = [[diag_tile(vd_ref, cols)] for cols in pair_cols]
        for b in range(n_joint):
            for p, (kt, vt) in enumerate(zip(past_tiles(kp_ref, n_past - 1 - b),
                                             past_tiles(vp_ref, n_past - 1 - b))):
                kblks[p].append(kt)
                vblks[p].append(vt)
        outs, carries = _sb_blocks(qq_tiles(), kblks, vblks,
                                   [jnp.zeros((2 * tq, LANES), F32)] * N_PAIRS,
                                   suffix_ones, lane_lo, [causal] + [None] * n_joint)
        finish(outs, carries, True)

    n_past = n_past_fn()
    if isinstance(n_past, int):
        n_joint = min(JOINT_PAST_BLOCKS, n_past)
        first_step(n_joint)
        next_block = n_past - n_joint - 1
    else:
        pl.when(n_past >= JOINT_PAST_BLOCKS)(lambda: first_step(JOINT_PAST_BLOCKS))
        pl.when(n_past < JOINT_PAST_BLOCKS)(lambda: first_step(0))
        next_block = jnp.where(n_past >= JOINT_PAST_BLOCKS, n_past - JOINT_PAST_BLOCKS, n_past) - 1

    def least_used():
        return jnp.min(functools.reduce(jnp.minimum, [carry_ref[p] for p in range(N_PAIRS)]))

    def body(state):
        j, _ = state
        outs, carries = _sb_blocks(qq_tiles(), [[kt] for kt in past_tiles(kp_ref, j)],
                                   [[vt] for vt in past_tiles(vp_ref, j)],
                                   [carry_ref[p] for p in range(N_PAIRS)],
                                   suffix_ones, lane_lo, [None])
        finish(outs, carries, False)
        return j - 1, least_used()

    lax.while_loop(lambda s: (s[0] >= 0) & (s[1] < USED_STICK_CUTOFF), body,
                   (next_block, least_used()))


def _sb_prompt(q, kb, vb, batch, seq):
    nq = seq // KEY_BLOCK
    blk = pl.BlockSpec((KEY_BLOCK, SB_WIDTH), lambda b, i: (b * nq + i, 0))
    past = pl.BlockSpec((1, seq, SB_WIDTH), lambda b, i: (b, 0, 0))
    kb3 = kb.reshape(batch, seq, SB_WIDTH)
    vb3 = vb.reshape(batch, seq, SB_WIDTH)
    return pl.pallas_call(
        functools.partial(_sb_kernel, tq=KEY_BLOCK, n_past_fn=lambda: pl.program_id(1)),
        grid=(batch, nq),
        in_specs=[blk, blk, blk, past, past],
        out_specs=blk,
        out_shape=jax.ShapeDtypeStruct((batch * seq, SB_WIDTH), F32),
        scratch_shapes=[pltpu.VMEM((N_PAIRS, 2 * KEY_BLOCK, LANES), F32)],
        compiler_params=_params(2),
        name="sb_prompt",
    )(q, kb, vb, kb3, vb3)


def _sb_sample(q, kb, vb, past_k, past_v, batch, t_new):
    past_len = past_k.shape[1]
    blk = pl.BlockSpec((t_new, SB_WIDTH), lambda b: (b, 0))
    past = pl.BlockSpec((1, past_len, SB_WIDTH), lambda b: (b, 0, 0))
    return pl.pallas_call(
        functools.partial(_sb_kernel, tq=t_new, n_past_fn=lambda: past_len // KEY_BLOCK),
        grid=(batch,),
        in_specs=[blk, blk, blk, past, past],
        out_specs=blk,
        out_shape=jax.ShapeDtypeStruct((batch * t_new, SB_WIDTH), F32),
        scratch_shapes=[pltpu.VMEM((N_PAIRS, 2 * t_new, LANES), F32)],
        compiler_params=_params(1),
        name="sb_sample",
    )(q, kb, vb, past_k, past_v)


def _mix_kernel(x_ref, osb_ref, u_ref, gn_ref, ws_ref, bias_ref, gsb_ref, gsgu_ref, wout_ref,
                post_ref, y_ref, sgu_ref, *, chunk):
    lane_lo = lax.broadcasted_iota(jnp.int32, (1, LANES), 1) < HEAD_DIM
    tm = x_ref.shape[0]
    for p in range(N_PAIRS):
        cols = slice(p * LANES, (p + 1) * LANES)
        for c in range(tm // chunk):
            rows = slice(c * chunk, (c + 1) * chunk)
            gg = _pair_rows(gn_ref[rows, cols].astype(BF16), lane_lo)
            mixed = jnp.dot(ws_ref[p], gg, preferred_element_type=F32) + bias_ref[:, cols]
            sgu_ref[rows, cols] = u_ref[rows, cols] * mixed
    merged = jnp.concatenate([_rms(osb_ref[...], gsb_ref[...]).astype(BF16),
                              _rms(sgu_ref[...], gsgu_ref[...]).astype(BF16)], axis=1)
    y = jnp.dot(merged, wout_ref[...], preferred_element_type=F32)
    y_ref[...] = x_ref[...] + _rms(y, post_ref[...])


def _mix(x, o_sb, u, gn, ws_pairs, bias, g_sb, g_sgu, w_out, post_g, chunk):
    n = x.shape[0]
    tm = min(ROW_TILE, n)
    row = pl.BlockSpec((tm, D_MODEL), lambda i: (i, 0))
    half = pl.BlockSpec((tm, SB_WIDTH), lambda i: (i, 0))
    return pl.pallas_call(
        functools.partial(_mix_kernel, chunk=chunk),
        grid=(n // tm,),
        in_specs=[row, half, half, half, _const_spec(ws_pairs.shape), _const_spec(bias.shape),
                  _const_spec((1, SB_WIDTH)), _const_spec((1, SGU_WIDTH)),
                  _const_spec((D_MODEL, D_MODEL)), _const_spec((1, D_MODEL))],
        out_specs=row,
        out_shape=jax.ShapeDtypeStruct((n, D_MODEL), F32),
        scratch_shapes=[pltpu.VMEM((tm, SGU_WIDTH), F32)],
        compiler_params=_params(1),
        name="mix",
    )(x, o_sb, u, gn, ws_pairs, bias, g_sb, g_sgu, w_out, post_g)


def _sgu_operands(w_s, b_s, chunk):
    i = jnp.arange(SGU_CHUNK)
    mask = (i[None, :] // 64) <= (i[:, None] // 64)
    w = (w_s * mask[None].astype(w_s.dtype))[:, :chunk, :chunk]
    groups = w.shape[0]
    ws_pairs = w.reshape(groups // 2, 2, chunk, chunk).transpose(0, 2, 1, 3)
    ws_pairs = ws_pairs.reshape(groups // 2, chunk, 2 * chunk).astype(BF16)
    bias = jnp.repeat(b_s[:, :chunk].T, SGU_WIDTH // groups, axis=1)
    return ws_pairs, bias


def _layer(x3, p, past_k, past_v):
    batch, t, _ = x3.shape
    x = x3.reshape(batch * t, D_MODEL)
    x = _ffn(x, p['ffn1_pre_g'], p['ffn1_post_g'], p['ffn1_w_gate'], p['ffn1_w_up'], p['ffn1_w_down'])
    q, k, v, kb, vb, u, gn = _inproj(x, p['mix_pre_g'], p['w_in'], p['sgu_ln_g'], p['sgu_ln_b'])
    if past_k is None:
        o_sb = _sb_prompt(q, kb, vb, batch, t)
        chunk = SGU_CHUNK
    else:
        o_sb = _sb_sample(q, kb, vb, past_k, past_v, batch, t)
        chunk = t
    ws_pairs, bias = _sgu_operands(p['sgu_w_s'], p['sgu_b_s'], chunk)
    x = _mix(x, o_sb, u, gn, ws_pairs, bias, p['g_out_sb'], p['g_out_sgu'], p['w_out'],
             p['mix_post_g'], chunk)
    x = _ffn(x, p['ffn2_pre_g'], p['ffn2_post_g'], p['ffn2_w_gate'], p['ffn2_w_up'], p['ffn2_w_down'])
    heads = SB_WIDTH // HEAD_DIM
    return (x.reshape(batch, t, D_MODEL), k.reshape(batch, t, heads, HEAD_DIM),
            v.reshape(batch, t, heads, HEAD_DIM), gn.reshape(batch, t, heads, HEAD_DIM))


def kernel(x_prompt, x_sample, cache_k_sb, cache_v_sb, ffn1_pre_g, ffn1_post_g, ffn1_w_gate, ffn1_w_up, ffn1_w_down, mix_pre_g, mix_post_g, w_in, sgu_ln_g, sgu_ln_b, sgu_w_s, sgu_b_s, g_out_sb, g_out_sgu, w_out, ffn2_pre_g, ffn2_post_g, ffn2_w_gate, ffn2_w_up, ffn2_w_down):
    depth = w_in.shape[0]
    assert depth == 1
    matrices = dict(ffn1_w_gate=ffn1_w_gate, ffn1_w_up=ffn1_w_up, ffn1_w_down=ffn1_w_down,
                    w_in=w_in, w_out=w_out,
                    ffn2_w_gate=ffn2_w_gate, ffn2_w_up=ffn2_w_up, ffn2_w_down=ffn2_w_down)
    vectors = dict(ffn1_pre_g=ffn1_pre_g, ffn1_post_g=ffn1_post_g, mix_pre_g=mix_pre_g,
                   mix_post_g=mix_post_g, sgu_ln_g=sgu_ln_g, sgu_ln_b=sgu_ln_b,
                   g_out_sb=g_out_sb, g_out_sgu=g_out_sgu,
                   ffn2_pre_g=ffn2_pre_g, ffn2_post_g=ffn2_post_g)
    p = {name: w[0].astype(BF16) for name, w in matrices.items()}
    p.update({name: g[0][None, :] for name, g in vectors.items()})
    p['sgu_w_s'] = sgu_w_s[0]
    p['sgu_b_s'] = sgu_b_s[0]

    dec_batch, past_len = cache_k_sb.shape[1], cache_k_sb.shape[2]
    past_k = cache_k_sb[0].reshape(dec_batch, past_len, SB_WIDTH).astype(BF16)
    past_v = cache_v_sb[0].reshape(dec_batch, past_len, SB_WIDTH).astype(BF16)

    yp, kp, vp, _ = _layer(x_prompt, p, None, None)
    ys, ks, vs, gs = _layer(x_sample, p, past_k, past_v)
    return (yp, ys, kp[None], vp[None], ks[None], vs[None], gs[None])
```

```python
import functools

import jax
import jax.numpy as jnp
from jax import lax
from jax.experimental import pallas as pl
from jax.experimental.pallas import tpu as pltpu

D_MODEL = 1024
D_FF = 2816
SB_WIDTH = 512
SGU_WIDTH = 512
D_IN = 3 * SB_WIDTH + 2 * SGU_WIDTH
HEAD_DIM = 64
N_HEADS = SB_WIDTH // HEAD_DIM
SGU_CHUNK = 128
SGU_CAUSAL_CHUNK = 64
FFN_RES = 0.5
EPS = 1e-6

LANES = 128
N_PAIRS = SB_WIDTH // LANES
KEY_BLOCK = 128
JOINT_PAST_BLOCKS = 2
FF_CHUNKS = (1536, 1280)
ROW_TILE = 512
VMEM_LIMIT = 56 * 1024 * 1024

LOG2_E = 1.4426950408889634
USED_STICK_CUTOFF = 105.0 * LOG2_E

F32 = jnp.float32
BF16 = jnp.bfloat16


def _rms(x, g):
    return x * lax.rsqrt(jnp.mean(x * x, axis=-1, keepdims=True) + EPS) * g


def _gelu(x):
    return 0.5 * x * (1.0 + lax.erf(x * (0.5 ** 0.5)))


def _const_spec(shape):
    return pl.BlockSpec(shape, lambda *_: (0,) * len(shape), pipeline_mode=pl.Buffered(1))


def _params(n_axes):
    return pltpu.CompilerParams(dimension_semantics=("arbitrary",) * n_axes,
                                vmem_limit_bytes=VMEM_LIMIT)


def _is_sample_step():
    return pl.program_id(0) == 0


def _merged_tile_spec(width, n_prompt_tiles):
    return pl.BlockSpec((ROW_TILE, width), lambda s: ((s + n_prompt_tiles) % (n_prompt_tiles + 1), 0))


def _prompt_tile_spec(width, rows=ROW_TILE):
    return pl.BlockSpec((rows, width), lambda s: (jnp.maximum(s - 1, 0), 0))


def _sample_tile_spec(width, rows=ROW_TILE):
    return pl.BlockSpec((rows, width), lambda s: (0, 0))


def _ffn_kernel(*refs, split_in, split_out):
    refs = list(refs)
    xs = [refs.pop(0) for _ in range(2 if split_in else 1)]
    pre_ref, post_ref, wg_ref, wu_ref, wd_ref = (refs.pop(0) for _ in range(5))
    outs = [refs.pop(0) for _ in range(2 if split_out else 1)]
    act_ref, = refs

    x = jnp.where(_is_sample_step(), xs[1][...], xs[0][...]) if split_in else xs[0][...]
    h = _rms(x, pre_ref[...]).astype(BF16)
    start = 0
    for width in FF_CHUNKS:
        cols = slice(start, start + width)
        gate = jnp.dot(h, wg_ref[:, cols], preferred_element_type=F32)
        up = jnp.dot(h, wu_ref[:, cols], preferred_element_type=F32)
        act_ref[:, cols] = (gate * jax.nn.sigmoid(gate) * up).astype(BF16)
        start += width
    f = jnp.dot(act_ref[...], wd_ref[...], preferred_element_type=F32)
    y = x + FFN_RES * _rms(f, post_ref[...])
    outs[0][...] = y
    if split_out:
        @pl.when(_is_sample_step())
        def _():
            outs[1][...] = outs[0][...]


def _ffn(xs, pre_g, post_g, wg, wu, wd, n_prompt_tiles, split_out):
    split_in = len(xs) == 2
    n_tiles = n_prompt_tiles + 1
    merged = _merged_tile_spec(D_MODEL, n_prompt_tiles)
    prompt = _prompt_tile_spec(D_MODEL)
    sample = _sample_tile_spec(D_MODEL)
    sds = lambda rows: jax.ShapeDtypeStruct((rows, D_MODEL), F32)
    return pl.pallas_call(
        functools.partial(_ffn_kernel, split_in=split_in, split_out=split_out),
        grid=(n_tiles,),
        in_specs=([prompt, sample] if split_in else [merged]) + [
            _const_spec((1, D_MODEL)), _const_spec((1, D_MODEL)),
            _const_spec((D_MODEL, D_FF)), _const_spec((D_MODEL, D_FF)),
            _const_spec((D_FF, D_MODEL))],
        out_specs=[prompt, sample] if split_out else merged,
        out_shape=([sds(n_prompt_tiles * ROW_TILE), sds(ROW_TILE)] if split_out
                   else sds(n_tiles * ROW_TILE)),
        scratch_shapes=[pltpu.VMEM((ROW_TILE, D_FF), BF16)],
        compiler_params=_params(1),
        name="ffn",
    )(*xs, pre_g, post_g, wg, wu, wd)


def _store_heads(ref, x):
    rows = x.shape[0]
    for h in range(N_HEADS):
        ref[pl.ds(h, rows, stride=N_HEADS), :] = x[:, h * HEAD_DIM:(h + 1) * HEAD_DIM]


def _inproj_kernel(x_ref, g_ref, w_ref, lng_ref, lnb_ref,
                   q_ref, kb_ref, vb_ref, u_ref, gn_ref,
                   kp_ref, vp_ref, ks_ref, vs_ref, gs_ref):
    h = _rms(x_ref[...], g_ref[...]).astype(BF16)
    z = jnp.dot(h, w_ref[...], preferred_element_type=F32)
    w = SB_WIDTH
    q_ref[...] = (z[:, 0:w] * (HEAD_DIM ** -0.5 * LOG2_E)).astype(BF16)
    k = z[:, w:2 * w]
    v = z[:, 2 * w:3 * w]
    kb_ref[...] = k.astype(BF16)
    vb_ref[...] = v.astype(BF16)
    u_ref[...] = _gelu(z[:, 3 * w:3 * w + SGU_WIDTH])
    ge = _gelu(z[:, 3 * w + SGU_WIDTH:])
    xc = ge - jnp.mean(ge, axis=-1, keepdims=True)
    y = xc * lax.rsqrt(jnp.mean(xc * xc, axis=-1, keepdims=True) + EPS)
    gn = y * lng_ref[...] + lnb_ref[...]
    gn_ref[...] = gn
    _store_heads(kp_ref, k)
    _store_heads(vp_ref, v)

    @pl.when(_is_sample_step())
    def _():
        ks_ref[...] = kp_ref[...]
        vs_ref[...] = vp_ref[...]
        _store_heads(gs_ref, gn_ref[...])


def _inproj(x, g, w_in, ln_g, ln_b, n_prompt_tiles):
    n = x.shape[0]
    half = _merged_tile_spec(SB_WIDTH, n_prompt_tiles)
    f32_half = jax.ShapeDtypeStruct((n, SB_WIDTH), F32)
    bf_half = jax.ShapeDtypeStruct((n, SB_WIDTH), BF16)
    head_rows = ROW_TILE * N_HEADS
    heads_prompt = _prompt_tile_spec(HEAD_DIM, rows=head_rows)
    heads_sample = _sample_tile_spec(HEAD_DIM, rows=head_rows)
    heads_prompt_sds = jax.ShapeDtypeStruct((n_prompt_tiles * head_rows, HEAD_DIM), F32)
    heads_sample_sds = jax.ShapeDtypeStruct((head_rows, HEAD_DIM), F32)
    return pl.pallas_call(
        _inproj_kernel,
        grid=(n // ROW_TILE,),
        in_specs=[_merged_tile_spec(D_MODEL, n_prompt_tiles), _const_spec((1, D_MODEL)),
                  _const_spec((D_MODEL, D_IN)), _const_spec((1, SGU_WIDTH)),
                  _const_spec((1, SGU_WIDTH))],
        out_specs=[half] * 5 + [heads_prompt] * 2 + [heads_sample] * 3,
        out_shape=[bf_half, bf_half, bf_half, f32_half, f32_half,
                   heads_prompt_sds, heads_prompt_sds,
                   heads_sample_sds, heads_sample_sds, heads_sample_sds],
        compiler_params=_params(1),
        name="inproj",
    )(x, g, w_in, ln_g, ln_b)


def _pair_rows(x, lane_lo):
    zero = jnp.zeros_like(x)
    return jnp.concatenate([jnp.where(lane_lo, x, zero), jnp.where(lane_lo, zero, x)], axis=0)


def _sb_blocks(qqs, kblks, vblks, carries, suffix_ones, lane_lo, masks):
    zs = [[lax.dot_general(qq, kblk, (((1,), (1,)), ((), ())), preferred_element_type=F32)
           for kblk in kblks_p] for qq, kblks_p in zip(qqs, kblks)]
    useds, splits = [], []
    for zs_p in zs:
        useds.append([])
        splits.append([])
        for z, mask in zip(zs_p, masks):
            used = jnp.maximum(z, 0.0) + jnp.log(1.0 + jnp.exp2(-jnp.abs(z))) * LOG2_E
            if mask is not None:
                used = jnp.where(mask, used, 0.0)
            hi = used.astype(BF16)
            lo = (used - hi.astype(F32)).astype(BF16)
            useds[-1].append(used)
            splits[-1].append(jnp.concatenate([hi, lo], axis=1))
    sums = [[jnp.dot(s, suffix_ones, preferred_element_type=F32) for s in splits_p]
            for splits_p in splits]
    outs, new_carries = [], []
    for zs_p, useds_p, sums_p, vblks_p, carry in zip(zs, useds, sums, vblks, carries):
        weights = []
        for z, used, s, mask in zip(zs_p, useds_p, sums_p, masks):
            a = jnp.exp2(z - used - (s[:, :KEY_BLOCK] + carry))
            if mask is not None:
                a = jnp.where(mask, a, 0.0)
            a = a.astype(BF16)
            t = a.shape[0] // 2
            weights += [a[:t], a[t:]]
            carry = carry + s[:, KEY_BLOCK:]
        values = jnp.concatenate([_pair_rows(vblk, lane_lo) for vblk in vblks_p], axis=0)
        outs.append(jnp.dot(jnp.concatenate(weights, axis=1), values, preferred_element_type=F32))
        new_carries.append(carry)
    return outs, new_carries


def _suffix_ones():
    r = lax.broadcasted_iota(jnp.int32, (2 * KEY_BLOCK, 2 * KEY_BLOCK), 0) % KEY_BLOCK
    c = lax.broadcasted_iota(jnp.int32, (2 * KEY_BLOCK, 2 * KEY_BLOCK), 1)
    return jnp.where((c >= KEY_BLOCK) | (r > c), 1.0, 0.0).astype(BF16)


_PAIR_COLS = [slice(p * LANES, (p + 1) * LANES) for p in range(N_PAIRS)]


def _sb_core(q_ref, kd_ref, vd_ref, o_ref, carry_ref, n_past, joint_tiles, walk_tiles):
    tq = q_ref.shape[0]
    lane_lo = lax.broadcasted_iota(jnp.int32, (1, LANES), 1) < HEAD_DIM
    suffix_ones = _suffix_ones()
    row = lax.broadcasted_iota(jnp.int32, (2 * tq, KEY_BLOCK), 0) % tq
    col = lax.broadcasted_iota(jnp.int32, (2 * tq, KEY_BLOCK), 1)
    causal = col < row
    pad = KEY_BLOCK - tq

    def diag_tile(ref, cols):
        x = ref[:, cols]
        if pad:
            x = jnp.concatenate([x, jnp.zeros((pad, LANES), x.dtype)], axis=0)
        return x

    def qq_tiles():
        return [_pair_rows(q_ref[:, cols], lane_lo) for cols in _PAIR_COLS]

    def finish(outs, carries, first):
        for p, cols in enumerate(_PAIR_COLS):
            if first:
                o_ref[:, cols] = outs[p]
            else:
                o_ref[:, cols] += outs[p]
            carry_ref[p] = carries[p]

    def first_step(n_joint):
        kblks = [[diag_tile(kd_ref, cols)] for cols in _PAIR_COLS]
        vblks = [[diag_tile(vd_ref, cols)] for cols in _PAIR_COLS]
        for b in range(n_joint):
            kts, vts = joint_tiles(n_past - 1 - b)
            for p in range(N_PAIRS):
                kblks[p].append(kts[p])
                vblks[p].append(vts[p])
        outs, carries = _sb_blocks(qq_tiles(), kblks, vblks,
                                   [jnp.zeros((2 * tq, LANES), F32)] * N_PAIRS,
                                   suffix_ones, lane_lo, [causal] + [None] * n_joint)
        finish(outs, carries, True)

    if isinstance(n_past, int):
        n_joint = min(JOINT_PAST_BLOCKS, n_past)
        first_step(n_joint)
        next_block = n_past - n_joint - 1
    else:
        pl.when(n_past >= JOINT_PAST_BLOCKS)(lambda: first_step(JOINT_PAST_BLOCKS))
        pl.when(n_past < JOINT_PAST_BLOCKS)(lambda: first_step(0))
        next_block = jnp.where(n_past >= JOINT_PAST_BLOCKS, n_past - JOINT_PAST_BLOCKS, n_past) - 1

    def least_used():
        return jnp.min(functools.reduce(jnp.minimum, [carry_ref[p] for p in range(N_PAIRS)]))

    def body(state):
        j, _ = state
        kts, vts = walk_tiles(j)
        outs, carries = _sb_blocks(qq_tiles(), [[kt] for kt in kts], [[vt] for vt in vts],
                                   [carry_ref[p] for p in range(N_PAIRS)],
                                   suffix_ones, lane_lo, [None])
        finish(outs, carries, False)
        return j - 1, least_used()

    lax.while_loop(lambda s: (s[0] >= 0) & (s[1] < USED_STICK_CUTOFF), body,
                   (next_block, least_used()))


def _sb_prompt_kernel(q_ref, kd_ref, vd_ref, kp_ref, vp_ref, o_ref, carry_ref):
    def tiles(j):
        keys = pl.ds(pl.multiple_of(j * KEY_BLOCK, KEY_BLOCK), KEY_BLOCK)
        return ([kp_ref[keys, cols] for cols in _PAIR_COLS],
                [vp_ref[keys, cols] for cols in _PAIR_COLS])

    _sb_core(q_ref, kd_ref, vd_ref, o_ref, carry_ref, pl.program_id(1), tiles, tiles)


def _sb_prompt(q, kb, vb, batch, seq):
    nq = seq // KEY_BLOCK
    blk = pl.BlockSpec((KEY_BLOCK, SB_WIDTH), lambda b, i: (b * nq + i, 0))
    past = pl.BlockSpec((seq, SB_WIDTH), lambda b, i: (b, 0))
    return pl.pallas_call(
        _sb_prompt_kernel,
        grid=(batch, nq),
        in_specs=[blk, blk, blk, past, past],
        out_specs=blk,
        out_shape=jax.ShapeDtypeStruct((batch * seq, SB_WIDTH), F32),
        scratch_shapes=[pltpu.VMEM((N_PAIRS, 2 * KEY_BLOCK, LANES), F32)],
        compiler_params=_params(2),
        name="sb_prompt",
    )(q, kb, vb, kb, vb)


def _sb_sample_kernel(q_ref, kd_ref, vd_ref, kr_ref, vr_ref, kc_hbm, vc_hbm, o_ref,
                      carry_ref, kbuf_ref, vbuf_ref, sem_ref, *, past_len):
    n_past = past_len // KEY_BLOCK
    n_recent = kr_ref.shape[0] // KEY_BLOCK
    b = pl.program_id(0)

    def recent_tiles(j):
        keys = pl.ds((j - (n_past - n_recent)) * KEY_BLOCK, KEY_BLOCK)
        return ([kr_ref[keys, cols] for cols in _PAIR_COLS],
                [vr_ref[keys, cols] for cols in _PAIR_COLS])

    def cache_tiles(j):
        rows = pl.ds(pl.multiple_of((b * past_len + j * KEY_BLOCK) * N_HEADS, KEY_BLOCK * N_HEADS),
                     KEY_BLOCK * N_HEADS)
        copies = [pltpu.make_async_copy(hbm.at[rows, :], buf, sem_ref.at[s])
                  for s, (hbm, buf) in enumerate(((kc_hbm, kbuf_ref), (vc_hbm, vbuf_ref)))]
        for c in copies:
            c.start()
        for c in copies:
            c.wait()

        def pair_tile(buf, p):
            heads = [buf[pl.ds(2 * p + e, KEY_BLOCK, stride=N_HEADS), :] for e in range(2)]
            return jnp.concatenate(heads, axis=1).astype(BF16)

        return ([pair_tile(kbuf_ref, p) for p in range(N_PAIRS)],
                [pair_tile(vbuf_ref, p) for p in range(N_PAIRS)])

    assert n_recent == JOINT_PAST_BLOCKS
    _sb_core(q_ref, kd_ref, vd_ref, o_ref, carry_ref, n_past, recent_tiles, cache_tiles)


def _sb_sample(q, kb, vb, recent_k, recent_v, cache_k, cache_v, batch, t_new, first_row):
    past_len = cache_k.shape[0] // (batch * N_HEADS)
    n_recent_rows = recent_k.shape[0] // batch
    blk = pl.BlockSpec((t_new, SB_WIDTH), lambda b: (first_row // t_new + b, 0))
    recent = pl.BlockSpec((n_recent_rows, SB_WIDTH), lambda b: (b, 0))
    hbm = pl.BlockSpec(memory_space=pl.ANY)
    return pl.pallas_call(
        functools.partial(_sb_sample_kernel, past_len=past_len),
        grid=(batch,),
        in_specs=[blk, blk, blk, recent, recent, hbm, hbm],
        out_specs=pl.BlockSpec((t_new, SB_WIDTH), lambda b: (b, 0)),
        out_shape=jax.ShapeDtypeStruct((batch * t_new, SB_WIDTH), F32),
        scratch_shapes=[pltpu.VMEM((N_PAIRS, 2 * t_new, LANES), F32),
                        pltpu.VMEM((KEY_BLOCK * N_HEADS, HEAD_DIM), F32),
                        pltpu.VMEM((KEY_BLOCK * N_HEADS, HEAD_DIM), F32),
                        pltpu.SemaphoreType.DMA((2,))],
        compiler_params=_params(1),
        name="sb_sample",
    )(q, kb, vb, recent_k, recent_v, cache_k, cache_v)


def _mix_kernel(x_ref, op_ref, os_ref, u_ref, gn_ref, ws_ref, bias_ref, gsb_ref, gsgu_ref, wout_ref,
                post_ref, y_ref, sgu_ref):
    lane_lo = lax.broadcasted_iota(jnp.int32, (1, LANES), 1) < HEAD_DIM
    for p in range(N_PAIRS):
        cols = slice(p * LANES, (p + 1) * LANES)
        for c in range(ROW_TILE // SGU_CHUNK):
            rows = slice(c * SGU_CHUNK, (c + 1) * SGU_CHUNK)
            gg = _pair_rows(gn_ref[rows, cols].astype(BF16), lane_lo)
            mixed = jnp.dot(ws_ref[0, p], gg, preferred_element_type=F32) + bias_ref[0, :, cols]
            sgu_ref[rows, cols] = u_ref[rows, cols] * mixed
    o_sb = jnp.where(_is_sample_step(), os_ref[...], op_ref[...])
    merged = jnp.concatenate([_rms(o_sb, gsb_ref[...]).astype(BF16),
                              _rms(sgu_ref[...], gsgu_ref[...]).astype(BF16)], axis=1)
    y = jnp.dot(merged, wout_ref[...], preferred_element_type=F32)
    y_ref[...] = x_ref[...] + _rms(y, post_ref[...])


def _mix(x, o_prompt, o_sample, u, gn, ws_pairs, bias, g_sb, g_sgu, w_out, post_g, n_prompt_tiles):
    n = x.shape[0]
    row = _merged_tile_spec(D_MODEL, n_prompt_tiles)
    half = _merged_tile_spec(SB_WIDTH, n_prompt_tiles)
    which = lambda s: jnp.where(s == 0, 1, 0)
    return pl.pallas_call(
        _mix_kernel,
        grid=(n // ROW_TILE,),
        in_specs=[row, _prompt_tile_spec(SB_WIDTH), _sample_tile_spec(SB_WIDTH), half, half,
                  pl.BlockSpec((1,) + ws_pairs.shape[1:], lambda s: (which(s), 0, 0, 0)),
                  pl.BlockSpec((1,) + bias.shape[1:], lambda s: (which(s), 0, 0)),
                  _const_spec((1, SB_WIDTH)), _const_spec((1, SGU_WIDTH)),
                  _const_spec((D_MODEL, D_MODEL)), _const_spec((1, D_MODEL))],
        out_specs=row,
        out_shape=jax.ShapeDtypeStruct((n, D_MODEL), F32),
        scratch_shapes=[pltpu.VMEM((ROW_TILE, SGU_WIDTH), F32)],
        compiler_params=_params(1),
        name="mix",
    )(x, o_prompt, o_sample, u, gn, ws_pairs, bias, g_sb, g_sgu, w_out, post_g)


def _sgu_operands(w_s, b_s, t_new):
    i = jnp.arange(SGU_CHUNK)
    mask = (i[None, :] // SGU_CAUSAL_CHUNK) <= (i[:, None] // SGU_CAUSAL_CHUNK)
    w_prompt = w_s * mask[None].astype(w_s.dtype)
    reps = SGU_CHUNK // t_new
    corner = w_prompt[:, :t_new, :t_new]
    eye = jnp.eye(reps, dtype=w_s.dtype)
    w_sample = jnp.einsum('ab,gij->gaibj', eye, corner).reshape(w_s.shape)
    groups = w_s.shape[0]

    def pairs(w):
        w = w.reshape(groups // 2, 2, SGU_CHUNK, SGU_CHUNK).transpose(0, 2, 1, 3)
        return w.reshape(groups // 2, SGU_CHUNK, 2 * SGU_CHUNK)

    def bias(b):
        return jnp.repeat(b.T, SGU_WIDTH // groups, axis=1)

    ws = jnp.stack([pairs(w_prompt), pairs(w_sample)]).astype(BF16)
    return ws, jnp.stack([bias(b_s), bias(jnp.tile(b_s[:, :t_new], (1, reps)))])


def kernel(x_prompt, x_sample, cache_k_sb, cache_v_sb, ffn1_pre_g, ffn1_post_g, ffn1_w_gate, ffn1_w_up, ffn1_w_down, mix_pre_g, mix_post_g, w_in, sgu_ln_g, sgu_ln_b, sgu_w_s, sgu_b_s, g_out_sb, g_out_sgu, w_out, ffn2_pre_g, ffn2_post_g, ffn2_w_gate, ffn2_w_up, ffn2_w_down):
    depth, batch, seq = w_in.shape[0], x_prompt.shape[0], x_prompt.shape[1]
    dec_batch, t_new, past_len = x_sample.shape[0], x_sample.shape[1], cache_k_sb.shape[2]
    assert depth == 1 and dec_batch * t_new == ROW_TILE and (batch * seq) % ROW_TILE == 0
    assert SGU_CHUNK % t_new == 0 and past_len % KEY_BLOCK == 0 and seq % KEY_BLOCK == 0
    n_prompt_tiles = batch * seq // ROW_TILE

    mat = lambda w: w[0].astype(BF16)
    vec = lambda g: g[0][None, :]

    x = _ffn([x_prompt.reshape(batch * seq, D_MODEL), x_sample.reshape(ROW_TILE, D_MODEL)],
             vec(ffn1_pre_g), vec(ffn1_post_g), mat(ffn1_w_gate), mat(ffn1_w_up), mat(ffn1_w_down),
             n_prompt_tiles, split_out=False)
    q, kb, vb, u, gn, k_p, v_p, k_s, v_s, g_s = _inproj(
        x, vec(mix_pre_g), mat(w_in), vec(sgu_ln_g), vec(sgu_ln_b), n_prompt_tiles)

    o_prompt = _sb_prompt(q, kb, vb, batch, seq)
    n_recent = JOINT_PAST_BLOCKS * KEY_BLOCK
    recent = lambda c: c[0, :, past_len - n_recent:].reshape(dec_batch * n_recent, SB_WIDTH).astype(BF16)
    whole = lambda c: c[0].reshape(dec_batch * past_len * N_HEADS, HEAD_DIM)
    o_sample = _sb_sample(q, kb, vb, recent(cache_k_sb), recent(cache_v_sb),
                          whole(cache_k_sb), whole(cache_v_sb), dec_batch, t_new, batch * seq)

    ws_pairs, bias = _sgu_operands(sgu_w_s[0], sgu_b_s[0], t_new)
    x = _mix(x, o_prompt, o_sample, u, gn, ws_pairs, bias, vec(g_out_sb), vec(g_out_sgu),
             mat(w_out), vec(mix_post_g), n_prompt_tiles)
    y_p, y_s = _ffn([x], vec(ffn2_pre_g), vec(ffn2_post_g), mat(ffn2_w_gate), mat(ffn2_w_up),
                    mat(ffn2_w_down), n_prompt_tiles, split_out=True)

    heads_p = (depth, batch, seq, N_HEADS, HEAD_DIM)
    heads_s = (depth, dec_batch, t_new, N_HEADS, HEAD_DIM)
    return (y_p.reshape(batch, seq, D_MODEL), y_s.reshape(dec_batch, t_new, D_MODEL),
            k_p.reshape(heads_p), v_p.reshape(heads_p),
            k_s.reshape(heads_s), v_s.reshape(heads_s), g_s.reshape(heads_s))
```

```python
import functools

import jax
import jax.numpy as jnp
from jax import lax
from jax.experimental import pallas as pl
from jax.experimental.pallas import tpu as pltpu

D_MODEL = 1024
D_FF = 2816
SB_WIDTH = 512
SGU_WIDTH = 512
HEAD_DIM = 64
N_HEADS = SB_WIDTH // HEAD_DIM
SGU_CHUNK = 128
SGU_CAUSAL_CHUNK = 64
FFN_RES = 0.5
EPS = 1e-6

LANES = 128
N_PAIRS = SB_WIDTH // LANES
KEY_BLOCK = 128
JOINT_PAST_BLOCKS = 2
FF_CHUNKS = (1536, 1280)
ROW_TILE = 512
VMEM_LIMIT = 56 * 1024 * 1024

LOG2_E = 1.4426950408889634
USED_STICK_CUTOFF = 105.0 * LOG2_E

F32 = jnp.float32
BF16 = jnp.bfloat16


def _rms(x, g):
    return x * lax.rsqrt(jnp.mean(x * x, axis=-1, keepdims=True) + EPS) * g


def _gelu(x):
    return 0.5 * x * (1.0 + lax.erf(x * (0.5 ** 0.5)))


def _const_spec(shape):
    return pl.BlockSpec(shape, lambda *_: (0,) * len(shape), pipeline_mode=pl.Buffered(1))


def _params(n_axes):
    return pltpu.CompilerParams(dimension_semantics=("arbitrary",) * n_axes,
                                vmem_limit_bytes=VMEM_LIMIT)


def _is_sample_step():
    return pl.program_id(0) == 0


def _merged_tile_spec(width, n_prompt_tiles):
    return pl.BlockSpec((ROW_TILE, width), lambda s: ((s + n_prompt_tiles) % (n_prompt_tiles + 1), 0))


def _prompt_tile_spec(width, rows=ROW_TILE):
    return pl.BlockSpec((rows, width), lambda s: (jnp.maximum(s - 1, 0), 0))


def _sample_tile_spec(width, rows=ROW_TILE):
    return pl.BlockSpec((rows, width), lambda s: (0, 0))


def _ffn_kernel(*refs, split_in, split_out):
    refs = list(refs)
    xs = [refs.pop(0) for _ in range(2 if split_in else 1)]
    pre_ref, post_ref, wg_ref, wu_ref, wd_ref = (refs.pop(0) for _ in range(5))
    outs = [refs.pop(0) for _ in range(2 if split_out else 1)]
    act_ref, = refs

    x = jnp.where(_is_sample_step(), xs[1][...], xs[0][...]) if split_in else xs[0][...]
    h = _rms(x, pre_ref[...]).astype(BF16)
    start = 0
    for width in FF_CHUNKS:
        cols = slice(start, start + width)
        gate = jnp.dot(h, wg_ref[:, cols], preferred_element_type=F32)
        up = jnp.dot(h, wu_ref[:, cols], preferred_element_type=F32)
        act_ref[:, cols] = (gate * jax.nn.sigmoid(gate) * up).astype(BF16)
        start += width
    f = jnp.dot(act_ref[...], wd_ref[...], preferred_element_type=F32)
    y = x + FFN_RES * _rms(f, post_ref[...])
    outs[0][...] = y
    if split_out:
        @pl.when(_is_sample_step())
        def _():
            outs[1][...] = outs[0][...]


def _ffn(xs, pre_g, post_g, wg, wu, wd, n_prompt_tiles, split_out):
    split_in = len(xs) == 2
    n_tiles = n_prompt_tiles + 1
    merged = _merged_tile_spec(D_MODEL, n_prompt_tiles)
    prompt = _prompt_tile_spec(D_MODEL)
    sample = _sample_tile_spec(D_MODEL)
    sds = lambda rows: jax.ShapeDtypeStruct((rows, D_MODEL), F32)
    return pl.pallas_call(
        functools.partial(_ffn_kernel, split_in=split_in, split_out=split_out),
        grid=(n_tiles,),
        in_specs=([prompt, sample] if split_in else [merged]) + [
            _const_spec((1, D_MODEL)), _const_spec((1, D_MODEL)),
            _const_spec((D_MODEL, D_FF)), _const_spec((D_MODEL, D_FF)),
            _const_spec((D_FF, D_MODEL))],
        out_specs=[prompt, sample] if split_out else merged,
        out_shape=([sds(n_prompt_tiles * ROW_TILE), sds(ROW_TILE)] if split_out
                   else sds(n_tiles * ROW_TILE)),
        scratch_shapes=[pltpu.VMEM((ROW_TILE, D_FF), BF16)],
        compiler_params=_params(1),
        name="ffn",
    )(*xs, pre_g, post_g, wg, wu, wd)


def _inproj_kernel(x_ref, g_ref, wqug_ref, wkvt_ref, lng_ref, lnb_ref,
                   q_ref, u_ref, gn_ref, kt3_ref, vt3_ref, ktp_ref, vtp_ref, kts_ref, vts_ref):
    h = _rms(x_ref[...], g_ref[...]).astype(BF16)
    z = jnp.dot(h, wqug_ref[...], preferred_element_type=F32)
    kvt = lax.dot_general(wkvt_ref[...], h, (((1,), (1,)), ((), ())), preferred_element_type=F32)
    w = SB_WIDTH
    q_ref[...] = (z[:, 0:w] * (HEAD_DIM ** -0.5 * LOG2_E)).astype(BF16)
    u_ref[...] = _gelu(z[:, w:w + SGU_WIDTH])
    ge = _gelu(z[:, w + SGU_WIDTH:])
    xc = ge - jnp.mean(ge, axis=-1, keepdims=True)
    y = xc * lax.rsqrt(jnp.mean(xc * xc, axis=-1, keepdims=True) + EPS)
    gn_ref[...] = y * lng_ref[...] + lnb_ref[...]
    for t_ref, t3_ref, rows in ((ktp_ref, kt3_ref, slice(0, w)), (vtp_ref, vt3_ref, slice(w, 2 * w))):
        t = kvt[rows]
        t_ref[...] = t
        for c in range(ROW_TILE // KEY_BLOCK):
            t3_ref[c] = t[:, c * KEY_BLOCK:(c + 1) * KEY_BLOCK].astype(BF16)

    @pl.when(_is_sample_step())
    def _():
        kts_ref[...] = ktp_ref[...]
        vts_ref[...] = vtp_ref[...]


def _inproj(x, g, w_qug, w_kvt, ln_g, ln_b, n_prompt_tiles, prompt_tiles_per_row):
    n = x.shape[0]
    half = _merged_tile_spec(SB_WIDTH, n_prompt_tiles)
    blocks_per_tile = ROW_TILE // KEY_BLOCK
    key_blocks = pl.BlockSpec((blocks_per_tile, SB_WIDTH, KEY_BLOCK),
                              lambda s: ((s + n_prompt_tiles) % (n_prompt_tiles + 1), 0, 0))

    def prompt_t_index(s):
        tile = jnp.maximum(s - 1, 0)
        return tile // prompt_tiles_per_row, tile % prompt_tiles_per_row

    prompt_t = pl.BlockSpec((SB_WIDTH, ROW_TILE), prompt_t_index)
    prompt_t_sds = jax.ShapeDtypeStruct(
        (n_prompt_tiles // prompt_tiles_per_row * SB_WIDTH, prompt_tiles_per_row * ROW_TILE), F32)
    sample_t_sds = jax.ShapeDtypeStruct((SB_WIDTH, ROW_TILE), F32)
    f32_half = jax.ShapeDtypeStruct((n, SB_WIDTH), F32)
    key_blocks_sds = jax.ShapeDtypeStruct((n // KEY_BLOCK, SB_WIDTH, KEY_BLOCK), BF16)
    return pl.pallas_call(
        _inproj_kernel,
        grid=(n // ROW_TILE,),
        in_specs=[_merged_tile_spec(D_MODEL, n_prompt_tiles), _const_spec((1, D_MODEL)),
                  _const_spec(w_qug.shape), _const_spec(w_kvt.shape), _const_spec((1, SGU_WIDTH)),
                  _const_spec((1, SGU_WIDTH))],
        out_specs=[half] * 3 + [key_blocks] * 2 + [prompt_t] * 2 + [_sample_tile_spec(ROW_TILE, SB_WIDTH)] * 2,
        out_shape=[jax.ShapeDtypeStruct((n, SB_WIDTH), BF16), f32_half, f32_half,
                   key_blocks_sds, key_blocks_sds, prompt_t_sds, prompt_t_sds,
                   sample_t_sds, sample_t_sds],
        compiler_params=_params(1),
        name="inproj",
    )(x, g, w_qug, w_kvt, ln_g, ln_b)


def _pair_rows(x, lane_lo):
    zero = jnp.zeros_like(x)
    return jnp.concatenate([jnp.where(lane_lo, x, zero), jnp.where(lane_lo, zero, x)], axis=0)


def _pair_cols(xt, row_lo):
    zero = jnp.zeros_like(xt)
    return jnp.concatenate([jnp.where(row_lo, xt, zero), jnp.where(row_lo, zero, xt)], axis=1)


def _sb_blocks(qqs, kts, vts, carries, suffix_ones, row_lo, masks):
    zs = [[jnp.dot(qq, kt, preferred_element_type=F32) for kt in kts_p]
          for qq, kts_p in zip(qqs, kts)]
    useds, splits = [], []
    for zs_p in zs:
        useds.append([])
        splits.append([])
        for z, mask in zip(zs_p, masks):
            used = jnp.maximum(z, 0.0) + jnp.log(1.0 + jnp.exp2(-jnp.abs(z))) * LOG2_E
            if mask is not None:
                used = jnp.where(mask, used, 0.0)
            hi = used.astype(BF16)
            lo = (used - hi.astype(F32)).astype(BF16)
            useds[-1].append(used)
            splits[-1].append(jnp.concatenate([hi, lo], axis=1))
    sums = [[jnp.dot(s, suffix_ones, preferred_element_type=F32) for s in splits_p]
            for splits_p in splits]
    outs, new_carries = [], []
    for zs_p, useds_p, sums_p, vts_p, carry in zip(zs, useds, sums, vts, carries):
        weights = []
        for z, used, s, mask in zip(zs_p, useds_p, sums_p, masks):
            a = jnp.exp2(z - used - (s[:, :KEY_BLOCK] + carry))
            if mask is not None:
                a = jnp.where(mask, a, 0.0)
            a = a.astype(BF16)
            t = a.shape[0] // 2
            weights += [a[:t], a[t:]]
            carry = carry + s[:, KEY_BLOCK:]
        values_t = jnp.concatenate([_pair_cols(vt, row_lo) for vt in vts_p], axis=1)
        outs.append(lax.dot_general(jnp.concatenate(weights, axis=1), values_t,
                                    (((1,), (1,)), ((), ())), preferred_element_type=F32))
        new_carries.append(carry)
    return outs, new_carries


def _suffix_ones():
    r = lax.broadcasted_iota(jnp.int32, (2 * KEY_BLOCK, 2 * KEY_BLOCK), 0) % KEY_BLOCK
    c = lax.broadcasted_iota(jnp.int32, (2 * KEY_BLOCK, 2 * KEY_BLOCK), 1)
    return jnp.where((c >= KEY_BLOCK) | (r > c), 1.0, 0.0).astype(BF16)


_PAIR_SLICES = [slice(p * LANES, (p + 1) * LANES) for p in range(N_PAIRS)]


def _sb_core(q_ref, o_ref, carry_ref, n_past, diag_key_offset, diag_tiles, joint_tiles, walk_tiles):
    tq = q_ref.shape[0]
    lane_lo = lax.broadcasted_iota(jnp.int32, (1, LANES), 1) < HEAD_DIM
    row_lo = lax.broadcasted_iota(jnp.int32, (LANES, 1), 0) < HEAD_DIM
    suffix_ones = _suffix_ones()
    row = lax.broadcasted_iota(jnp.int32, (2 * tq, KEY_BLOCK), 0) % tq
    col = lax.broadcasted_iota(jnp.int32, (2 * tq, KEY_BLOCK), 1) - diag_key_offset
    causal = (col >= 0) & (col < row)

    def qq_tiles():
        return [_pair_rows(q_ref[:, cols], lane_lo) for cols in _PAIR_SLICES]

    def finish(outs, carries, first):
        for p, cols in enumerate(_PAIR_SLICES):
            if first:
                o_ref[:, cols] = outs[p]
            else:
                o_ref[:, cols] += outs[p]
            carry_ref[p] = carries[p]

    def first_step(n_joint):
        kts, vts = diag_tiles()
        kts, vts = [[t] for t in kts], [[t] for t in vts]
        for b in range(n_joint):
            kj, vj = joint_tiles(n_past - 1 - b)
            for p in range(N_PAIRS):
                kts[p].append(kj[p])
                vts[p].append(vj[p])
        outs, carries = _sb_blocks(qq_tiles(), kts, vts,
                                   [jnp.zeros((2 * tq, LANES), F32)] * N_PAIRS,
                                   suffix_ones, row_lo, [causal] + [None] * n_joint)
        finish(outs, carries, True)

    if isinstance(n_past, int):
        n_joint = min(JOINT_PAST_BLOCKS, n_past)
        first_step(n_joint)
        next_block = n_past - n_joint - 1
    else:
        pl.when(n_past >= JOINT_PAST_BLOCKS)(lambda: first_step(JOINT_PAST_BLOCKS))
        pl.when(n_past < JOINT_PAST_BLOCKS)(lambda: first_step(0))
        next_block = jnp.where(n_past >= JOINT_PAST_BLOCKS, n_past - JOINT_PAST_BLOCKS, n_past) - 1

    def least_used():
        return jnp.min(functools.reduce(jnp.minimum, [carry_ref[p] for p in range(N_PAIRS)]))

    def body(state):
        j, _ = state
        kts, vts = walk_tiles(j)
        outs, carries = _sb_blocks(qq_tiles(), [[t] for t in kts], [[t] for t in vts],
                                   [carry_ref[p] for p in range(N_PAIRS)],
                                   suffix_ones, row_lo, [None])
        finish(outs, carries, False)
        return j - 1, least_used()

    lax.while_loop(lambda s: (s[0] >= 0) & (s[1] < USED_STICK_CUTOFF), body,
                   (next_block, least_used()))


def _sb_prompt_kernel(q_ref, kt_ref, vt_ref, o_ref, carry_ref):
    def tiles(j):
        return ([kt_ref[j, rows, :] for rows in _PAIR_SLICES],
                [vt_ref[j, rows, :] for rows in _PAIR_SLICES])

    i = pl.program_id(1)
    _sb_core(q_ref, o_ref, carry_ref, i, 0, lambda: tiles(i), tiles, tiles)


def _sb_prompt(q, kt3, vt3, batch, seq):
    nq = seq // KEY_BLOCK
    blk = pl.BlockSpec((KEY_BLOCK, SB_WIDTH), lambda b, i: (b * nq + i, 0))
    keys = pl.BlockSpec((nq, SB_WIDTH, KEY_BLOCK), lambda b, i: (b, 0, 0))
    return pl.pallas_call(
        _sb_prompt_kernel,
        grid=(batch, nq),
        in_specs=[blk, keys, keys],
        out_specs=blk,
        out_shape=jax.ShapeDtypeStruct((batch * seq, SB_WIDTH), F32),
        scratch_shapes=[pltpu.VMEM((N_PAIRS, 2 * KEY_BLOCK, LANES), F32)],
        compiler_params=_params(2),
        name="sb_prompt",
    )(q, kt3, vt3)


def _sb_sample_kernel(q_ref, kd_ref, vd_ref, kr_ref, vr_ref, kc_hbm, vc_hbm, o_ref,
                      carry_ref, kbuf_ref, vbuf_ref, sem_ref, *, past_len):
    tq = q_ref.shape[0]
    n_past = past_len // KEY_BLOCK
    n_recent = kr_ref.shape[2] // KEY_BLOCK
    b = pl.program_id(0)

    def diag_tiles():
        return ([kd_ref[0, rows, :] for rows in _PAIR_SLICES],
                [vd_ref[0, rows, :] for rows in _PAIR_SLICES])

    def recent_tiles(j):
        first = (j - (n_past - n_recent)) * KEY_BLOCK
        keys = slice(first, first + KEY_BLOCK)
        return ([kr_ref[0, rows, keys].astype(BF16) for rows in _PAIR_SLICES],
                [vr_ref[0, rows, keys].astype(BF16) for rows in _PAIR_SLICES])

    def cache_tiles(j):
        keys = pl.ds(pl.multiple_of(j * KEY_BLOCK, KEY_BLOCK), KEY_BLOCK)
        copies = [pltpu.make_async_copy(hbm.at[b, :, keys], buf, sem_ref.at[s])
                  for s, (hbm, buf) in enumerate(((kc_hbm, kbuf_ref), (vc_hbm, vbuf_ref)))]
        for c in copies:
            c.start()
        for c in copies:
            c.wait()
        return ([kbuf_ref[rows, :].astype(BF16) for rows in _PAIR_SLICES],
                [vbuf_ref[rows, :].astype(BF16) for rows in _PAIR_SLICES])

    assert n_recent == JOINT_PAST_BLOCKS
    _sb_core(q_ref, o_ref, carry_ref, n_past, (b % (KEY_BLOCK // tq)) * tq,
             diag_tiles, recent_tiles, cache_tiles)


def _sb_sample(q, kt3, vt3, cache_kt, cache_vt, t_new, first_row):
    batch, _, past_len = cache_kt.shape
    n_recent = JOINT_PAST_BLOCKS * KEY_BLOCK
    seqs_per_block = KEY_BLOCK // t_new
    new = pl.BlockSpec((1, SB_WIDTH, KEY_BLOCK),
                       lambda b: (first_row // KEY_BLOCK + b // seqs_per_block, 0, 0))
    recent = pl.BlockSpec((1, SB_WIDTH, n_recent), lambda b: (b, 0, past_len // n_recent - 1))
    hbm = pl.BlockSpec(memory_space=pl.ANY)
    return pl.pallas_call(
        functools.partial(_sb_sample_kernel, past_len=past_len),
        grid=(batch,),
        in_specs=[pl.BlockSpec((t_new, SB_WIDTH), lambda b: (first_row // t_new + b, 0)),
                  new, new, recent, recent, hbm, hbm],
        out_specs=pl.BlockSpec((t_new, SB_WIDTH), lambda b: (b, 0)),
        out_shape=jax.ShapeDtypeStruct((batch * t_new, SB_WIDTH), F32),
        scratch_shapes=[pltpu.VMEM((N_PAIRS, 2 * t_new, LANES), F32),
                        pltpu.VMEM((SB_WIDTH, KEY_BLOCK), F32),
                        pltpu.VMEM((SB_WIDTH, KEY_BLOCK), F32),
                        pltpu.SemaphoreType.DMA((2,))],
        compiler_params=_params(1),
        name="sb_sample",
    )(q, kt3, vt3, cache_kt, cache_vt, cache_kt, cache_vt)


def _mix_kernel(x_ref, op_ref, os_ref, u_ref, gn_ref, ws_ref, bias_ref, gsb_ref, gsgu_ref, wout_ref,
                post_ref, y_ref, sgu_ref):
    lane_lo = lax.broadcasted_iota(jnp.int32, (1, LANES), 1) < HEAD_DIM
    for p in range(N_PAIRS):
        cols = slice(p * LANES, (p + 1) * LANES)
        for c in range(ROW_TILE // SGU_CHUNK):
            rows = slice(c * SGU_CHUNK, (c + 1) * SGU_CHUNK)
            gg = _pair_rows(gn_ref[rows, cols].astype(BF16), lane_lo)
            mixed = jnp.dot(ws_ref[0, p], gg, preferred_element_type=F32) + bias_ref[0, :, cols]
            sgu_ref[rows, cols] = u_ref[rows, cols] * mixed
    o_sb = jnp.where(_is_sample_step(), os_ref[...], op_ref[...])
    merged = jnp.concatenate([_rms(o_sb, gsb_ref[...]).astype(BF16),
                              _rms(sgu_ref[...], gsgu_ref[...]).astype(BF16)], axis=1)
    y = jnp.dot(merged, wout_ref[...], preferred_element_type=F32)
    y_ref[...] = x_ref[...] + _rms(y, post_ref[...])


def _mix(x, o_prompt, o_sample, u, gn, ws_pairs, bias, g_sb, g_sgu, w_out, post_g, n_prompt_tiles):
    n = x.shape[0]
    row = _merged_tile_spec(D_MODEL, n_prompt_tiles)
    half = _merged_tile_spec(SB_WIDTH, n_prompt_tiles)
    which = lambda s: jnp.where(s == 0, 1, 0)
    return pl.pallas_call(
        _mix_kernel,
        grid=(n // ROW_TILE,),
        in_specs=[row, _prompt_tile_spec(SB_WIDTH), _sample_tile_spec(SB_WIDTH), half, half,
                  pl.BlockSpec((1,) + ws_pairs.shape[1:], lambda s: (which(s), 0, 0, 0)),
                  pl.BlockSpec((1,) + bias.shape[1:], lambda s: (which(s), 0, 0)),
                  _const_spec((1, SB_WIDTH)), _const_spec((1, SGU_WIDTH)),
                  _const_spec((D_MODEL, D_MODEL)), _const_spec((1, D_MODEL))],
        out_specs=row,
        out_shape=jax.ShapeDtypeStruct((n, D_MODEL), F32),
        scratch_shapes=[pltpu.VMEM((ROW_TILE, SGU_WIDTH), F32)],
        compiler_params=_params(1),
        name="mix",
    )(x, o_prompt, o_sample, u, gn, ws_pairs, bias, g_sb, g_sgu, w_out, post_g)


def _sgu_operands(w_s, b_s, t_new):
    i = jnp.arange(SGU_CHUNK)
    mask = (i[None, :] // SGU_CAUSAL_CHUNK) <= (i[:, None] // SGU_CAUSAL_CHUNK)
    w_prompt = w_s * mask[None].astype(w_s.dtype)
    reps = SGU_CHUNK // t_new
    corner = w_prompt[:, :t_new, :t_new]
    eye = jnp.eye(reps, dtype=w_s.dtype)
    w_sample = jnp.einsum('ab,gij->gaibj', eye, corner).reshape(w_s.shape)
    groups = w_s.shape[0]

    def pairs(w):
        w = w.reshape(groups // 2, 2, SGU_CHUNK, SGU_CHUNK).transpose(0, 2, 1, 3)
        return w.reshape(groups // 2, SGU_CHUNK, 2 * SGU_CHUNK)

    def bias(b):
        return jnp.repeat(b.T, SGU_WIDTH // groups, axis=1)

    ws = jnp.stack([pairs(w_prompt), pairs(w_sample)]).astype(BF16)
    return ws, jnp.stack([bias(b_s), bias(jnp.tile(b_s[:, :t_new], (1, reps)))])


def kernel(x_prompt, x_sample, cache_k_sb, cache_v_sb, ffn1_pre_g, ffn1_post_g, ffn1_w_gate, ffn1_w_up, ffn1_w_down, mix_pre_g, mix_post_g, w_in, sgu_ln_g, sgu_ln_b, sgu_w_s, sgu_b_s, g_out_sb, g_out_sgu, w_out, ffn2_pre_g, ffn2_post_g, ffn2_w_gate, ffn2_w_up, ffn2_w_down):
    depth, batch, seq = w_in.shape[0], x_prompt.shape[0], x_prompt.shape[1]
    dec_batch, t_new, past_len = x_sample.shape[0], x_sample.shape[1], cache_k_sb.shape[2]
    assert depth == 1 and dec_batch * t_new == ROW_TILE and seq % ROW_TILE == 0
    assert SGU_CHUNK % t_new == 0 and KEY_BLOCK % t_new == 0
    assert past_len % (JOINT_PAST_BLOCKS * KEY_BLOCK) == 0
    n_prompt_tiles = batch * seq // ROW_TILE

    mat = lambda w: w[0].astype(BF16)
    vec = lambda g: g[0][None, :]
    w_qug = jnp.concatenate([w_in[0][:, :SB_WIDTH], w_in[0][:, 3 * SB_WIDTH:]], axis=1).astype(BF16)
    w_kvt = w_in[0][:, SB_WIDTH:3 * SB_WIDTH].T.astype(BF16)

    x = _ffn([x_prompt.reshape(batch * seq, D_MODEL), x_sample.reshape(ROW_TILE, D_MODEL)],
             vec(ffn1_pre_g), vec(ffn1_post_g), mat(ffn1_w_gate), mat(ffn1_w_up), mat(ffn1_w_down),
             n_prompt_tiles, split_out=False)
    q, u, gn, kt3, vt3, kt_p, vt_p, kt_s, vt_s = _inproj(
        x, vec(mix_pre_g), w_qug, w_kvt, vec(sgu_ln_g), vec(sgu_ln_b), n_prompt_tiles, seq // ROW_TILE)

    o_prompt = _sb_prompt(q, kt3, vt3, batch, seq)
    cache_t = lambda c: c[0].transpose(0, 2, 3, 1).reshape(dec_batch, SB_WIDTH, past_len)
    o_sample = _sb_sample(q, kt3, vt3, cache_t(cache_k_sb), cache_t(cache_v_sb), t_new, batch * seq)

    ws_pairs, bias = _sgu_operands(sgu_w_s[0], sgu_b_s[0], t_new)
    x = _mix(x, o_prompt, o_sample, u, gn, ws_pairs, bias, vec(g_out_sb), vec(g_out_sgu),
             mat(w_out), vec(mix_post_g), n_prompt_tiles)
    y_p, y_s = _ffn([x], vec(ffn2_pre_g), vec(ffn2_post_g), mat(ffn2_w_gate), mat(ffn2_w_up),
                    mat(ffn2_w_down), n_prompt_tiles, split_out=True)

    def prompt_heads(t):
        return t.reshape(batch, N_HEADS, HEAD_DIM, seq).transpose(0, 3, 1, 2)[None]

    def sample_heads(t):
        return t.reshape(N_HEADS, HEAD_DIM, dec_batch, t_new).transpose(2, 3, 0, 1)[None]

    g_s = gn[batch * seq:].reshape(1, dec_batch, t_new, N_HEADS, HEAD_DIM)
    return (y_p.reshape(batch, seq, D_MODEL), y_s.reshape(dec_batch, t_new, D_MODEL),
            prompt_heads(kt_p), prompt_heads(vt_p), sample_heads(kt_s), sample_heads(vt_s), g_s)
```

```python
import functools

import jax
import jax.numpy as jnp
from jax import lax
from jax.experimental import pallas as pl
from jax.experimental.pallas import tpu as pltpu

D_MODEL = 1024
D_FF = 2816
SB_WIDTH = 512
SGU_WIDTH = 512
HEAD_DIM = 64
N_HEADS = SB_WIDTH // HEAD_DIM
SGU_CHUNK = 128
SGU_CAUSAL_CHUNK = 64
FFN_RES = 0.5
EPS = 1e-6

LANES = 128
N_PAIRS = SB_WIDTH // LANES
KEY_BLOCK = 128
JOINT_PAST_BLOCKS = 2
FF_CHUNKS = (1536, 1280)
ROW_TILE = 512
FFN_ROWS = 256
VMEM_LIMIT = 56 * 1024 * 1024

LOG2_E = 1.4426950408889634
USED_STICK_CUTOFF = 105.0 * LOG2_E

F32 = jnp.float32
BF16 = jnp.bfloat16


def _rms(x, g):
    return x * lax.rsqrt(jnp.mean(x * x, axis=-1, keepdims=True) + EPS) * g


def _gelu(x):
    return 0.5 * x * (1.0 + lax.erf(x * (0.5 ** 0.5)))


def _const_spec(shape):
    return pl.BlockSpec(shape, lambda *_: (0,) * len(shape), pipeline_mode=pl.Buffered(1))


def _params(n_axes):
    return pltpu.CompilerParams(dimension_semantics=("arbitrary",) * n_axes,
                                vmem_limit_bytes=VMEM_LIMIT)


def _is_sample_step():
    return pl.program_id(0) == 0


def _merged_tile_spec(width, n_prompt_tiles):
    return pl.BlockSpec((ROW_TILE, width), lambda s: ((s + n_prompt_tiles) % (n_prompt_tiles + 1), 0))


def _prompt_tile_spec(width, rows=ROW_TILE):
    return pl.BlockSpec((rows, width), lambda s: (jnp.maximum(s - 1, 0), 0))


def _sample_tile_spec(width, rows=ROW_TILE):
    return pl.BlockSpec((rows, width), lambda s: (0, 0))


def _ffn_stages(x, pre_ref, post_ref, wg_ref, wu_ref, wd_ref, act_ref, rows):
    state = {}

    def hidden():
        state['h'] = _rms(x, pre_ref[...]).astype(BF16)

    def chunk(cols):
        gate = jnp.dot(state['h'], wg_ref[:, cols], preferred_element_type=F32)
        up = jnp.dot(state['h'], wu_ref[:, cols], preferred_element_type=F32)
        act_ref[rows, cols] = (gate * jax.nn.sigmoid(gate) * up).astype(BF16)

    def finish():
        f = jnp.dot(act_ref[rows, :], wd_ref[...], preferred_element_type=F32)
        return x + FFN_RES * _rms(f, post_ref[...])

    stages, start = [hidden], 0
    for width in FF_CHUNKS:
        stages.append(functools.partial(chunk, slice(start, start + width)))
        start += width
    return stages + [finish]


def _ffn_rows(x, pre_ref, post_ref, wg_ref, wu_ref, wd_ref, act_ref, rows):
    for stage in _ffn_stages(x, pre_ref, post_ref, wg_ref, wu_ref, wd_ref, act_ref, rows):
        y = stage()
    return y


_FFN_WEIGHT_SPECS = [_const_spec((1, D_MODEL)), _const_spec((1, D_MODEL)),
                     _const_spec((D_MODEL, D_FF)), _const_spec((D_MODEL, D_FF)),
                     _const_spec((D_FF, D_MODEL))]


def _ffn_kernel(xp_ref, xs_ref, pre_ref, post_ref, wg_ref, wu_ref, wd_ref, o_ref, act_ref):
    for r in range(ROW_TILE // FFN_ROWS):
        rows = slice(r * FFN_ROWS, (r + 1) * FFN_ROWS)
        x = jnp.where(_is_sample_step(), xs_ref[rows, :], xp_ref[rows, :])
        o_ref[rows, :] = _ffn_rows(x, pre_ref, post_ref, wg_ref, wu_ref, wd_ref, act_ref, rows)


def _ffn(x_prompt, x_sample, pre_g, post_g, wg, wu, wd, n_prompt_tiles):
    n_tiles = n_prompt_tiles + 1
    return pl.pallas_call(
        _ffn_kernel,
        grid=(n_tiles,),
        in_specs=[_prompt_tile_spec(D_MODEL), _sample_tile_spec(D_MODEL)] + _FFN_WEIGHT_SPECS,
        out_specs=_merged_tile_spec(D_MODEL, n_prompt_tiles),
        out_shape=jax.ShapeDtypeStruct((n_tiles * ROW_TILE, D_MODEL), F32),
        scratch_shapes=[pltpu.VMEM((ROW_TILE, D_FF), BF16)],
        compiler_params=_params(1),
        name="ffn",
    )(x_prompt, x_sample, pre_g, post_g, wg, wu, wd)


def _inproj_kernel(x_ref, g_ref, wqug_ref, wkvt_ref, lng_ref, lnb_ref,
                   q_ref, u_ref, gn_ref, kt3_ref, vt3_ref, ktp_ref, vtp_ref, kts_ref, vts_ref):
    h = _rms(x_ref[...], g_ref[...]).astype(BF16)
    z = jnp.dot(h, wqug_ref[...], preferred_element_type=F32)
    kvt = lax.dot_general(wkvt_ref[...], h, (((1,), (1,)), ((), ())), preferred_element_type=F32)
    w = SB_WIDTH
    q_ref[...] = (z[:, 0:w] * (HEAD_DIM ** -0.5 * LOG2_E)).astype(BF16)
    u_ref[...] = _gelu(z[:, w:w + SGU_WIDTH])
    ge = _gelu(z[:, w + SGU_WIDTH:])
    xc = ge - jnp.mean(ge, axis=-1, keepdims=True)
    y = xc * lax.rsqrt(jnp.mean(xc * xc, axis=-1, keepdims=True) + EPS)
    gn_ref[...] = y * lng_ref[...] + lnb_ref[...]
    for t_ref, t3_ref, rows in ((ktp_ref, kt3_ref, slice(0, w)), (vtp_ref, vt3_ref, slice(w, 2 * w))):
        t = kvt[rows]
        t_ref[...] = t
        for c in range(ROW_TILE // KEY_BLOCK):
            t3_ref[c] = t[:, c * KEY_BLOCK:(c + 1) * KEY_BLOCK].astype(BF16)

    @pl.when(_is_sample_step())
    def _():
        kts_ref[...] = ktp_ref[...]
        vts_ref[...] = vtp_ref[...]


def _inproj(x, g, w_qug, w_kvt, ln_g, ln_b, n_prompt_tiles, prompt_tiles_per_row):
    n = x.shape[0]
    half = _merged_tile_spec(SB_WIDTH, n_prompt_tiles)
    blocks_per_tile = ROW_TILE // KEY_BLOCK
    key_blocks = pl.BlockSpec((blocks_per_tile, SB_WIDTH, KEY_BLOCK),
                              lambda s: ((s + n_prompt_tiles) % (n_prompt_tiles + 1), 0, 0))

    def prompt_t_index(s):
        tile = jnp.maximum(s - 1, 0)
        return tile // prompt_tiles_per_row, tile % prompt_tiles_per_row

    prompt_t = pl.BlockSpec((SB_WIDTH, ROW_TILE), prompt_t_index)
    prompt_t_sds = jax.ShapeDtypeStruct(
        (n_prompt_tiles // prompt_tiles_per_row * SB_WIDTH, prompt_tiles_per_row * ROW_TILE), F32)
    sample_t_sds = jax.ShapeDtypeStruct((SB_WIDTH, ROW_TILE), F32)
    f32_half = jax.ShapeDtypeStruct((n, SB_WIDTH), F32)
    key_blocks_sds = jax.ShapeDtypeStruct((n // KEY_BLOCK, SB_WIDTH, KEY_BLOCK), BF16)
    return pl.pallas_call(
        _inproj_kernel,
        grid=(n // ROW_TILE,),
        in_specs=[_merged_tile_spec(D_MODEL, n_prompt_tiles), _const_spec((1, D_MODEL)),
                  _const_spec(w_qug.shape), _const_spec(w_kvt.shape), _const_spec((1, SGU_WIDTH)),
                  _const_spec((1, SGU_WIDTH))],
        out_specs=[half] * 3 + [key_blocks] * 2 + [prompt_t] * 2 + [_sample_tile_spec(ROW_TILE, SB_WIDTH)] * 2,
        out_shape=[jax.ShapeDtypeStruct((n, SB_WIDTH), BF16), f32_half, f32_half,
                   key_blocks_sds, key_blocks_sds, prompt_t_sds, prompt_t_sds,
                   sample_t_sds, sample_t_sds],
        compiler_params=_params(1),
        name="inproj",
    )(x, g, w_qug, w_kvt, ln_g, ln_b)


def _pair_rows(x, lane_lo):
    zero = jnp.zeros_like(x)
    return jnp.concatenate([jnp.where(lane_lo, x, zero), jnp.where(lane_lo, zero, x)], axis=0)


def _pair_cols(xt, row_lo):
    zero = jnp.zeros_like(xt)
    return jnp.concatenate([jnp.where(row_lo, xt, zero), jnp.where(row_lo, zero, xt)], axis=1)


def _sb_blocks(qqs, kts, vts, carries, suffix_ones, row_lo, masks):
    zs = [[jnp.dot(qq, kt, preferred_element_type=F32) for kt in kts_p]
          for qq, kts_p in zip(qqs, kts)]
    useds, splits = [], []
    for zs_p in zs:
        useds.append([])
        splits.append([])
        for z, mask in zip(zs_p, masks):
            used = jnp.maximum(z, 0.0) + jnp.log(1.0 + jnp.exp2(-jnp.abs(z))) * LOG2_E
            if mask is not None:
                used = jnp.where(mask, used, 0.0)
            hi = used.astype(BF16)
            lo = (used - hi.astype(F32)).astype(BF16)
            useds[-1].append(used)
            splits[-1].append(jnp.concatenate([hi, lo], axis=1))
    sums = [[jnp.dot(s, suffix_ones, preferred_element_type=F32) for s in splits_p]
            for splits_p in splits]
    outs, new_carries = [], []
    for zs_p, useds_p, sums_p, vts_p, carry in zip(zs, useds, sums, vts, carries):
        weights = []
        for z, used, s, mask in zip(zs_p, useds_p, sums_p, masks):
            a = jnp.exp2(z - used - (s[:, :KEY_BLOCK] + carry))
            if mask is not None:
                a = jnp.where(mask, a, 0.0)
            a = a.astype(BF16)
            t = a.shape[0] // 2
            weights += [a[:t], a[t:]]
            carry = carry + s[:, KEY_BLOCK:]
        values_t = jnp.concatenate([_pair_cols(vt, row_lo) for vt in vts_p], axis=1)
        outs.append(lax.dot_general(jnp.concatenate(weights, axis=1), values_t,
                                    (((1,), (1,)), ((), ())), preferred_element_type=F32))
        new_carries.append(carry)
    return outs, new_carries


def _suffix_ones():
    r = lax.broadcasted_iota(jnp.int32, (2 * KEY_BLOCK, 2 * KEY_BLOCK), 0) % KEY_BLOCK
    c = lax.broadcasted_iota(jnp.int32, (2 * KEY_BLOCK, 2 * KEY_BLOCK), 1)
    return jnp.where((c >= KEY_BLOCK) | (r > c), 1.0, 0.0).astype(BF16)


_PAIR_SLICES = [slice(p * LANES, (p + 1) * LANES) for p in range(N_PAIRS)]


def _sb_core(q_ref, o_ref, carry_ref, n_past, diag_key_offset, diag_tiles, joint_tiles, walk_tiles):
    tq = q_ref.shape[0]
    lane_lo = lax.broadcasted_iota(jnp.int32, (1, LANES), 1) < HEAD_DIM
    row_lo = lax.broadcasted_iota(jnp.int32, (LANES, 1), 0) < HEAD_DIM
    suffix_ones = _suffix_ones()
    row = lax.broadcasted_iota(jnp.int32, (2 * tq, KEY_BLOCK), 0) % tq
    col = lax.broadcasted_iota(jnp.int32, (2 * tq, KEY_BLOCK), 1) - diag_key_offset
    causal = (col >= 0) & (col < row)

    def qq_tiles():
        return [_pair_rows(q_ref[:, cols], lane_lo) for cols in _PAIR_SLICES]

    def finish(outs, carries, first):
        for p, cols in enumerate(_PAIR_SLICES):
            if first:
                o_ref[:, cols] = outs[p]
            else:
                o_ref[:, cols] += outs[p]
            carry_ref[p] = carries[p]

    def first_step(n_joint):
        kts, vts = diag_tiles()
        kts, vts = [[t] for t in kts], [[t] for t in vts]
        for b in range(n_joint):
            kj, vj = joint_tiles(n_past - 1 - b)
            for p in range(N_PAIRS):
                kts[p].append(kj[p])
                vts[p].append(vj[p])
        outs, carries = _sb_blocks(qq_tiles(), kts, vts,
                                   [jnp.zeros((2 * tq, LANES), F32)] * N_PAIRS,
                                   suffix_ones, row_lo, [causal] + [None] * n_joint)
        finish(outs, carries, True)

    if isinstance(n_past, int):
        n_joint = min(JOINT_PAST_BLOCKS, n_past)
        first_step(n_joint)
        next_block = n_past - n_joint - 1
    else:
        pl.when(n_past >= JOINT_PAST_BLOCKS)(lambda: first_step(JOINT_PAST_BLOCKS))
        pl.when(n_past < JOINT_PAST_BLOCKS)(lambda: first_step(0))
        next_block = jnp.where(n_past >= JOINT_PAST_BLOCKS, n_past - JOINT_PAST_BLOCKS, n_past) - 1

    def least_used():
        return jnp.min(functools.reduce(jnp.minimum, [carry_ref[p] for p in range(N_PAIRS)]))

    def body(state):
        j, _ = state
        kts, vts = walk_tiles(j)
        outs, carries = _sb_blocks(qq_tiles(), [[t] for t in kts], [[t] for t in vts],
                                   [carry_ref[p] for p in range(N_PAIRS)],
                                   suffix_ones, row_lo, [None])
        finish(outs, carries, False)
        return j - 1, least_used()

    lax.while_loop(lambda s: (s[0] >= 0) & (s[1] < USED_STICK_CUTOFF), body,
                   (next_block, least_used()))


def _sb_prompt_kernel(q_ref, kt_ref, vt_ref, o_ref, carry_ref):
    def tiles(j):
        return ([kt_ref[j, rows, :] for rows in _PAIR_SLICES],
                [vt_ref[j, rows, :] for rows in _PAIR_SLICES])

    i = pl.program_id(1)
    _sb_core(q_ref, o_ref, carry_ref, i, 0, lambda: tiles(i), tiles, tiles)


def _sb_prompt(q, kt3, vt3, batch, seq):
    nq = seq // KEY_BLOCK
    blk = pl.BlockSpec((KEY_BLOCK, SB_WIDTH), lambda b, i: (b * nq + i, 0))
    keys = pl.BlockSpec((nq, SB_WIDTH, KEY_BLOCK), lambda b, i: (b, 0, 0))
    return pl.pallas_call(
        _sb_prompt_kernel,
        grid=(batch, nq),
        in_specs=[blk, keys, keys],
        out_specs=blk,
        out_shape=jax.ShapeDtypeStruct((batch * seq, SB_WIDTH), F32),
        scratch_shapes=[pltpu.VMEM((N_PAIRS, 2 * KEY_BLOCK, LANES), F32)],
        compiler_params=_params(2),
        name="sb_prompt",
    )(q, kt3, vt3)


def _sb_sample_kernel(q_ref, kd_ref, vd_ref, kr_ref, vr_ref, kc_hbm, vc_hbm, o_ref,
                      carry_ref, kbuf_ref, vbuf_ref, sem_ref, *, past_len):
    tq = q_ref.shape[0]
    n_past = past_len // KEY_BLOCK
    n_recent = kr_ref.shape[2] // KEY_BLOCK
    b = pl.program_id(0)

    def diag_tiles():
        return ([kd_ref[0, rows, :] for rows in _PAIR_SLICES],
                [vd_ref[0, rows, :] for rows in _PAIR_SLICES])

    def recent_tiles(j):
        first = (j - (n_past - n_recent)) * KEY_BLOCK
        keys = slice(first, first + KEY_BLOCK)
        return ([kr_ref[0, rows, keys].astype(BF16) for rows in _PAIR_SLICES],
                [vr_ref[0, rows, keys].astype(BF16) for rows in _PAIR_SLICES])

    def cache_tiles(j):
        keys = pl.ds(pl.multiple_of(j * KEY_BLOCK, KEY_BLOCK), KEY_BLOCK)
        copies = [pltpu.make_async_copy(hbm.at[b, :, keys], buf, sem_ref.at[s])
                  for s, (hbm, buf) in enumerate(((kc_hbm, kbuf_ref), (vc_hbm, vbuf_ref)))]
        for c in copies:
            c.start()
        for c in copies:
            c.wait()
        return ([kbuf_ref[rows, :].astype(BF16) for rows in _PAIR_SLICES],
                [vbuf_ref[rows, :].astype(BF16) for rows in _PAIR_SLICES])

    assert n_recent == JOINT_PAST_BLOCKS
    _sb_core(q_ref, o_ref, carry_ref, n_past, (b % (KEY_BLOCK // tq)) * tq,
             diag_tiles, recent_tiles, cache_tiles)


def _sb_sample(q, kt3, vt3, cache_kt, cache_vt, t_new, first_row):
    batch, _, past_len = cache_kt.shape
    n_recent = JOINT_PAST_BLOCKS * KEY_BLOCK
    seqs_per_block = KEY_BLOCK // t_new
    new = pl.BlockSpec((1, SB_WIDTH, KEY_BLOCK),
                       lambda b: (first_row // KEY_BLOCK + b // seqs_per_block, 0, 0))
    recent = pl.BlockSpec((1, SB_WIDTH, n_recent), lambda b: (b, 0, past_len // n_recent - 1))
    hbm = pl.BlockSpec(memory_space=pl.ANY)
    return pl.pallas_call(
        functools.partial(_sb_sample_kernel, past_len=past_len),
        grid=(batch,),
        in_specs=[pl.BlockSpec((t_new, SB_WIDTH), lambda b: (first_row // t_new + b, 0)),
                  new, new, recent, recent, hbm, hbm],
        out_specs=pl.BlockSpec((t_new, SB_WIDTH), lambda b: (b, 0)),
        out_shape=jax.ShapeDtypeStruct((batch * t_new, SB_WIDTH), F32),
        scratch_shapes=[pltpu.VMEM((N_PAIRS, 2 * t_new, LANES), F32),
                        pltpu.VMEM((SB_WIDTH, KEY_BLOCK), F32),
                        pltpu.VMEM((SB_WIDTH, KEY_BLOCK), F32),
                        pltpu.SemaphoreType.DMA((2,))],
        compiler_params=_params(1),
        name="sb_sample",
    )(q, kt3, vt3, cache_kt, cache_vt, cache_kt, cache_vt)


def _mix_rows(x_ref, op_ref, os_ref, u_ref, gn_ref, ws_ref, bias_ref, gsb_ref, gsgu_ref, wout_ref,
              post_ref, sgu_ref, rows):
    lane_lo = lax.broadcasted_iota(jnp.int32, (1, LANES), 1) < HEAD_DIM
    for p in range(N_PAIRS):
        cols = slice(p * LANES, (p + 1) * LANES)
        for first in range(rows.start, rows.stop, SGU_CHUNK):
            chunk = slice(first, first + SGU_CHUNK)
            gg = _pair_rows(gn_ref[chunk, cols].astype(BF16), lane_lo)
            mixed = jnp.dot(ws_ref[0, p], gg, preferred_element_type=F32) + bias_ref[0, :, cols]
            sgu_ref[chunk, cols] = u_ref[chunk, cols] * mixed
    o_sb = jnp.where(_is_sample_step(), os_ref[rows, :], op_ref[rows, :])
    merged = jnp.concatenate([_rms(o_sb, gsb_ref[...]).astype(BF16),
                              _rms(sgu_ref[rows, :], gsgu_ref[...]).astype(BF16)], axis=1)
    y = jnp.dot(merged, wout_ref[...], preferred_element_type=F32)
    return x_ref[rows, :] + _rms(y, post_ref[...])


def _mix_ffn_kernel(x_ref, op_ref, os_ref, u_ref, gn_ref, ws_ref, bias_ref, gsb_ref, gsgu_ref,
                    wout_ref, mix_post_ref, pre_ref, post_ref, wg_ref, wu_ref, wd_ref,
                    yp_ref, ys_ref, sgu_ref, act_ref):
    def mix(rows):
        return _mix_rows(x_ref, op_ref, os_ref, u_ref, gn_ref, ws_ref, bias_ref, gsb_ref, gsgu_ref,
                         wout_ref, mix_post_ref, sgu_ref, rows)

    groups = [slice(r * FFN_ROWS, (r + 1) * FFN_ROWS) for r in range(ROW_TILE // FFN_ROWS)]
    x = mix(groups[0])
    for g, rows in enumerate(groups):
        stages = _ffn_stages(x, pre_ref, post_ref, wg_ref, wu_ref, wd_ref, act_ref, rows)
        stages[0]()
        stages[1]()
        if g + 1 < len(groups):
            x = mix(groups[g + 1])
        for stage in stages[2:]:
            y = stage()
        yp_ref[rows, :] = y

    @pl.when(_is_sample_step())
    def _():
        ys_ref[...] = yp_ref[...]


def _mix_ffn(x, o_prompt, o_sample, u, gn, ws_pairs, bias, g_sb, g_sgu, w_out, mix_post_g,
             pre_g, post_g, wg, wu, wd, n_prompt_tiles):
    row = _merged_tile_spec(D_MODEL, n_prompt_tiles)
    half = _merged_tile_spec(SB_WIDTH, n_prompt_tiles)
    which = lambda s: jnp.where(s == 0, 1, 0)
    sds = lambda rows: jax.ShapeDtypeStruct((rows, D_MODEL), F32)
    return pl.pallas_call(
        _mix_ffn_kernel,
        grid=(n_prompt_tiles + 1,),
        in_specs=[row, _prompt_tile_spec(SB_WIDTH), _sample_tile_spec(SB_WIDTH), half, half,
                  pl.BlockSpec((1,) + ws_pairs.shape[1:], lambda s: (which(s), 0, 0, 0)),
                  pl.BlockSpec((1,) + bias.shape[1:], lambda s: (which(s), 0, 0)),
                  _const_spec((1, SB_WIDTH)), _const_spec((1, SGU_WIDTH)),
                  _const_spec((D_MODEL, D_MODEL)), _const_spec((1, D_MODEL))] + _FFN_WEIGHT_SPECS,
        out_specs=[_prompt_tile_spec(D_MODEL), _sample_tile_spec(D_MODEL)],
        out_shape=[sds(n_prompt_tiles * ROW_TILE), sds(ROW_TILE)],
        scratch_shapes=[pltpu.VMEM((ROW_TILE, SGU_WIDTH), F32), pltpu.VMEM((ROW_TILE, D_FF), BF16)],
        compiler_params=_params(1),
        name="mix_ffn",
    )(x, o_prompt, o_sample, u, gn, ws_pairs, bias, g_sb, g_sgu, w_out, mix_post_g,
      pre_g, post_g, wg, wu, wd)


def _sgu_operands(w_s, b_s, t_new):
    i = jnp.arange(SGU_CHUNK)
    mask = (i[None, :] // SGU_CAUSAL_CHUNK) <= (i[:, None] // SGU_CAUSAL_CHUNK)
    w_prompt = w_s * mask[None].astype(w_s.dtype)
    reps = SGU_CHUNK // t_new
    corner = w_prompt[:, :t_new, :t_new]
    eye = jnp.eye(reps, dtype=w_s.dtype)
    w_sample = jnp.einsum('ab,gij->gaibj', eye, corner).reshape(w_s.shape)
    groups = w_s.shape[0]

    def pairs(w):
        w = w.reshape(groups // 2, 2, SGU_CHUNK, SGU_CHUNK).transpose(0, 2, 1, 3)
        return w.reshape(groups // 2, SGU_CHUNK, 2 * SGU_CHUNK)

    def bias(b):
        return jnp.repeat(b.T, SGU_WIDTH // groups, axis=1)

    ws = jnp.stack([pairs(w_prompt), pairs(w_sample)]).astype(BF16)
    return ws, jnp.stack([bias(b_s), bias(jnp.tile(b_s[:, :t_new], (1, reps)))])


def kernel(x_prompt, x_sample, cache_k_sb, cache_v_sb, ffn1_pre_g, ffn1_post_g, ffn1_w_gate, ffn1_w_up, ffn1_w_down, mix_pre_g, mix_post_g, w_in, sgu_ln_g, sgu_ln_b, sgu_w_s, sgu_b_s, g_out_sb, g_out_sgu, w_out, ffn2_pre_g, ffn2_post_g, ffn2_w_gate, ffn2_w_up, ffn2_w_down):
    depth, batch, seq = w_in.shape[0], x_prompt.shape[0], x_prompt.shape[1]
    dec_batch, t_new, past_len = x_sample.shape[0], x_sample.shape[1], cache_k_sb.shape[2]
    assert depth == 1 and dec_batch * t_new == ROW_TILE and seq % ROW_TILE == 0
    assert SGU_CHUNK % t_new == 0 and KEY_BLOCK % t_new == 0
    assert past_len % (JOINT_PAST_BLOCKS * KEY_BLOCK) == 0
    n_prompt_tiles = batch * seq // ROW_TILE

    mat = lambda w: w[0].astype(BF16)
    vec = lambda g: g[0][None, :]
    w_qug = jnp.concatenate([w_in[0][:, :SB_WIDTH], w_in[0][:, 3 * SB_WIDTH:]], axis=1).astype(BF16)
    w_kvt = w_in[0][:, SB_WIDTH:3 * SB_WIDTH].T.astype(BF16)

    x = _ffn(x_prompt.reshape(batch * seq, D_MODEL), x_sample.reshape(ROW_TILE, D_MODEL),
             vec(ffn1_pre_g), vec(ffn1_post_g), mat(ffn1_w_gate), mat(ffn1_w_up), mat(ffn1_w_down),
             n_prompt_tiles)
    q, u, gn, kt3, vt3, kt_p, vt_p, kt_s, vt_s = _inproj(
        x, vec(mix_pre_g), w_qug, w_kvt, vec(sgu_ln_g), vec(sgu_ln_b), n_prompt_tiles, seq // ROW_TILE)

    o_prompt = _sb_prompt(q, kt3, vt3, batch, seq)
    cache_t = lambda c: c[0].transpose(0, 2, 3, 1).reshape(dec_batch, SB_WIDTH, past_len)
    o_sample = _sb_sample(q, kt3, vt3, cache_t(cache_k_sb), cache_t(cache_v_sb), t_new, batch * seq)

    ws_pairs, bias = _sgu_operands(sgu_w_s[0], sgu_b_s[0], t_new)
    y_p, y_s = _mix_ffn(x, o_prompt, o_sample, u, gn, ws_pairs, bias, vec(g_out_sb), vec(g_out_sgu),
                        mat(w_out), vec(mix_post_g), vec(ffn2_pre_g), vec(ffn2_post_g),
                        mat(ffn2_w_gate), mat(ffn2_w_up), mat(ffn2_w_down), n_prompt_tiles)

    def prompt_heads(t):
        return t.reshape(batch, N_HEADS, HEAD_DIM, seq).transpose(0, 3, 1, 2)[None]

    def sample_heads(t):
        return t.reshape(N_HEADS, HEAD_DIM, dec_batch, t_new).transpose(2, 3, 0, 1)[None]

    g_s = gn[batch * seq:].reshape(1, dec_batch, t_new, N_HEADS, HEAD_DIM)
    return (y_p.reshape(batch, seq, D_MODEL), y_s.reshape(dec_batch, t_new, D_MODEL),
            prompt_heads(kt_p), prompt_heads(vt_p), sample_heads(kt_s), sample_heads(vt_s), g_s)
```

```python
import functools

import jax
import jax.numpy as jnp
from jax import lax
from jax.experimental import pallas as pl
from jax.experimental.pallas import tpu as pltpu

D_MODEL = 1024
D_FF = 2816
SB_WIDTH = 512
SGU_WIDTH = 512
HEAD_DIM = 64
N_HEADS = SB_WIDTH // HEAD_DIM
SGU_CHUNK = 128
SGU_CAUSAL_CHUNK = 64
FFN_RES = 0.5
EPS = 1e-6

LANES = 128
N_PAIRS = SB_WIDTH // LANES
KEY_BLOCK = 128
JOINT_PAST_BLOCKS = 2
TOP_ROWS = 32
Q_BLOCKS_PER_STEP = 2
FF_CHUNKS = (1536, 1280)
ROW_TILE = 512
FFN_ROWS = 256
VMEM_LIMIT = 56 * 1024 * 1024

LOG2_E = 1.4426950408889634
USED_STICK_CUTOFF = 105.0 * LOG2_E

F32 = jnp.float32
BF16 = jnp.bfloat16


def _rms(x, g):
    return x * lax.rsqrt(jnp.mean(x * x, axis=-1, keepdims=True) + EPS) * g


def _gelu(x):
    return 0.5 * x * (1.0 + lax.erf(x * (0.5 ** 0.5)))


def _const_spec(shape):
    return pl.BlockSpec(shape, lambda *_: (0,) * len(shape), pipeline_mode=pl.Buffered(1))


def _params(n_axes):
    return pltpu.CompilerParams(dimension_semantics=("arbitrary",) * n_axes,
                                vmem_limit_bytes=VMEM_LIMIT)


def _is_sample_step():
    return pl.program_id(0) == 0


def _merged_tile_spec(width, n_prompt_tiles):
    return pl.BlockSpec((ROW_TILE, width), lambda s: ((s + n_prompt_tiles) % (n_prompt_tiles + 1), 0))


def _prompt_tile_spec(width, rows=ROW_TILE):
    return pl.BlockSpec((rows, width), lambda s: (jnp.maximum(s - 1, 0), 0))


def _sample_tile_spec(width, rows=ROW_TILE):
    return pl.BlockSpec((rows, width), lambda s: (0, 0))


def _ffn_stages(x, pre_ref, post_ref, wg_ref, wu_ref, wd_ref, act_ref, rows):
    state = {}

    def hidden():
        state['h'] = _rms(x, pre_ref[...]).astype(BF16)

    def chunk(cols):
        gate = jnp.dot(state['h'], wg_ref[:, cols], preferred_element_type=F32)
        up = jnp.dot(state['h'], wu_ref[:, cols], preferred_element_type=F32)
        act_ref[rows, cols] = (gate * jax.nn.sigmoid(gate) * up).astype(BF16)

    def finish():
        f = jnp.dot(act_ref[rows, :], wd_ref[...], preferred_element_type=F32)
        return x + FFN_RES * _rms(f, post_ref[...])

    stages, start = [hidden], 0
    for width in FF_CHUNKS:
        stages.append(functools.partial(chunk, slice(start, start + width)))
        start += width
    return stages + [finish]


def _ffn_rows(x, pre_ref, post_ref, wg_ref, wu_ref, wd_ref, act_ref, rows):
    for stage in _ffn_stages(x, pre_ref, post_ref, wg_ref, wu_ref, wd_ref, act_ref, rows):
        y = stage()
    return y


_FFN_WEIGHT_SPECS = [_const_spec((1, D_MODEL)), _const_spec((1, D_MODEL)),
                     _const_spec((D_MODEL, D_FF)), _const_spec((D_MODEL, D_FF)),
                     _const_spec((D_FF, D_MODEL))]


def _ffn_kernel(xp_ref, xs_ref, pre_ref, post_ref, wg_ref, wu_ref, wd_ref, o_ref, act_ref):
    for r in range(ROW_TILE // FFN_ROWS):
        rows = slice(r * FFN_ROWS, (r + 1) * FFN_ROWS)
        x = jnp.where(_is_sample_step(), xs_ref[rows, :], xp_ref[rows, :])
        o_ref[rows, :] = _ffn_rows(x, pre_ref, post_ref, wg_ref, wu_ref, wd_ref, act_ref, rows)


def _ffn(x_prompt, x_sample, pre_g, post_g, wg, wu, wd, n_prompt_tiles):
    n_tiles = n_prompt_tiles + 1
    return pl.pallas_call(
        _ffn_kernel,
        grid=(n_tiles,),
        in_specs=[_prompt_tile_spec(D_MODEL), _sample_tile_spec(D_MODEL)] + _FFN_WEIGHT_SPECS,
        out_specs=_merged_tile_spec(D_MODEL, n_prompt_tiles),
        out_shape=jax.ShapeDtypeStruct((n_tiles * ROW_TILE, D_MODEL), F32),
        scratch_shapes=[pltpu.VMEM((ROW_TILE, D_FF), BF16)],
        compiler_params=_params(1),
        name="ffn",
    )(x_prompt, x_sample, pre_g, post_g, wg, wu, wd)


def _inproj_kernel(x_ref, g_ref, wqug_ref, wkvt_ref, lng_ref, lnb_ref,
                   q_ref, u_ref, gn_ref, kt3_ref, vt3_ref, ktp_ref, vtp_ref, kts_ref, vts_ref):
    h = _rms(x_ref[...], g_ref[...]).astype(BF16)
    z = jnp.dot(h, wqug_ref[...], preferred_element_type=F32)
    kvt = lax.dot_general(wkvt_ref[...], h, (((1,), (1,)), ((), ())), preferred_element_type=F32)
    w = SB_WIDTH
    q_ref[...] = (z[:, 0:w] * (HEAD_DIM ** -0.5 * LOG2_E)).astype(BF16)
    u_ref[...] = _gelu(z[:, w:w + SGU_WIDTH])
    ge = _gelu(z[:, w + SGU_WIDTH:])
    xc = ge - jnp.mean(ge, axis=-1, keepdims=True)
    y = xc * lax.rsqrt(jnp.mean(xc * xc, axis=-1, keepdims=True) + EPS)
    gn_ref[...] = y * lng_ref[...] + lnb_ref[...]
    for t_ref, t3_ref, rows in ((ktp_ref, kt3_ref, slice(0, w)), (vtp_ref, vt3_ref, slice(w, 2 * w))):
        t = kvt[rows]
        t_ref[...] = t
        for c in range(ROW_TILE // KEY_BLOCK):
            t3_ref[c] = t[:, c * KEY_BLOCK:(c + 1) * KEY_BLOCK].astype(BF16)

    @pl.when(_is_sample_step())
    def _():
        kts_ref[...] = ktp_ref[...]
        vts_ref[...] = vtp_ref[...]


def _inproj(x, g, w_qug, w_kvt, ln_g, ln_b, n_prompt_tiles, prompt_tiles_per_row):
    n = x.shape[0]
    half = _merged_tile_spec(SB_WIDTH, n_prompt_tiles)
    blocks_per_tile = ROW_TILE // KEY_BLOCK
    key_blocks = pl.BlockSpec((blocks_per_tile, SB_WIDTH, KEY_BLOCK),
                              lambda s: ((s + n_prompt_tiles) % (n_prompt_tiles + 1), 0, 0))

    def prompt_t_index(s):
        tile = jnp.maximum(s - 1, 0)
        return tile // prompt_tiles_per_row, tile % prompt_tiles_per_row

    prompt_t = pl.BlockSpec((SB_WIDTH, ROW_TILE), prompt_t_index)
    prompt_t_sds = jax.ShapeDtypeStruct(
        (n_prompt_tiles // prompt_tiles_per_row * SB_WIDTH, prompt_tiles_per_row * ROW_TILE), F32)
    sample_t_sds = jax.ShapeDtypeStruct((SB_WIDTH, ROW_TILE), F32)
    f32_half = jax.ShapeDtypeStruct((n, SB_WIDTH), F32)
    key_blocks_sds = jax.ShapeDtypeStruct((n // KEY_BLOCK, SB_WIDTH, KEY_BLOCK), BF16)
    return pl.pallas_call(
        _inproj_kernel,
        grid=(n // ROW_TILE,),
        in_specs=[_merged_tile_spec(D_MODEL, n_prompt_tiles), _const_spec((1, D_MODEL)),
                  _const_spec(w_qug.shape), _const_spec(w_kvt.shape), _const_spec((1, SGU_WIDTH)),
                  _const_spec((1, SGU_WIDTH))],
        out_specs=[half] * 3 + [key_blocks] * 2 + [prompt_t] * 2 + [_sample_tile_spec(ROW_TILE, SB_WIDTH)] * 2,
        out_shape=[jax.ShapeDtypeStruct((n, SB_WIDTH), BF16), f32_half, f32_half,
                   key_blocks_sds, key_blocks_sds, prompt_t_sds, prompt_t_sds,
                   sample_t_sds, sample_t_sds],
        compiler_params=_params(1),
        name="inproj",
    )(x, g, w_qug, w_kvt, ln_g, ln_b)


def _pair_rows(x, lane_lo):
    zero = jnp.zeros_like(x)
    return jnp.concatenate([jnp.where(lane_lo, x, zero), jnp.where(lane_lo, zero, x)], axis=0)


def _pair_cols(xt, row_lo):
    zero = jnp.zeros_like(xt)
    return jnp.concatenate([jnp.where(row_lo, xt, zero), jnp.where(row_lo, zero, xt)], axis=1)


def _sb_blocks(qqs, kts, vts, carries, suffix_ones, row_lo, masks, tops=None):
    n_blocks = len(masks)
    tops = tops or [None] * n_blocks
    t = qqs[0].shape[0] // 2

    def head_rows(x, top):
        return x if top is None else jnp.concatenate([x[:top], x[t:t + top]], axis=0)

    zs = []
    for qq, kts_p in zip(qqs, kts):
        zs_p, b = [], 0
        while b < n_blocks:
            if b + 1 < n_blocks and tops[b] is None and tops[b + 1] is None:
                z2 = jnp.dot(qq, jnp.concatenate([kts_p[b], kts_p[b + 1]], axis=1),
                             preferred_element_type=F32)
                zs_p += [z2[:, :KEY_BLOCK], z2[:, KEY_BLOCK:]]
                b += 2
            else:
                zs_p.append(jnp.dot(head_rows(qq, tops[b]), kts_p[b], preferred_element_type=F32))
                b += 1
        zs.append(zs_p)
    useds, splits = [], []
    for zs_p in zs:
        useds.append([])
        splits.append([])
        for z, mask in zip(zs_p, masks):
            used = jnp.maximum(z, 0.0) + jnp.log(1.0 + jnp.exp2(-jnp.abs(z))) * LOG2_E
            if mask is not None:
                used = jnp.where(mask, used, 0.0)
            hi = used.astype(BF16)
            lo = (used - hi.astype(F32)).astype(BF16)
            useds[-1].append(used)
            splits[-1].append(jnp.concatenate([hi, lo], axis=1))
    sums = [[jnp.dot(s, suffix_ones, preferred_element_type=F32) for s in splits_p]
            for splits_p in splits]
    nt_dims = (((1,), (1,)), ((), ()))
    outs, new_carries = [], []
    for zs_p, useds_p, sums_p, vts_p, carry in zip(zs, useds, sums, vts, carries):
        weights, values_t, top_outs = [], [], []
        for z, used, s, vt, mask, top in zip(zs_p, useds_p, sums_p, vts_p, masks, tops):
            a = jnp.exp2(z - used - (s[:, :KEY_BLOCK] + head_rows(carry, top)))
            if mask is not None:
                a = jnp.where(mask, a, 0.0)
            a = a.astype(BF16)
            rows = a.shape[0] // 2
            w = jnp.concatenate([a[:rows], a[rows:]], axis=1)
            row_sum = s[:, KEY_BLOCK:]
            if top is None:
                weights.append(w)
                values_t.append(_pair_cols(vt, row_lo))
                carry = carry + row_sum
            else:
                top_outs.append(lax.dot_general(w, _pair_cols(vt, row_lo), nt_dims,
                                                preferred_element_type=F32))
                carry = jnp.concatenate([carry[:top] + row_sum[:top], carry[top:t],
                                         carry[t:t + top] + row_sum[top:], carry[t + top:]], axis=0)
        out = lax.dot_general(jnp.concatenate(weights, axis=1), jnp.concatenate(values_t, axis=1),
                              nt_dims, preferred_element_type=F32)
        for o_top in top_outs:
            top = o_top.shape[0]
            out = jnp.concatenate([out[:top] + o_top, out[top:]], axis=0)
        outs.append(out)
        new_carries.append(carry)
    return outs, new_carries


def _suffix_ones():
    r = lax.broadcasted_iota(jnp.int32, (2 * KEY_BLOCK, 2 * KEY_BLOCK), 0) % KEY_BLOCK
    c = lax.broadcasted_iota(jnp.int32, (2 * KEY_BLOCK, 2 * KEY_BLOCK), 1)
    return jnp.where((c >= KEY_BLOCK) | (r > c), 1.0, 0.0).astype(BF16)


_PAIR_SLICES = [slice(p * LANES, (p + 1) * LANES) for p in range(N_PAIRS)]


def _sb_core(q_ref, o_ref, carry_ref, n_past, diag_key_offset, diag_tiles, joint_tiles, walk_tiles):
    tq = q_ref.shape[0]
    lane_lo = lax.broadcasted_iota(jnp.int32, (1, LANES), 1) < HEAD_DIM
    row_lo = lax.broadcasted_iota(jnp.int32, (LANES, 1), 0) < HEAD_DIM
    suffix_ones = _suffix_ones()
    row = lax.broadcasted_iota(jnp.int32, (2 * tq, KEY_BLOCK), 0) % tq
    col = lax.broadcasted_iota(jnp.int32, (2 * tq, KEY_BLOCK), 1) - diag_key_offset
    causal = (col >= 0) & (col < row)

    def qq_tiles():
        return [_pair_rows(q_ref[:, cols], lane_lo) for cols in _PAIR_SLICES]

    def finish(outs, carries, first):
        for p, cols in enumerate(_PAIR_SLICES):
            if first:
                o_ref[:, cols] = outs[p]
            else:
                o_ref[:, cols] += outs[p]
            carry_ref[p] = carries[p]

    def first_step(n_joint):
        kts, vts = diag_tiles()
        kts, vts = [[t] for t in kts], [[t] for t in vts]
        for b in range(n_joint):
            kj, vj = joint_tiles(n_past - 1 - b)
            for p in range(N_PAIRS):
                kts[p].append(kj[p])
                vts[p].append(vj[p])
        tops = [None] * n_joint + ([TOP_ROWS] if n_joint > 1 else [None])
        outs, carries = _sb_blocks(qq_tiles(), kts, vts,
                                   [jnp.zeros((2 * tq, LANES), F32)] * N_PAIRS,
                                   suffix_ones, row_lo, [causal] + [None] * n_joint, tops)
        finish(outs, carries, True)

    def single_block(tiles, mask):
        kts, vts = tiles
        outs, carries = _sb_blocks(qq_tiles(), [[t] for t in kts], [[t] for t in vts],
                                   [carry_ref[p] for p in range(N_PAIRS)],
                                   suffix_ones, row_lo, [mask])
        finish(outs, carries, False)

    def least_used(rows=None):
        def pick(x):
            return x if rows is None else jnp.concatenate([x[rows], x[tq + rows.start:tq + rows.stop]], axis=0)
        return jnp.min(functools.reduce(jnp.minimum, [pick(carry_ref[p]) for p in range(N_PAIRS)]))

    def finish_partial_block():
        @pl.when(least_used(slice(TOP_ROWS, tq)) < USED_STICK_CUTOFF)
        def _():
            single_block(joint_tiles(n_past - JOINT_PAST_BLOCKS), row >= TOP_ROWS)

    if isinstance(n_past, int):
        n_joint = min(JOINT_PAST_BLOCKS, n_past)
        first_step(n_joint)
        if n_joint > 1:
            finish_partial_block()
        next_block = n_past - n_joint - 1
    else:
        @pl.when(n_past >= JOINT_PAST_BLOCKS)
        def _():
            first_step(JOINT_PAST_BLOCKS)
            finish_partial_block()

        pl.when(n_past < JOINT_PAST_BLOCKS)(lambda: first_step(0))
        next_block = jnp.where(n_past >= JOINT_PAST_BLOCKS, n_past - JOINT_PAST_BLOCKS, n_past) - 1

    def body(state):
        j, _ = state
        single_block(walk_tiles(j), None)
        return j - 1, least_used()

    lax.while_loop(lambda s: (s[0] >= 0) & (s[1] < USED_STICK_CUTOFF), body,
                   (next_block, least_used()))


def _sb_prompt_kernel(q_ref, kt_ref, vt_ref, o_ref, carry_ref):
    def tiles(j):
        return ([kt_ref[j, rows, :] for rows in _PAIR_SLICES],
                [vt_ref[j, rows, :] for rows in _PAIR_SLICES])

    for r in range(Q_BLOCKS_PER_STEP):
        i = pl.program_id(1) * Q_BLOCKS_PER_STEP + r
        rows = pl.ds(r * KEY_BLOCK, KEY_BLOCK)
        _sb_core(q_ref.at[rows, :], o_ref.at[rows, :], carry_ref, i, 0,
                 functools.partial(tiles, i), tiles, tiles)


def _sb_prompt(q, kt3, vt3, batch, seq):
    nq = seq // (KEY_BLOCK * Q_BLOCKS_PER_STEP)
    blk = pl.BlockSpec((KEY_BLOCK * Q_BLOCKS_PER_STEP, SB_WIDTH), lambda b, i: (b * nq + i, 0))
    keys = pl.BlockSpec((seq // KEY_BLOCK, SB_WIDTH, KEY_BLOCK), lambda b, i: (b, 0, 0))
    return pl.pallas_call(
        _sb_prompt_kernel,
        grid=(batch, nq),
        in_specs=[blk, keys, keys],
        out_specs=blk,
        out_shape=jax.ShapeDtypeStruct((batch * seq, SB_WIDTH), F32),
        scratch_shapes=[pltpu.VMEM((N_PAIRS, 2 * KEY_BLOCK, LANES), F32)],
        compiler_params=_params(2),
        name="sb_prompt",
    )(q, kt3, vt3)


def _sb_sample_kernel(q_ref, kd_ref, vd_ref, kr_ref, vr_ref, kc_hbm, vc_hbm, o_ref,
                      carry_ref, kbuf_ref, vbuf_ref, sem_ref, *, past_len):
    tq = q_ref.shape[0]
    n_past = past_len // KEY_BLOCK
    n_recent = kr_ref.shape[2] // KEY_BLOCK
    b = pl.program_id(0)

    def diag_tiles():
        return ([kd_ref[0, rows, :] for rows in _PAIR_SLICES],
                [vd_ref[0, rows, :] for rows in _PAIR_SLICES])

    def recent_tiles(j):
        first = (j - (n_past - n_recent)) * KEY_BLOCK
        keys = slice(first, first + KEY_BLOCK)
        return ([kr_ref[0, rows, keys].astype(BF16) for rows in _PAIR_SLICES],
                [vr_ref[0, rows, keys].astype(BF16) for rows in _PAIR_SLICES])

    def cache_tiles(j):
        keys = pl.ds(pl.multiple_of(j * KEY_BLOCK, KEY_BLOCK), KEY_BLOCK)
        copies = [pltpu.make_async_copy(hbm.at[b, :, keys], buf, sem_ref.at[s])
                  for s, (hbm, buf) in enumerate(((kc_hbm, kbuf_ref), (vc_hbm, vbuf_ref)))]
        for c in copies:
            c.start()
        for c in copies:
            c.wait()
        return ([kbuf_ref[rows, :].astype(BF16) for rows in _PAIR_SLICES],
                [vbuf_ref[rows, :].astype(BF16) for rows in _PAIR_SLICES])

    assert n_recent == JOINT_PAST_BLOCKS
    _sb_core(q_ref, o_ref, carry_ref, n_past, (b % (KEY_BLOCK // tq)) * tq,
             diag_tiles, recent_tiles, cache_tiles)


def _sb_sample(q, kt3, vt3, cache_kt, cache_vt, t_new, first_row):
    batch, _, past_len = cache_kt.shape
    n_recent = JOINT_PAST_BLOCKS * KEY_BLOCK
    seqs_per_block = KEY_BLOCK // t_new
    new = pl.BlockSpec((1, SB_WIDTH, KEY_BLOCK),
                       lambda b: (first_row // KEY_BLOCK + b // seqs_per_block, 0, 0))
    recent = pl.BlockSpec((1, SB_WIDTH, n_recent), lambda b: (b, 0, past_len // n_recent - 1))
    hbm = pl.BlockSpec(memory_space=pl.ANY)
    return pl.pallas_call(
        functools.partial(_sb_sample_kernel, past_len=past_len),
        grid=(batch,),
        in_specs=[pl.BlockSpec((t_new, SB_WIDTH), lambda b: (first_row // t_new + b, 0)),
                  new, new, recent, recent, hbm, hbm],
        out_specs=pl.BlockSpec((t_new, SB_WIDTH), lambda b: (b, 0)),
        out_shape=jax.ShapeDtypeStruct((batch * t_new, SB_WIDTH), F32),
        scratch_shapes=[pltpu.VMEM((N_PAIRS, 2 * t_new, LANES), F32),
                        pltpu.VMEM((SB_WIDTH, KEY_BLOCK), F32),
                        pltpu.VMEM((SB_WIDTH, KEY_BLOCK), F32),
                        pltpu.SemaphoreType.DMA((2,))],
        compiler_params=_params(1),
        name="sb_sample",
    )(q, kt3, vt3, cache_kt, cache_vt, cache_kt, cache_vt)


def _mix_rows(x_ref, op_ref, os_ref, u_ref, gn_ref, ws_ref, bias_ref, gsb_ref, gsgu_ref, wout_ref,
              post_ref, sgu_ref, rows):
    lane_lo = lax.broadcasted_iota(jnp.int32, (1, LANES), 1) < HEAD_DIM
    for p in range(N_PAIRS):
        cols = slice(p * LANES, (p + 1) * LANES)
        for first in range(rows.start, rows.stop, SGU_CHUNK):
            chunk = slice(first, first + SGU_CHUNK)
            gg = _pair_rows(gn_ref[chunk, cols].astype(BF16), lane_lo)
            mixed = jnp.dot(ws_ref[0, p], gg, preferred_element_type=F32) + bias_ref[0, :, cols]
            sgu_ref[chunk, cols] = u_ref[chunk, cols] * mixed
    o_sb = jnp.where(_is_sample_step(), os_ref[rows, :], op_ref[rows, :])
    merged = jnp.concatenate([_rms(o_sb, gsb_ref[...]).astype(BF16),
                              _rms(sgu_ref[rows, :], gsgu_ref[...]).astype(BF16)], axis=1)
    y = jnp.dot(merged, wout_ref[...], preferred_element_type=F32)
    return x_ref[rows, :] + _rms(y, post_ref[...])


def _mix_ffn_kernel(x_ref, op_ref, os_ref, u_ref, gn_ref, ws_ref, bias_ref, gsb_ref, gsgu_ref,
                    wout_ref, mix_post_ref, pre_ref, post_ref, wg_ref, wu_ref, wd_ref,
                    yp_ref, ys_ref, sgu_ref, act_ref):
    def mix(rows):
        return _mix_rows(x_ref, op_ref, os_ref, u_ref, gn_ref, ws_ref, bias_ref, gsb_ref, gsgu_ref,
                         wout_ref, mix_post_ref, sgu_ref, rows)

    groups = [slice(r * FFN_ROWS, (r + 1) * FFN_ROWS) for r in range(ROW_TILE // FFN_ROWS)]
    x = mix(groups[0])
    for g, rows in enumerate(groups):
        stages = _ffn_stages(x, pre_ref, post_ref, wg_ref, wu_ref, wd_ref, act_ref, rows)
        stages[0]()
        stages[1]()
        if g + 1 < len(groups):
            x = mix(groups[g + 1])
        for stage in stages[2:]:
            y = stage()
        yp_ref[rows, :] = y

    @pl.when(_is_sample_step())
    def _():
        ys_ref[...] = yp_ref[...]


def _mix_ffn(x, o_prompt, o_sample, u, gn, ws_pairs, bias, g_sb, g_sgu, w_out, mix_post_g,
             pre_g, post_g, wg, wu, wd, n_prompt_tiles):
    row = _merged_tile_spec(D_MODEL, n_prompt_tiles)
    half = _merged_tile_spec(SB_WIDTH, n_prompt_tiles)
    which = lambda s: jnp.where(s == 0, 1, 0)
    sds = lambda rows: jax.ShapeDtypeStruct((rows, D_MODEL), F32)
    return pl.pallas_call(
        _mix_ffn_kernel,
        grid=(n_prompt_tiles + 1,),
        in_specs=[row, _prompt_tile_spec(SB_WIDTH), _sample_tile_spec(SB_WIDTH), half, half,
                  pl.BlockSpec((1,) + ws_pairs.shape[1:], lambda s: (which(s), 0, 0, 0)),
                  pl.BlockSpec((1,) + bias.shape[1:], lambda s: (which(s), 0, 0)),
                  _const_spec((1, SB_WIDTH)), _const_spec((1, SGU_WIDTH)),
                  _const_spec((D_MODEL, D_MODEL)), _const_spec((1, D_MODEL))] + _FFN_WEIGHT_SPECS,
        out_specs=[_prompt_tile_spec(D_MODEL), _sample_tile_spec(D_MODEL)],
        out_shape=[sds(n_prompt_tiles * ROW_TILE), sds(ROW_TILE)],
        scratch_shapes=[pltpu.VMEM((ROW_TILE, SGU_WIDTH), F32), pltpu.VMEM((ROW_TILE, D_FF), BF16)],
        compiler_params=_params(1),
        name="mix_ffn",
    )(x, o_prompt, o_sample, u, gn, ws_pairs, bias, g_sb, g_sgu, w_out, mix_post_g,
      pre_g, post_g, wg, wu, wd)


def _sgu_operands(w_s, b_s, t_new):
    i = jnp.arange(SGU_CHUNK)
    mask = (i[None, :] // SGU_CAUSAL_CHUNK) <= (i[:, None] // SGU_CAUSAL_CHUNK)
    w_prompt = w_s * mask[None].astype(w_s.dtype)
    reps = SGU_CHUNK // t_new
    corner = w_prompt[:, :t_new, :t_new]
    eye = jnp.eye(reps, dtype=w_s.dtype)
    w_sample = jnp.einsum('ab,gij->gaibj', eye, corner).reshape(w_s.shape)
    groups = w_s.shape[0]

    def pairs(w):
        w = w.reshape(groups // 2, 2, SGU_CHUNK, SGU_CHUNK).transpose(0, 2, 1, 3)
        return w.reshape(groups // 2, SGU_CHUNK, 2 * SGU_CHUNK)

    def bias(b):
        return jnp.repeat(b.T, SGU_WIDTH // groups, axis=1)

    ws = jnp.stack([pairs(w_prompt), pairs(w_sample)]).astype(BF16)
    return ws, jnp.stack([bias(b_s), bias(jnp.tile(b_s[:, :t_new], (1, reps)))])


def kernel(x_prompt, x_sample, cache_k_sb, cache_v_sb, ffn1_pre_g, ffn1_post_g, ffn1_w_gate, ffn1_w_up, ffn1_w_down, mix_pre_g, mix_post_g, w_in, sgu_ln_g, sgu_ln_b, sgu_w_s, sgu_b_s, g_out_sb, g_out_sgu, w_out, ffn2_pre_g, ffn2_post_g, ffn2_w_gate, ffn2_w_up, ffn2_w_down):
    depth, batch, seq = w_in.shape[0], x_prompt.shape[0], x_prompt.shape[1]
    dec_batch, t_new, past_len = x_sample.shape[0], x_sample.shape[1], cache_k_sb.shape[2]
    assert depth == 1 and dec_batch * t_new == ROW_TILE and seq % ROW_TILE == 0
    assert SGU_CHUNK % t_new == 0 and KEY_BLOCK % t_new == 0
    assert past_len % (JOINT_PAST_BLOCKS * KEY_BLOCK) == 0
    n_prompt_tiles = batch * seq // ROW_TILE

    mat = lambda w: w[0].astype(BF16)
    vec = lambda g: g[0][None, :]
    w_qug = jnp.concatenate([w_in[0][:, :SB_WIDTH], w_in[0][:, 3 * SB_WIDTH:]], axis=1).astype(BF16)
    w_kvt = w_in[0][:, SB_WIDTH:3 * SB_WIDTH].T.astype(BF16)

    x = _ffn(x_prompt.reshape(batch * seq, D_MODEL), x_sample.reshape(ROW_TILE, D_MODEL),
             vec(ffn1_pre_g), vec(ffn1_post_g), mat(ffn1_w_gate), mat(ffn1_w_up), mat(ffn1_w_down),
             n_prompt_tiles)
    q, u, gn, kt3, vt3, kt_p, vt_p, kt_s, vt_s = _inproj(
        x, vec(mix_pre_g), w_qug, w_kvt, vec(sgu_ln_g), vec(sgu_ln_b), n_prompt_tiles, seq // ROW_TILE)

    o_prompt = _sb_prompt(q, kt3, vt3, batch, seq)
    cache_t = lambda c: c[0].transpose(0, 2, 3, 1).reshape(dec_batch, SB_WIDTH, past_len)
    o_sample = _sb_sample(q, kt3, vt3, cache_t(cache_k_sb), cache_t(cache_v_sb), t_new, batch * seq)

    ws_pairs, bias = _sgu_operands(sgu_w_s[0], sgu_b_s[0], t_new)
    y_p, y_s = _mix_ffn(x, o_prompt, o_sample, u, gn, ws_pairs, bias, vec(g_out_sb), vec(g_out_sgu),
                        mat(w_out), vec(mix_post_g), vec(ffn2_pre_g), vec(ffn2_post_g),
                        mat(ffn2_w_gate), mat(ffn2_w_up), mat(ffn2_w_down), n_prompt_tiles)

    def prompt_heads(t):
        return t.reshape(batch, N_HEADS, HEAD_DIM, seq).transpose(0, 3, 1, 2)[None]

    def sample_heads(t):
        return t.reshape(N_HEADS, HEAD_DIM, dec_batch, t_new).transpose(2, 3, 0, 1)[None]

    g_s = gn[batch * seq:].reshape(1, dec_batch, t_new, N_HEADS, HEAD_DIM)
    return (y_p.reshape(batch, seq, D_MODEL), y_s.reshape(dec_batch, t_new, D_MODEL),
            prompt_heads(kt_p), prompt_heads(vt_p), sample_heads(kt_s), sample_heads(vt_s), g_s)
```

```python
import functools

import jax
import jax.numpy as jnp
from jax import lax
from jax.experimental import pallas as pl
from jax.experimental.pallas import tpu as pltpu

D_MODEL = 1024
D_FF = 2816
SB_WIDTH = 512
SGU_WIDTH = 512
HEAD_DIM = 64
N_HEADS = SB_WIDTH // HEAD_DIM
SGU_CHUNK = 128
SGU_CAUSAL_CHUNK = 64
FFN_RES = 0.5
EPS = 1e-6

LANES = 128
N_PAIRS = SB_WIDTH // LANES
KEY_BLOCK = 128
JOINT_PAST_BLOCKS = 2
TOP_ROWS = 32
Q_BLOCKS_PER_STEP = 2
FF_CHUNKS = (1536, 1280)
ROW_TILE = 512
FFN_ROW_GROUPS = (slice(0, 128), slice(128, 384), slice(384, 512))
VMEM_LIMIT = 56 * 1024 * 1024

LOG2_E = 1.4426950408889634
USED_STICK_CUTOFF = 105.0 * LOG2_E
HIDDEN_LOGIT = -1e30

F32 = jnp.float32
BF16 = jnp.bfloat16


def _rms(x, g):
    return x * lax.rsqrt(jnp.mean(x * x, axis=-1, keepdims=True) + EPS) * g


def _gelu(x):
    return 0.5 * x * (1.0 + lax.erf(x * (0.5 ** 0.5)))


def _const_spec(shape):
    return pl.BlockSpec(shape, lambda *_: (0,) * len(shape), pipeline_mode=pl.Buffered(1))


def _params(n_axes):
    return pltpu.CompilerParams(dimension_semantics=("arbitrary",) * n_axes,
                                vmem_limit_bytes=VMEM_LIMIT)


def _is_sample_step():
    return pl.program_id(0) == 0


def _merged_tile_spec(width, n_prompt_tiles):
    return pl.BlockSpec((ROW_TILE, width), lambda s: ((s + n_prompt_tiles) % (n_prompt_tiles + 1), 0))


def _prompt_tile_spec(width, rows=ROW_TILE):
    return pl.BlockSpec((rows, width), lambda s: (jnp.maximum(s - 1, 0), 0))


def _sample_tile_spec(width, rows=ROW_TILE):
    return pl.BlockSpec((rows, width), lambda s: (0, 0))


def _ffn_stages(x, pre_ref, post_ref, wg_ref, wu_ref, wd_ref, act_ref, rows):
    state = {}

    def hidden():
        state['h'] = _rms(x, pre_ref[...]).astype(BF16)

    def chunk(cols):
        gate = jnp.dot(state['h'], wg_ref[:, cols], preferred_element_type=F32)
        up = jnp.dot(state['h'], wu_ref[:, cols], preferred_element_type=F32)
        act_ref[rows, cols] = (gate * jax.nn.sigmoid(gate) * up).astype(BF16)

    def finish():
        f = jnp.dot(act_ref[rows, :], wd_ref[...], preferred_element_type=F32)
        return x + FFN_RES * _rms(f, post_ref[...])

    stages, start = [hidden], 0
    for width in FF_CHUNKS:
        stages.append(functools.partial(chunk, slice(start, start + width)))
        start += width
    return stages + [finish]


def _ffn_rows(x, pre_ref, post_ref, wg_ref, wu_ref, wd_ref, act_ref, rows):
    for stage in _ffn_stages(x, pre_ref, post_ref, wg_ref, wu_ref, wd_ref, act_ref, rows):
        y = stage()
    return y


_FFN_WEIGHT_SPECS = [_const_spec((1, D_MODEL)), _const_spec((1, D_MODEL)),
                     _const_spec((D_MODEL, D_FF)), _const_spec((D_MODEL, D_FF)),
                     _const_spec((D_FF, D_MODEL))]


def _ffn_kernel(xp_ref, xs_ref, pre_ref, post_ref, wg_ref, wu_ref, wd_ref, o_ref, act_ref):
    for rows in FFN_ROW_GROUPS:
        x = jnp.where(_is_sample_step(), xs_ref[rows, :], xp_ref[rows, :])
        o_ref[rows, :] = _ffn_rows(x, pre_ref, post_ref, wg_ref, wu_ref, wd_ref, act_ref, rows)


def _ffn(x_prompt, x_sample, pre_g, post_g, wg, wu, wd, n_prompt_tiles):
    n_tiles = n_prompt_tiles + 1
    return pl.pallas_call(
        _ffn_kernel,
        grid=(n_tiles,),
        in_specs=[_prompt_tile_spec(D_MODEL), _sample_tile_spec(D_MODEL)] + _FFN_WEIGHT_SPECS,
        out_specs=_merged_tile_spec(D_MODEL, n_prompt_tiles),
        out_shape=jax.ShapeDtypeStruct((n_tiles * ROW_TILE, D_MODEL), F32),
        scratch_shapes=[pltpu.VMEM((ROW_TILE, D_FF), BF16)],
        compiler_params=_params(1),
        name="ffn",
    )(x_prompt, x_sample, pre_g, post_g, wg, wu, wd)


def _inproj_kernel(x_ref, g_ref, wqug_ref, wkvt_ref, lng_ref, lnb_ref,
                   q_ref, u_ref, gn_ref, kt3_ref, vt3_ref, ktp_ref, vtp_ref, kts_ref, vts_ref):
    h = _rms(x_ref[...], g_ref[...]).astype(BF16)
    z = jnp.dot(h, wqug_ref[...], preferred_element_type=F32)
    kvt = lax.dot_general(wkvt_ref[...], h, (((1,), (1,)), ((), ())), preferred_element_type=F32)
    w = SB_WIDTH
    q_ref[...] = (z[:, 0:w] * (HEAD_DIM ** -0.5 * LOG2_E)).astype(BF16)
    u_ref[...] = _gelu(z[:, w:w + SGU_WIDTH])
    ge = _gelu(z[:, w + SGU_WIDTH:])
    xc = ge - jnp.mean(ge, axis=-1, keepdims=True)
    y = xc * lax.rsqrt(jnp.mean(xc * xc, axis=-1, keepdims=True) + EPS)
    gn_ref[...] = y * lng_ref[...] + lnb_ref[...]
    for t_ref, t3_ref, rows in ((ktp_ref, kt3_ref, slice(0, w)), (vtp_ref, vt3_ref, slice(w, 2 * w))):
        t = kvt[rows]
        t_ref[...] = t
        for c in range(ROW_TILE // KEY_BLOCK):
            t3_ref[c] = t[:, c * KEY_BLOCK:(c + 1) * KEY_BLOCK].astype(BF16)

    @pl.when(_is_sample_step())
    def _():
        kts_ref[...] = ktp_ref[...]
        vts_ref[...] = vtp_ref[...]


def _inproj(x, g, w_qug, w_kvt, ln_g, ln_b, n_prompt_tiles, prompt_tiles_per_row):
    n = x.shape[0]
    half = _merged_tile_spec(SB_WIDTH, n_prompt_tiles)
    blocks_per_tile = ROW_TILE // KEY_BLOCK
    key_blocks = pl.BlockSpec((blocks_per_tile, SB_WIDTH, KEY_BLOCK),
                              lambda s: ((s + n_prompt_tiles) % (n_prompt_tiles + 1), 0, 0))

    def prompt_t_index(s):
        tile = jnp.maximum(s - 1, 0)
        return tile // prompt_tiles_per_row, tile % prompt_tiles_per_row

    prompt_t = pl.BlockSpec((SB_WIDTH, ROW_TILE), prompt_t_index)
    prompt_t_sds = jax.ShapeDtypeStruct(
        (n_prompt_tiles // prompt_tiles_per_row * SB_WIDTH, prompt_tiles_per_row * ROW_TILE), F32)
    sample_t_sds = jax.ShapeDtypeStruct((SB_WIDTH, ROW_TILE), F32)
    f32_half = jax.ShapeDtypeStruct((n, SB_WIDTH), F32)
    key_blocks_sds = jax.ShapeDtypeStruct((n // KEY_BLOCK, SB_WIDTH, KEY_BLOCK), BF16)
    return pl.pallas_call(
        _inproj_kernel,
        grid=(n // ROW_TILE,),
        in_specs=[_merged_tile_spec(D_MODEL, n_prompt_tiles), _const_spec((1, D_MODEL)),
                  _const_spec(w_qug.shape), _const_spec(w_kvt.shape), _const_spec((1, SGU_WIDTH)),
                  _const_spec((1, SGU_WIDTH))],
        out_specs=[half] * 3 + [key_blocks] * 2 + [prompt_t] * 2 + [_sample_tile_spec(ROW_TILE, SB_WIDTH)] * 2,
        out_shape=[jax.ShapeDtypeStruct((n, SB_WIDTH), BF16), f32_half, f32_half,
                   key_blocks_sds, key_blocks_sds, prompt_t_sds, prompt_t_sds,
                   sample_t_sds, sample_t_sds],
        compiler_params=_params(1),
        name="inproj",
    )(x, g, w_qug, w_kvt, ln_g, ln_b)


def _pair_rows(x, lane_lo):
    zero = jnp.zeros_like(x)
    return jnp.concatenate([jnp.where(lane_lo, x, zero), jnp.where(lane_lo, zero, x)], axis=0)


def _sb_blocks(qqs, kts, vts, carries, suffix_ones, lane_lo, masks, tops=None):
    n_blocks = len(masks)
    tops = tops or [None] * n_blocks
    t = qqs[0].shape[0] // 2
    nt_dims = (((1,), (1,)), ((), ()))

    def head_rows(x, top):
        return x if top is None else jnp.concatenate([x[:top], x[t:t + top]], axis=0)

    def weighted_values(a, vt):
        both = lax.dot_general(a, vt, nt_dims, preferred_element_type=F32)
        rows = a.shape[0] // 2
        return jnp.where(lane_lo, both[:rows], both[rows:])

    zs = []
    for qq, kts_p in zip(qqs, kts):
        zs_p, b = [], 0
        while b < n_blocks:
            if b + 1 < n_blocks and tops[b] is None and tops[b + 1] is None:
                z2 = jnp.dot(qq, jnp.concatenate([kts_p[b], kts_p[b + 1]], axis=1),
                             preferred_element_type=F32)
                zs_p += [z2[:, :KEY_BLOCK], z2[:, KEY_BLOCK:]]
                b += 2
            else:
                zs_p.append(jnp.dot(head_rows(qq, tops[b]), kts_p[b], preferred_element_type=F32))
                b += 1
        zs.append(zs_p)
    useds, splits = [], []
    for zs_p in zs:
        useds.append([])
        splits.append([])
        for i, mask in enumerate(masks):
            z = zs_p[i]
            if mask is not None:
                z = zs_p[i] = jnp.where(mask, z, HIDDEN_LOGIT)
            used = jnp.maximum(z, 0.0) + jnp.log(1.0 + jnp.exp2(-jnp.abs(z))) * LOG2_E
            hi = used.astype(BF16)
            lo = (used - hi.astype(F32)).astype(BF16)
            useds[-1].append(used)
            splits[-1].append(jnp.concatenate([hi, lo], axis=1))
    sums = [[jnp.dot(s, suffix_ones, preferred_element_type=F32) for s in splits_p]
            for splits_p in splits]
    outs, new_carries = [], []
    for zs_p, useds_p, sums_p, vts_p, carry in zip(zs, useds, sums, vts, carries):
        weights, values_t, top_outs = [], [], []
        for z, used, s, vt, top in zip(zs_p, useds_p, sums_p, vts_p, tops):
            a = jnp.exp2(z - used - (s[:, :KEY_BLOCK] + head_rows(carry, top))).astype(BF16)
            row_sum = s[:, KEY_BLOCK:]
            if top is None:
                weights.append(a)
                values_t.append(vt)
                carry = carry + row_sum
            else:
                top_outs.append(weighted_values(a, vt))
                carry = jnp.concatenate([carry[:top] + row_sum[:top], carry[top:t],
                                         carry[t:t + top] + row_sum[top:], carry[t + top:]], axis=0)
        out = weighted_values(jnp.concatenate(weights, axis=1), jnp.concatenate(values_t, axis=1))
        for o_top in top_outs:
            top = o_top.shape[0]
            out = jnp.concatenate([out[:top] + o_top, out[top:]], axis=0)
        outs.append(out)
        new_carries.append(carry)
    return outs, new_carries


def _suffix_ones():
    r = lax.broadcasted_iota(jnp.int32, (2 * KEY_BLOCK, 2 * KEY_BLOCK), 0) % KEY_BLOCK
    c = lax.broadcasted_iota(jnp.int32, (2 * KEY_BLOCK, 2 * KEY_BLOCK), 1)
    return jnp.where((c >= KEY_BLOCK) | (r > c), 1.0, 0.0).astype(BF16)


_PAIR_SLICES = [slice(p * LANES, (p + 1) * LANES) for p in range(N_PAIRS)]


class _QueryBlock:
    def __init__(self, q_ref, o_ref, carry_ref, n_past, diag_tiles, joint_tiles, walk_tiles):
        self.q_ref, self.o_ref, self.carry_ref, self.n_past = q_ref, o_ref, carry_ref, n_past
        self.diag_tiles, self.joint_tiles, self.walk_tiles = diag_tiles, joint_tiles, walk_tiles


class _StickBreaking:
    def __init__(self, tq, diag_key_offset):
        self.tq = tq
        self.lane_lo = lax.broadcasted_iota(jnp.int32, (1, LANES), 1) < HEAD_DIM
        self.suffix_ones = _suffix_ones()
        self.row = lax.broadcasted_iota(jnp.int32, (2 * tq, KEY_BLOCK), 0) % tq
        col = lax.broadcasted_iota(jnp.int32, (2 * tq, KEY_BLOCK), 1) - diag_key_offset
        self.causal = (col >= 0) & (col < self.row)

    def _qq_tiles(self, blk):
        return [_pair_rows(blk.q_ref[:, cols], self.lane_lo) for cols in _PAIR_SLICES]

    def _store(self, blk, outs, carries, first):
        for p, cols in enumerate(_PAIR_SLICES):
            if first:
                blk.o_ref[:, cols] = outs[p]
            else:
                blk.o_ref[:, cols] += outs[p]
            blk.carry_ref[p] = carries[p]

    def first_step(self, blks, n_joint):
        qqs, kts, vts = [], [], []
        for blk in blks:
            kd, vd = blk.diag_tiles()
            kts_b, vts_b = [[t] for t in kd], [[t] for t in vd]
            for b in range(n_joint):
                kj, vj = blk.joint_tiles(blk.n_past - 1 - b)
                for p in range(N_PAIRS):
                    kts_b[p].append(kj[p])
                    vts_b[p].append(vj[p])
            qqs += self._qq_tiles(blk)
            kts += kts_b
            vts += vts_b
        tops = [None] * n_joint + ([TOP_ROWS] if n_joint > 1 else [None])
        outs, carries = _sb_blocks(qqs, kts, vts, [jnp.zeros((2 * self.tq, LANES), F32)] * len(qqs),
                                   self.suffix_ones, self.lane_lo, [self.causal] + [None] * n_joint, tops)
        for i, blk in enumerate(blks):
            pairs = slice(i * N_PAIRS, (i + 1) * N_PAIRS)
            self._store(blk, outs[pairs], carries[pairs], True)

    def _single_block(self, blk, tiles, mask):
        kts, vts = tiles
        outs, carries = _sb_blocks(self._qq_tiles(blk), [[t] for t in kts], [[t] for t in vts],
                                   [blk.carry_ref[p] for p in range(N_PAIRS)],
                                   self.suffix_ones, self.lane_lo, [mask])
        self._store(blk, outs, carries, False)

    def _least_used(self, blk, rows=None):
        tq = self.tq

        def pick(x):
            return x if rows is None else jnp.concatenate([x[rows], x[tq + rows.start:tq + rows.stop]], axis=0)
        return jnp.min(functools.reduce(jnp.minimum, [pick(blk.carry_ref[p]) for p in range(N_PAIRS)]))

    def finish_partial_block(self, blk):
        @pl.when(self._least_used(blk, slice(TOP_ROWS, self.tq)) < USED_STICK_CUTOFF)
        def _():
            self._single_block(blk, blk.joint_tiles(blk.n_past - JOINT_PAST_BLOCKS), self.row >= TOP_ROWS)

    def walk(self, blk, next_block):
        def body(state):
            j, _ = state
            self._single_block(blk, blk.walk_tiles(j), None)
            return j - 1, self._least_used(blk)

        lax.while_loop(lambda s: (s[0] >= 0) & (s[1] < USED_STICK_CUTOFF), body,
                       (next_block, self._least_used(blk)))


def _sb_prompt_kernel(q_ref, kt_ref, vt_ref, o_ref, carry_ref):
    def tiles(j):
        return ([kt_ref[j, rows, :] for rows in _PAIR_SLICES],
                [vt_ref[j, rows, :] for rows in _PAIR_SLICES])

    step = pl.program_id(1)
    sb = _StickBreaking(KEY_BLOCK, 0)
    blks = []
    for r in range(Q_BLOCKS_PER_STEP):
        i = step * Q_BLOCKS_PER_STEP + r
        rows = pl.ds(r * KEY_BLOCK, KEY_BLOCK)
        blks.append(_QueryBlock(q_ref.at[rows, :], o_ref.at[rows, :], carry_ref.at[r], i,
                                functools.partial(tiles, i), tiles, tiles))

    assert Q_BLOCKS_PER_STEP >= JOINT_PAST_BLOCKS > 1

    @pl.when(step > 0)
    def _():
        sb.first_step(blks, JOINT_PAST_BLOCKS)
        for blk in blks:
            sb.finish_partial_block(blk)

    @pl.when(step == 0)
    def _():
        for r, blk in enumerate(blks):
            sb.first_step([blk], min(r, JOINT_PAST_BLOCKS))
            if r >= JOINT_PAST_BLOCKS:
                sb.finish_partial_block(blk)

    for r, blk in enumerate(blks):
        sb.walk(blk, jnp.where(step > 0, blk.n_past - JOINT_PAST_BLOCKS,
                               r - min(r, JOINT_PAST_BLOCKS)) - 1)


def _sb_prompt(q, kt3, vt3, batch, seq):
    nq = seq // (KEY_BLOCK * Q_BLOCKS_PER_STEP)
    blk = pl.BlockSpec((KEY_BLOCK * Q_BLOCKS_PER_STEP, SB_WIDTH), lambda b, i: (b * nq + i, 0))
    keys = pl.BlockSpec((seq // KEY_BLOCK, SB_WIDTH, KEY_BLOCK), lambda b, i: (b, 0, 0))
    return pl.pallas_call(
        _sb_prompt_kernel,
        grid=(batch, nq),
        in_specs=[blk, keys, keys],
        out_specs=blk,
        out_shape=jax.ShapeDtypeStruct((batch * seq, SB_WIDTH), F32),
        scratch_shapes=[pltpu.VMEM((Q_BLOCKS_PER_STEP, N_PAIRS, 2 * KEY_BLOCK, LANES), F32)],
        compiler_params=_params(2),
        name="sb_prompt",
    )(q, kt3, vt3)


def _sb_sample_kernel(q_ref, kd_ref, vd_ref, kr_ref, vr_ref, kc_hbm, vc_hbm, o_ref,
                      carry_ref, kbuf_ref, vbuf_ref, sem_ref, *, past_len):
    tq = q_ref.shape[0]
    n_past = past_len // KEY_BLOCK
    n_recent = kr_ref.shape[2] // KEY_BLOCK
    b = pl.program_id(0)

    def diag_tiles():
        return ([kd_ref[0, rows, :] for rows in _PAIR_SLICES],
                [vd_ref[0, rows, :] for rows in _PAIR_SLICES])

    def recent_tiles(j):
        first = (j - (n_past - n_recent)) * KEY_BLOCK
        keys = slice(first, first + KEY_BLOCK)
        return ([kr_ref[0, rows, keys].astype(BF16) for rows in _PAIR_SLICES],
                [vr_ref[0, rows, keys].astype(BF16) for rows in _PAIR_SLICES])

    def cache_tiles(j):
        keys = pl.ds(pl.multiple_of(j * KEY_BLOCK, KEY_BLOCK), KEY_BLOCK)
        copies = [pltpu.make_async_copy(hbm.at[b, :, keys], buf, sem_ref.at[s])
                  for s, (hbm, buf) in enumerate(((kc_hbm, kbuf_ref), (vc_hbm, vbuf_ref)))]
        for c in copies:
            c.start()
        for c in copies:
            c.wait()
        return ([kbuf_ref[rows, :].astype(BF16) for rows in _PAIR_SLICES],
                [vbuf_ref[rows, :].astype(BF16) for rows in _PAIR_SLICES])

    assert n_recent == JOINT_PAST_BLOCKS > 1
    sb = _StickBreaking(tq, (b % (KEY_BLOCK // tq)) * tq)
    blk = _QueryBlock(q_ref, o_ref, carry_ref, n_past, diag_tiles, recent_tiles, cache_tiles)
    sb.first_step([blk], JOINT_PAST_BLOCKS)
    sb.finish_partial_block(blk)
    sb.walk(blk, n_past - JOINT_PAST_BLOCKS - 1)


def _sb_sample(q, kt3, vt3, cache_kt, cache_vt, t_new, first_row):
    batch, _, past_len = cache_kt.shape
    n_recent = JOINT_PAST_BLOCKS * KEY_BLOCK
    seqs_per_block = KEY_BLOCK // t_new
    new = pl.BlockSpec((1, SB_WIDTH, KEY_BLOCK),
                       lambda b: (first_row // KEY_BLOCK + b // seqs_per_block, 0, 0))
    recent = pl.BlockSpec((1, SB_WIDTH, n_recent), lambda b: (b, 0, past_len // n_recent - 1))
    hbm = pl.BlockSpec(memory_space=pl.ANY)
    return pl.pallas_call(
        functools.partial(_sb_sample_kernel, past_len=past_len),
        grid=(batch,),
        in_specs=[pl.BlockSpec((t_new, SB_WIDTH), lambda b: (first_row // t_new + b, 0)),
                  new, new, recent, recent, hbm, hbm],
        out_specs=pl.BlockSpec((t_new, SB_WIDTH), lambda b: (b, 0)),
        out_shape=jax.ShapeDtypeStruct((batch * t_new, SB_WIDTH), F32),
        scratch_shapes=[pltpu.VMEM((N_PAIRS, 2 * t_new, LANES), F32),
                        pltpu.VMEM((SB_WIDTH, KEY_BLOCK), F32),
                        pltpu.VMEM((SB_WIDTH, KEY_BLOCK), F32),
                        pltpu.SemaphoreType.DMA((2,))],
        compiler_params=_params(1),
        name="sb_sample",
    )(q, kt3, vt3, cache_kt, cache_vt, cache_kt, cache_vt)


def _mix_rows(x_ref, op_ref, os_ref, u_ref, gn_ref, ws_ref, bias_ref, gsb_ref, gsgu_ref, wout_ref,
              post_ref, sgu_ref, rows):
    lane_lo = lax.broadcasted_iota(jnp.int32, (1, LANES), 1) < HEAD_DIM
    for p in range(N_PAIRS):
        cols = slice(p * LANES, (p + 1) * LANES)
        for first in range(rows.start, rows.stop, SGU_CHUNK):
            chunk = slice(first, first + SGU_CHUNK)
            gg = _pair_rows(gn_ref[chunk, cols].astype(BF16), lane_lo)
            mixed = jnp.dot(ws_ref[0, p], gg, preferred_element_type=F32) + bias_ref[0, :, cols]
            sgu_ref[chunk, cols] = u_ref[chunk, cols] * mixed
    o_sb = jnp.where(_is_sample_step(), os_ref[rows, :], op_ref[rows, :])
    merged = jnp.concatenate([_rms(o_sb, gsb_ref[...]).astype(BF16),
                              _rms(sgu_ref[rows, :], gsgu_ref[...]).astype(BF16)], axis=1)
    y = jnp.dot(merged, wout_ref[...], preferred_element_type=F32)
    return x_ref[rows, :] + _rms(y, post_ref[...])


def _mix_ffn_kernel(x_ref, op_ref, os_ref, u_ref, gn_ref, ws_ref, bias_ref, gsb_ref, gsgu_ref,
                    wout_ref, mix_post_ref, pre_ref, post_ref, wg_ref, wu_ref, wd_ref,
                    yp_ref, ys_ref, sgu_ref, act_ref):
    def mix(rows):
        return _mix_rows(x_ref, op_ref, os_ref, u_ref, gn_ref, ws_ref, bias_ref, gsb_ref, gsgu_ref,
                         wout_ref, mix_post_ref, sgu_ref, rows)

    groups = FFN_ROW_GROUPS
    x = mix(groups[0])
    for g, rows in enumerate(groups):
        stages = _ffn_stages(x, pre_ref, post_ref, wg_ref, wu_ref, wd_ref, act_ref, rows)
        stages[0]()
        stages[1]()
        if g + 1 < len(groups):
            x = mix(groups[g + 1])
        for stage in stages[2:]:
            y = stage()
        yp_ref[rows, :] = y

    @pl.when(_is_sample_step())
    def _():
        ys_ref[...] = yp_ref[...]


def _mix_ffn(x, o_prompt, o_sample, u, gn, ws_pairs, bias, g_sb, g_sgu, w_out, mix_post_g,
             pre_g, post_g, wg, wu, wd, n_prompt_tiles):
    row = _merged_tile_spec(D_MODEL, n_prompt_tiles)
    half = _merged_tile_spec(SB_WIDTH, n_prompt_tiles)
    which = lambda s: jnp.where(s == 0, 1, 0)
    sds = lambda rows: jax.ShapeDtypeStruct((rows, D_MODEL), F32)
    return pl.pallas_call(
        _mix_ffn_kernel,
        grid=(n_prompt_tiles + 1,),
        in_specs=[row, _prompt_tile_spec(SB_WIDTH), _sample_tile_spec(SB_WIDTH), half, half,
                  pl.BlockSpec((1,) + ws_pairs.shape[1:], lambda s: (which(s), 0, 0, 0)),
                  pl.BlockSpec((1,) + bias.shape[1:], lambda s: (which(s), 0, 0)),
                  _const_spec((1, SB_WIDTH)), _const_spec((1, SGU_WIDTH)),
                  _const_spec((D_MODEL, D_MODEL)), _const_spec((1, D_MODEL))] + _FFN_WEIGHT_SPECS,
        out_specs=[_prompt_tile_spec(D_MODEL), _sample_tile_spec(D_MODEL)],
        out_shape=[sds(n_prompt_tiles * ROW_TILE), sds(ROW_TILE)],
        scratch_shapes=[pltpu.VMEM((ROW_TILE, SGU_WIDTH), F32), pltpu.VMEM((ROW_TILE, D_FF), BF16)],
        compiler_params=_params(1),
        name="mix_ffn",
    )(x, o_prompt, o_sample, u, gn, ws_pairs, bias, g_sb, g_sgu, w_out, mix_post_g,
      pre_g, post_g, wg, wu, wd)


def _sgu_operands(w_s, b_s, t_new):
    i = jnp.arange(SGU_CHUNK)
    mask = (i[None, :] // SGU_CAUSAL_CHUNK) <= (i[:, None] // SGU_CAUSAL_CHUNK)
    w_prompt = w_s * mask[None].astype(w_s.dtype)
    reps = SGU_CHUNK // t_new
    corner = w_prompt[:, :t_new, :t_new]
    eye = jnp.eye(reps, dtype=w_s.dtype)
    w_sample = jnp.einsum('ab,gij->gaibj', eye, corner).reshape(w_s.shape)
    groups = w_s.shape[0]

    def pairs(w):
        w = w.reshape(groups // 2, 2, SGU_CHUNK, SGU_CHUNK).transpose(0, 2, 1, 3)
        return w.reshape(groups // 2, SGU_CHUNK, 2 * SGU_CHUNK)

    def bias(b):
        return jnp.repeat(b.T, SGU_WIDTH // groups, axis=1)

    ws = jnp.stack([pairs(w_prompt), pairs(w_sample)]).astype(BF16)
    return ws, jnp.stack([bias(b_s), bias(jnp.tile(b_s[:, :t_new], (1, reps)))])


def kernel(x_prompt, x_sample, cache_k_sb, cache_v_sb, ffn1_pre_g, ffn1_post_g, ffn1_w_gate, ffn1_w_up, ffn1_w_down, mix_pre_g, mix_post_g, w_in, sgu_ln_g, sgu_ln_b, sgu_w_s, sgu_b_s, g_out_sb, g_out_sgu, w_out, ffn2_pre_g, ffn2_post_g, ffn2_w_gate, ffn2_w_up, ffn2_w_down):
    depth, batch, seq = w_in.shape[0], x_prompt.shape[0], x_prompt.shape[1]
    dec_batch, t_new, past_len = x_sample.shape[0], x_sample.shape[1], cache_k_sb.shape[2]
    assert depth == 1 and dec_batch * t_new == ROW_TILE and seq % ROW_TILE == 0
    assert SGU_CHUNK % t_new == 0 and KEY_BLOCK % t_new == 0
    assert past_len % (JOINT_PAST_BLOCKS * KEY_BLOCK) == 0
    n_prompt_tiles = batch * seq // ROW_TILE

    mat = lambda w: w[0].astype(BF16)
    vec = lambda g: g[0][None, :]
    w_qug = jnp.concatenate([w_in[0][:, :SB_WIDTH], w_in[0][:, 3 * SB_WIDTH:]], axis=1).astype(BF16)
    w_kvt = w_in[0][:, SB_WIDTH:3 * SB_WIDTH].T.astype(BF16)

    x = _ffn(x_prompt.reshape(batch * seq, D_MODEL), x_sample.reshape(ROW_TILE, D_MODEL),
             vec(ffn1_pre_g), vec(ffn1_post_g), mat(ffn1_w_gate), mat(ffn1_w_up), mat(ffn1_w_down),
             n_prompt_tiles)
    q, u, gn, kt3, vt3, kt_p, vt_p, kt_s, vt_s = _inproj(
        x, vec(mix_pre_g), w_qug, w_kvt, vec(sgu_ln_g), vec(sgu_ln_b), n_prompt_tiles, seq // ROW_TILE)

    o_prompt = _sb_prompt(q, kt3, vt3, batch, seq)
    cache_t = lambda c: c[0].transpose(0, 2, 3, 1).reshape(dec_batch, SB_WIDTH, past_len)
    o_sample = _sb_sample(q, kt3, vt3, cache_t(cache_k_sb), cache_t(cache_v_sb), t_new, batch * seq)

    ws_pairs, bias = _sgu_operands(sgu_w_s[0], sgu_b_s[0], t_new)
    y_p, y_s = _mix_ffn(x, o_prompt, o_sample, u, gn, ws_pairs, bias, vec(g_out_sb), vec(g_out_sgu),
                        mat(w_out), vec(mix_post_g), vec(ffn2_pre_g), vec(ffn2_post_g),
                        mat(ffn2_w_gate), mat(ffn2_w_up), mat(ffn2_w_down), n_prompt_tiles)

    def prompt_heads(t):
        return t.reshape(batch, N_HEADS, HEAD_DIM, seq).transpose(0, 3, 1, 2)[None]

    def sample_heads(t):
        return t.reshape(N_HEADS, HEAD_DIM, dec_batch, t_new).transpose(2, 3, 0, 1)[None]

    g_s = gn[batch * seq:].reshape(1, dec_batch, t_new, N_HEADS, HEAD_DIM)
    return (y_p.reshape(batch, seq, D_MODEL), y_s.reshape(dec_batch, t_new, D_MODEL),
            prompt_heads(kt_p), prompt_heads(vt_p), sample_heads(kt_s), sample_heads(vt_s), g_s)
```

```python
import functools

import jax
import jax.numpy as jnp
from jax import lax
from jax.experimental import pallas as pl
from jax.experimental.pallas import tpu as pltpu

D_MODEL = 1024
D_FF = 2816
SB_WIDTH = 512
SGU_WIDTH = 512
HEAD_DIM = 64
N_HEADS = SB_WIDTH // HEAD_DIM
SGU_CHUNK = 128
SGU_CAUSAL_CHUNK = 64
FFN_RES = 0.5
EPS = 1e-6

LANES = 128
N_PAIRS = SB_WIDTH // LANES
KEY_BLOCK = 128
JOINT_PAST_BLOCKS = 2
TOP_ROWS = 32
Q_BLOCKS_PER_STEP = 2
FF_CHUNKS = (1536, 1280)
ROW_TILE = 512
WEIGHT_CHUNK_ROWS = 128
FFN_ROW_GROUPS = (slice(0, 256), slice(256, 512))
VMEM_LIMIT = 56 * 1024 * 1024

LOG2_E = 1.4426950408889634
USED_STICK_CUTOFF = 105.0 * LOG2_E
HIDDEN_LOGIT = -1e30

F32 = jnp.float32
BF16 = jnp.bfloat16


def _rms(x, g):
    return x * lax.rsqrt(jnp.mean(x * x, axis=-1, keepdims=True) + EPS) * g


def _gelu(x):
    return 0.5 * x * (1.0 + lax.erf(x * (0.5 ** 0.5)))


def _const_spec(shape):
    return pl.BlockSpec(shape, lambda *_: (0,) * len(shape), pipeline_mode=pl.Buffered(1))


def _params(n_axes):
    return pltpu.CompilerParams(dimension_semantics=("arbitrary",) * n_axes,
                                vmem_limit_bytes=VMEM_LIMIT)


def _is_sample_step():
    return pl.program_id(0) == 0


def _merged_tile_spec(width, n_prompt_tiles):
    return pl.BlockSpec((ROW_TILE, width), lambda s: ((s + n_prompt_tiles) % (n_prompt_tiles + 1), 0))


def _prompt_tile_spec(width, rows=ROW_TILE):
    return pl.BlockSpec((rows, width), lambda s: (jnp.maximum(s - 1, 0), 0))


def _sample_tile_spec(width, rows=ROW_TILE):
    return pl.BlockSpec((rows, width), lambda s: (0, 0))


def _ffn_stages(x, pre_ref, post_ref, wg_ref, wu_ref, wd_ref, act_ref, rows):
    state = {}

    def hidden():
        state['h'] = _rms(x, pre_ref[...]).astype(BF16)

    def chunk(cols):
        gate = jnp.dot(state['h'], wg_ref[:, cols], preferred_element_type=F32)
        up = jnp.dot(state['h'], wu_ref[:, cols], preferred_element_type=F32)
        act_ref[rows, cols] = (gate * jax.nn.sigmoid(gate) * up).astype(BF16)

    def finish():
        f = jnp.dot(act_ref[rows, :], wd_ref[...], preferred_element_type=F32)
        return x + FFN_RES * _rms(f, post_ref[...])

    stages, start = [hidden], 0
    for width in FF_CHUNKS:
        stages.append(functools.partial(chunk, slice(start, start + width)))
        start += width
    return stages + [finish]


def _ffn_rows(x, pre_ref, post_ref, wg_ref, wu_ref, wd_ref, act_ref, rows):
    for stage in _ffn_stages(x, pre_ref, post_ref, wg_ref, wu_ref, wd_ref, act_ref, rows):
        y = stage()
    return y


_HBM_SPEC = pl.BlockSpec(memory_space=pl.ANY)
_FFN_WEIGHT_SPECS = [_const_spec((1, D_MODEL)), _const_spec((1, D_MODEL)), _HBM_SPEC, _HBM_SPEC, _HBM_SPEC]
_FFN_SCRATCH = [pltpu.VMEM((D_MODEL, D_FF), BF16), pltpu.VMEM((D_MODEL, D_FF), BF16),
                pltpu.VMEM((D_FF, D_MODEL), BF16),
                pltpu.VMEM((2, WEIGHT_CHUNK_ROWS, D_FF), F32),
                pltpu.VMEM((2, WEIGHT_CHUNK_ROWS, D_MODEL), F32),
                pltpu.SemaphoreType.DMA((2,)),
                pltpu.VMEM((ROW_TILE, D_FF), BF16)]


def _load_bf16(hbm_ref, dst_ref, stage_ref, sem_ref):
    chunk = stage_ref.shape[1]
    n_chunks = hbm_ref.shape[0] // chunk

    def copy(i):
        return pltpu.make_async_copy(hbm_ref.at[pl.ds(i * chunk, chunk), :], stage_ref.at[i % 2],
                                     sem_ref.at[i % 2])

    copy(0).start()
    for i in range(n_chunks):
        if i + 1 < n_chunks:
            copy(i + 1).start()
        copy(i).wait()
        dst_ref[pl.ds(i * chunk, chunk), :] = stage_ref[i % 2].astype(BF16)


def _load_ffn_weights(wg_hbm, wu_hbm, wd_hbm, wg_ref, wu_ref, wd_ref, wide_stage_ref, stage_ref, sem_ref):
    @pl.when(pl.program_id(0) == 0)
    def _():
        _load_bf16(wg_hbm, wg_ref, wide_stage_ref, sem_ref)
        _load_bf16(wu_hbm, wu_ref, wide_stage_ref, sem_ref)
        _load_bf16(wd_hbm, wd_ref, stage_ref, sem_ref)


def _ffn_kernel(xp_ref, xs_ref, pre_ref, post_ref, wg_hbm, wu_hbm, wd_hbm, o_ref,
                wg_ref, wu_ref, wd_ref, wide_stage_ref, stage_ref, sem_ref, act_ref):
    _load_ffn_weights(wg_hbm, wu_hbm, wd_hbm, wg_ref, wu_ref, wd_ref, wide_stage_ref, stage_ref, sem_ref)
    for rows in FFN_ROW_GROUPS:
        x = jnp.where(_is_sample_step(), xs_ref[rows, :], xp_ref[rows, :])
        o_ref[rows, :] = _ffn_rows(x, pre_ref, post_ref, wg_ref, wu_ref, wd_ref, act_ref, rows)


def _ffn(x_prompt, x_sample, pre_g, post_g, wg, wu, wd, n_prompt_tiles):
    n_tiles = n_prompt_tiles + 1
    return pl.pallas_call(
        _ffn_kernel,
        grid=(n_tiles,),
        in_specs=[_prompt_tile_spec(D_MODEL), _sample_tile_spec(D_MODEL)] + _FFN_WEIGHT_SPECS,
        out_specs=_merged_tile_spec(D_MODEL, n_prompt_tiles),
        out_shape=jax.ShapeDtypeStruct((n_tiles * ROW_TILE, D_MODEL), F32),
        scratch_shapes=_FFN_SCRATCH,
        compiler_params=_params(1),
        name="ffn",
    )(x_prompt, x_sample, pre_g, post_g, wg, wu, wd)


def _inproj_kernel(x_ref, g_ref, wqug_ref, wkvt_ref, lng_ref, lnb_ref,
                   q_ref, u_ref, gn_ref, kt3_ref, vt3_ref, ktp_ref, vtp_ref, kts_ref, vts_ref):
    h = _rms(x_ref[...], g_ref[...]).astype(BF16)
    z = jnp.dot(h, wqug_ref[...], preferred_element_type=F32)
    kvt = lax.dot_general(wkvt_ref[...], h, (((1,), (1,)), ((), ())), preferred_element_type=F32)
    w = SB_WIDTH
    q_ref[...] = (z[:, 0:w] * (HEAD_DIM ** -0.5 * LOG2_E)).astype(BF16)
    u_ref[...] = _gelu(z[:, w:w + SGU_WIDTH])
    ge = _gelu(z[:, w + SGU_WIDTH:])
    xc = ge - jnp.mean(ge, axis=-1, keepdims=True)
    y = xc * lax.rsqrt(jnp.mean(xc * xc, axis=-1, keepdims=True) + EPS)
    gn_ref[...] = y * lng_ref[...] + lnb_ref[...]
    for t_ref, t3_ref, rows in ((ktp_ref, kt3_ref, slice(0, w)), (vtp_ref, vt3_ref, slice(w, 2 * w))):
        t = kvt[rows]
        t_ref[...] = t
        for c in range(ROW_TILE // KEY_BLOCK):
            t3_ref[c] = t[:, c * KEY_BLOCK:(c + 1) * KEY_BLOCK].astype(BF16)

    @pl.when(_is_sample_step())
    def _():
        kts_ref[...] = ktp_ref[...]
        vts_ref[...] = vtp_ref[...]


def _inproj(x, g, w_qug, w_kvt, ln_g, ln_b, n_prompt_tiles, prompt_tiles_per_row):
    n = x.shape[0]
    half = _merged_tile_spec(SB_WIDTH, n_prompt_tiles)
    blocks_per_tile = ROW_TILE // KEY_BLOCK
    key_blocks = pl.BlockSpec((blocks_per_tile, SB_WIDTH, KEY_BLOCK),
                              lambda s: ((s + n_prompt_tiles) % (n_prompt_tiles + 1), 0, 0))

    def prompt_t_index(s):
        tile = jnp.maximum(s - 1, 0)
        return tile // prompt_tiles_per_row, tile % prompt_tiles_per_row

    prompt_t = pl.BlockSpec((SB_WIDTH, ROW_TILE), prompt_t_index)
    prompt_t_sds = jax.ShapeDtypeStruct(
        (n_prompt_tiles // prompt_tiles_per_row * SB_WIDTH, prompt_tiles_per_row * ROW_TILE), F32)
    sample_t_sds = jax.ShapeDtypeStruct((SB_WIDTH, ROW_TILE), F32)
    f32_half = jax.ShapeDtypeStruct((n, SB_WIDTH), F32)
    key_blocks_sds = jax.ShapeDtypeStruct((n // KEY_BLOCK, SB_WIDTH, KEY_BLOCK), BF16)
    return pl.pallas_call(
        _inproj_kernel,
        grid=(n // ROW_TILE,),
        in_specs=[_merged_tile_spec(D_MODEL, n_prompt_tiles), _const_spec((1, D_MODEL)),
                  _const_spec(w_qug.shape), _const_spec(w_kvt.shape), _const_spec((1, SGU_WIDTH)),
                  _const_spec((1, SGU_WIDTH))],
        out_specs=[half] * 3 + [key_blocks] * 2 + [prompt_t] * 2 + [_sample_tile_spec(ROW_TILE, SB_WIDTH)] * 2,
        out_shape=[jax.ShapeDtypeStruct((n, SB_WIDTH), BF16), f32_half, f32_half,
                   key_blocks_sds, key_blocks_sds, prompt_t_sds, prompt_t_sds,
                   sample_t_sds, sample_t_sds],
        compiler_params=_params(1),
        name="inproj",
    )(x, g, w_qug, w_kvt, ln_g, ln_b)


def _pair_rows(x, lane_lo):
    zero = jnp.zeros_like(x)
    return jnp.concatenate([jnp.where(lane_lo, x, zero), jnp.where(lane_lo, zero, x)], axis=0)


def _sb_blocks(qqs, kts, vts, carries, suffix_ones, lane_lo, masks, tops=None):
    n_blocks = len(masks)
    tops = tops or [None] * n_blocks
    t = qqs[0].shape[0] // 2
    nt_dims = (((1,), (1,)), ((), ()))

    def head_rows(x, top):
        return x if top is None else jnp.concatenate([x[:top], x[t:t + top]], axis=0)

    def weighted_values(a, vt):
        both = lax.dot_general(a, vt, nt_dims, preferred_element_type=F32)
        rows = a.shape[0] // 2
        return jnp.where(lane_lo, both[:rows], both[rows:])

    zs = []
    for qq, kts_p in zip(qqs, kts):
        zs_p, b = [], 0
        while b < n_blocks:
            if b + 1 < n_blocks and tops[b] is None and tops[b + 1] is None:
                z2 = jnp.dot(qq, jnp.concatenate([kts_p[b], kts_p[b + 1]], axis=1),
                             preferred_element_type=F32)
                zs_p += [z2[:, :KEY_BLOCK], z2[:, KEY_BLOCK:]]
                b += 2
            else:
                zs_p.append(jnp.dot(head_rows(qq, tops[b]), kts_p[b], preferred_element_type=F32))
                b += 1
        zs.append(zs_p)
    useds, splits = [], []
    for zs_p in zs:
        useds.append([])
        splits.append([])
        for i, mask in enumerate(masks):
            z = zs_p[i]
            if mask is not None:
                z = zs_p[i] = jnp.where(mask, z, HIDDEN_LOGIT)
            used = jnp.maximum(z, 0.0) + jnp.log(1.0 + jnp.exp2(-jnp.abs(z))) * LOG2_E
            hi = used.astype(BF16)
            lo = (used - hi.astype(F32)).astype(BF16)
            useds[-1].append(used)
            splits[-1].append(jnp.concatenate([hi, lo], axis=1))
    sums = [[jnp.dot(s, suffix_ones, preferred_element_type=F32) for s in splits_p]
            for splits_p in splits]
    outs, new_carries = [], []
    for zs_p, useds_p, sums_p, vts_p, carry in zip(zs, useds, sums, vts, carries):
        weights, values_t, top_outs = [], [], []
        for z, used, s, vt, top in zip(zs_p, useds_p, sums_p, vts_p, tops):
            a = jnp.exp2(z - used - (s[:, :KEY_BLOCK] + head_rows(carry, top))).astype(BF16)
            row_sum = s[:, KEY_BLOCK:]
            if top is None:
                weights.append(a)
                values_t.append(vt)
                carry = carry + row_sum
            else:
                top_outs.append(weighted_values(a, vt))
                carry = jnp.concatenate([carry[:top] + row_sum[:top], carry[top:t],
                                         carry[t:t + top] + row_sum[top:], carry[t + top:]], axis=0)
        out = weighted_values(jnp.concatenate(weights, axis=1), jnp.concatenate(values_t, axis=1))
        for o_top in top_outs:
            top = o_top.shape[0]
            out = jnp.concatenate([out[:top] + o_top, out[top:]], axis=0)
        outs.append(out)
        new_carries.append(carry)
    return outs, new_carries


def _suffix_ones():
    r = lax.broadcasted_iota(jnp.int32, (2 * KEY_BLOCK, 2 * KEY_BLOCK), 0) % KEY_BLOCK
    c = lax.broadcasted_iota(jnp.int32, (2 * KEY_BLOCK, 2 * KEY_BLOCK), 1)
    return jnp.where((c >= KEY_BLOCK) | (r > c), 1.0, 0.0).astype(BF16)


_PAIR_SLICES = [slice(p * LANES, (p + 1) * LANES) for p in range(N_PAIRS)]


class _QueryBlock:
    def __init__(self, q_ref, o_ref, carry_ref, n_past, diag_tiles, joint_tiles, walk_tiles):
        self.q_ref, self.o_ref, self.carry_ref, self.n_past = q_ref, o_ref, carry_ref, n_past
        self.diag_tiles, self.joint_tiles, self.walk_tiles = diag_tiles, joint_tiles, walk_tiles


class _StickBreaking:
    def __init__(self, tq, diag_key_offset):
        self.tq = tq
        self.lane_lo = lax.broadcasted_iota(jnp.int32, (1, LANES), 1) < HEAD_DIM
        self.suffix_ones = _suffix_ones()
        self.row = lax.broadcasted_iota(jnp.int32, (2 * tq, KEY_BLOCK), 0) % tq
        col = lax.broadcasted_iota(jnp.int32, (2 * tq, KEY_BLOCK), 1) - diag_key_offset
        self.causal = (col >= 0) & (col < self.row)

    def _qq_tiles(self, blk):
        return [_pair_rows(blk.q_ref[:, cols], self.lane_lo) for cols in _PAIR_SLICES]

    def _store(self, blk, outs, carries, first):
        for p, cols in enumerate(_PAIR_SLICES):
            if first:
                blk.o_ref[:, cols] = outs[p]
            else:
                blk.o_ref[:, cols] += outs[p]
            blk.carry_ref[p] = carries[p]

    def first_step(self, blks, n_joint):
        qqs, kts, vts = [], [], []
        for blk in blks:
            kd, vd = blk.diag_tiles()
            kts_b, vts_b = [[t] for t in kd], [[t] for t in vd]
            for b in range(n_joint):
                kj, vj = blk.joint_tiles(blk.n_past - 1 - b)
                for p in range(N_PAIRS):
                    kts_b[p].append(kj[p])
                    vts_b[p].append(vj[p])
            qqs += self._qq_tiles(blk)
            kts += kts_b
            vts += vts_b
        tops = [None] * n_joint + ([TOP_ROWS] if n_joint > 1 else [None])
        outs, carries = _sb_blocks(qqs, kts, vts, [jnp.zeros((2 * self.tq, LANES), F32)] * len(qqs),
                                   self.suffix_ones, self.lane_lo, [self.causal] + [None] * n_joint, tops)
        for i, blk in enumerate(blks):
            pairs = slice(i * N_PAIRS, (i + 1) * N_PAIRS)
            self._store(blk, outs[pairs], carries[pairs], True)

    def _single_block(self, blk, tiles, mask):
        kts, vts = tiles
        outs, carries = _sb_blocks(self._qq_tiles(blk), [[t] for t in kts], [[t] for t in vts],
                                   [blk.carry_ref[p] for p in range(N_PAIRS)],
                                   self.suffix_ones, self.lane_lo, [mask])
        self._store(blk, outs, carries, False)

    def _least_used(self, blk, rows=None):
        tq = self.tq

        def pick(x):
            return x if rows is None else jnp.concatenate([x[rows], x[tq + rows.start:tq + rows.stop]], axis=0)
        return jnp.min(functools.reduce(jnp.minimum, [pick(blk.carry_ref[p]) for p in range(N_PAIRS)]))

    def finish_partial_block(self, blk):
        @pl.when(self._least_used(blk, slice(TOP_ROWS, self.tq)) < USED_STICK_CUTOFF)
        def _():
            self._single_block(blk, blk.joint_tiles(blk.n_past - JOINT_PAST_BLOCKS), self.row >= TOP_ROWS)

    def walk(self, blk, next_block):
        def body(state):
            j, _ = state
            self._single_block(blk, blk.walk_tiles(j), None)
            return j - 1, self._least_used(blk)

        lax.while_loop(lambda s: (s[0] >= 0) & (s[1] < USED_STICK_CUTOFF), body,
                       (next_block, self._least_used(blk)))


def _sb_prompt_kernel(q_ref, kt_ref, vt_ref, o_ref, carry_ref):
    def tiles(j):
        return ([kt_ref[j, rows, :] for rows in _PAIR_SLICES],
                [vt_ref[j, rows, :] for rows in _PAIR_SLICES])

    step = pl.program_id(1)
    sb = _StickBreaking(KEY_BLOCK, 0)
    blks = []
    for r in range(Q_BLOCKS_PER_STEP):
        i = step * Q_BLOCKS_PER_STEP + r
        rows = pl.ds(r * KEY_BLOCK, KEY_BLOCK)
        blks.append(_QueryBlock(q_ref.at[rows, :], o_ref.at[rows, :], carry_ref.at[r], i,
                                functools.partial(tiles, i), tiles, tiles))

    assert Q_BLOCKS_PER_STEP >= JOINT_PAST_BLOCKS > 1

    @pl.when(step > 0)
    def _():
        sb.first_step(blks, JOINT_PAST_BLOCKS)
        for blk in blks:
            sb.finish_partial_block(blk)

    @pl.when(step == 0)
    def _():
        for r, blk in enumerate(blks):
            sb.first_step([blk], min(r, JOINT_PAST_BLOCKS))
            if r >= JOINT_PAST_BLOCKS:
                sb.finish_partial_block(blk)

    for r, blk in enumerate(blks):
        sb.walk(blk, jnp.where(step > 0, blk.n_past - JOINT_PAST_BLOCKS,
                               r - min(r, JOINT_PAST_BLOCKS)) - 1)


def _sb_prompt(q, kt3, vt3, batch, seq):
    nq = seq // (KEY_BLOCK * Q_BLOCKS_PER_STEP)
    blk = pl.BlockSpec((KEY_BLOCK * Q_BLOCKS_PER_STEP, SB_WIDTH), lambda b, i: (b * nq + i, 0))
    keys = pl.BlockSpec((seq // KEY_BLOCK, SB_WIDTH, KEY_BLOCK), lambda b, i: (b, 0, 0))
    return pl.pallas_call(
        _sb_prompt_kernel,
        grid=(batch, nq),
        in_specs=[blk, keys, keys],
        out_specs=blk,
        out_shape=jax.ShapeDtypeStruct((batch * seq, SB_WIDTH), F32),
        scratch_shapes=[pltpu.VMEM((Q_BLOCKS_PER_STEP, N_PAIRS, 2 * KEY_BLOCK, LANES), F32)],
        compiler_params=_params(2),
        name="sb_prompt",
    )(q, kt3, vt3)


def _sb_sample_kernel(q_ref, kd_ref, vd_ref, kr_ref, vr_ref, kc_hbm, vc_hbm, o_ref,
                      carry_ref, kbuf_ref, vbuf_ref, sem_ref, *, past_len):
    tq = q_ref.shape[0]
    n_past = past_len // KEY_BLOCK
    n_recent = kr_ref.shape[2] // KEY_BLOCK
    b = pl.program_id(0)

    def diag_tiles():
        return ([kd_ref[0, rows, :] for rows in _PAIR_SLICES],
                [vd_ref[0, rows, :] for rows in _PAIR_SLICES])

    def recent_tiles(j):
        first = (j - (n_past - n_recent)) * KEY_BLOCK
        keys = slice(first, first + KEY_BLOCK)
        return ([kr_ref[0, rows, keys].astype(BF16) for rows in _PAIR_SLICES],
                [vr_ref[0, rows, keys].astype(BF16) for rows in _PAIR_SLICES])

    def cache_tiles(j):
        keys = pl.ds(pl.multiple_of(j * KEY_BLOCK, KEY_BLOCK), KEY_BLOCK)
        copies = [pltpu.make_async_copy(hbm.at[b, :, keys], buf, sem_ref.at[s])
                  for s, (hbm, buf) in enumerate(((kc_hbm, kbuf_ref), (vc_hbm, vbuf_ref)))]
        for c in copies:
            c.start()
        for c in copies:
            c.wait()
        return ([kbuf_ref[rows, :].astype(BF16) for rows in _PAIR_SLICES],
                [vbuf_ref[rows, :].astype(BF16) for rows in _PAIR_SLICES])

    assert n_recent == JOINT_PAST_BLOCKS > 1
    sb = _StickBreaking(tq, (b % (KEY_BLOCK // tq)) * tq)
    blk = _QueryBlock(q_ref, o_ref, carry_ref, n_past, diag_tiles, recent_tiles, cache_tiles)
    sb.first_step([blk], JOINT_PAST_BLOCKS)
    sb.finish_partial_block(blk)
    sb.walk(blk, n_past - JOINT_PAST_BLOCKS - 1)


def _sb_sample(q, kt3, vt3, cache_kt, cache_vt, t_new, first_row):
    batch, _, past_len = cache_kt.shape
    n_recent = JOINT_PAST_BLOCKS * KEY_BLOCK
    seqs_per_block = KEY_BLOCK // t_new
    new = pl.BlockSpec((1, SB_WIDTH, KEY_BLOCK),
                       lambda b: (first_row // KEY_BLOCK + b // seqs_per_block, 0, 0))
    recent = pl.BlockSpec((1, SB_WIDTH, n_recent), lambda b: (b, 0, past_len // n_recent - 1))
    hbm = pl.BlockSpec(memory_space=pl.ANY)
    return pl.pallas_call(
        functools.partial(_sb_sample_kernel, past_len=past_len),
        grid=(batch,),
        in_specs=[pl.BlockSpec((t_new, SB_WIDTH), lambda b: (first_row // t_new + b, 0)),
                  new, new, recent, recent, hbm, hbm],
        out_specs=pl.BlockSpec((t_new, SB_WIDTH), lambda b: (b, 0)),
        out_shape=jax.ShapeDtypeStruct((batch * t_new, SB_WIDTH), F32),
        scratch_shapes=[pltpu.VMEM((N_PAIRS, 2 * t_new, LANES), F32),
                        pltpu.VMEM((SB_WIDTH, KEY_BLOCK), F32),
                        pltpu.VMEM((SB_WIDTH, KEY_BLOCK), F32),
                        pltpu.SemaphoreType.DMA((2,))],
        compiler_params=_params(1),
        name="sb_sample",
    )(q, kt3, vt3, cache_kt, cache_vt, cache_kt, cache_vt)


def _mix_rows(x_ref, op_ref, os_ref, u_ref, gn_ref, ws_ref, bias_ref, gsb_ref, gsgu_ref, wout_ref,
              post_ref, sgu_ref, rows):
    lane_lo = lax.broadcasted_iota(jnp.int32, (1, LANES), 1) < HEAD_DIM
    for p in range(N_PAIRS):
        cols = slice(p * LANES, (p + 1) * LANES)
        for first in range(rows.start, rows.stop, SGU_CHUNK):
            chunk = slice(first, first + SGU_CHUNK)
            gg = _pair_rows(gn_ref[chunk, cols].astype(BF16), lane_lo)
            mixed = jnp.dot(ws_ref[0, p], gg, preferred_element_type=F32) + bias_ref[0, :, cols]
            sgu_ref[chunk, cols] = u_ref[chunk, cols] * mixed
    o_sb = jnp.where(_is_sample_step(), os_ref[rows, :], op_ref[rows, :])
    merged = jnp.concatenate([_rms(o_sb, gsb_ref[...]).astype(BF16),
                              _rms(sgu_ref[rows, :], gsgu_ref[...]).astype(BF16)], axis=1)
    y = jnp.dot(merged, wout_ref[...], preferred_element_type=F32)
    return x_ref[rows, :] + _rms(y, post_ref[...])


def _mix_ffn_kernel(x_ref, op_ref, os_ref, u_ref, gn_ref, ws_ref, bias_ref, gsb_ref, gsgu_ref,
                    wout_ref, mix_post_ref, pre_ref, post_ref, wg_hbm, wu_hbm, wd_hbm,
                    yp_ref, ys_ref, sgu_ref,
                    wg_ref, wu_ref, wd_ref, wide_stage_ref, stage_ref, sem_ref, act_ref):
    _load_ffn_weights(wg_hbm, wu_hbm, wd_hbm, wg_ref, wu_ref, wd_ref, wide_stage_ref, stage_ref, sem_ref)

    def mix(rows):
        return _mix_rows(x_ref, op_ref, os_ref, u_ref, gn_ref, ws_ref, bias_ref, gsb_ref, gsgu_ref,
                         wout_ref, mix_post_ref, sgu_ref, rows)

    groups = FFN_ROW_GROUPS
    x = mix(groups[0])
    for g, rows in enumerate(groups):
        stages = _ffn_stages(x, pre_ref, post_ref, wg_ref, wu_ref, wd_ref, act_ref, rows)
        stages[0]()
        stages[1]()
        if g + 1 < len(groups):
            x = mix(groups[g + 1])
        for stage in stages[2:]:
            y = stage()
        yp_ref[rows, :] = y

    @pl.when(_is_sample_step())
    def _():
        ys_ref[...] = yp_ref[...]


def _mix_ffn(x, o_prompt, o_sample, u, gn, ws_pairs, bias, g_sb, g_sgu, w_out, mix_post_g,
             pre_g, post_g, wg, wu, wd, n_prompt_tiles):
    row = _merged_tile_spec(D_MODEL, n_prompt_tiles)
    half = _merged_tile_spec(SB_WIDTH, n_prompt_tiles)
    which = lambda s: jnp.where(s == 0, 1, 0)
    sds = lambda rows: jax.ShapeDtypeStruct((rows, D_MODEL), F32)
    return pl.pallas_call(
        _mix_ffn_kernel,
        grid=(n_prompt_tiles + 1,),
        in_specs=[row, _prompt_tile_spec(SB_WIDTH), _sample_tile_spec(SB_WIDTH), half, half,
                  pl.BlockSpec((1,) + ws_pairs.shape[1:], lambda s: (which(s), 0, 0, 0)),
                  pl.BlockSpec((1,) + bias.shape[1:], lambda s: (which(s), 0, 0)),
                  _const_spec((1, SB_WIDTH)), _const_spec((1, SGU_WIDTH)),
                  _const_spec((D_MODEL, D_MODEL)), _const_spec((1, D_MODEL))] + _FFN_WEIGHT_SPECS,
        out_specs=[_prompt_tile_spec(D_MODEL), _sample_tile_spec(D_MODEL)],
        out_shape=[sds(n_prompt_tiles * ROW_TILE), sds(ROW_TILE)],
        scratch_shapes=[pltpu.VMEM((ROW_TILE, SGU_WIDTH), F32)] + _FFN_SCRATCH,
        compiler_params=_params(1),
        name="mix_ffn",
    )(x, o_prompt, o_sample, u, gn, ws_pairs, bias, g_sb, g_sgu, w_out, mix_post_g,
      pre_g, post_g, wg, wu, wd)


def _sgu_operands(w_s, b_s, t_new):
    i = jnp.arange(SGU_CHUNK)
    mask = (i[None, :] // SGU_CAUSAL_CHUNK) <= (i[:, None] // SGU_CAUSAL_CHUNK)
    w_prompt = w_s * mask[None].astype(w_s.dtype)
    reps = SGU_CHUNK // t_new
    corner = w_prompt[:, :t_new, :t_new]
    eye = jnp.eye(reps, dtype=w_s.dtype)
    w_sample = jnp.einsum('ab,gij->gaibj', eye, corner).reshape(w_s.shape)
    groups = w_s.shape[0]

    def pairs(w):
        w = w.reshape(groups // 2, 2, SGU_CHUNK, SGU_CHUNK).transpose(0, 2, 1, 3)
        return w.reshape(groups // 2, SGU_CHUNK, 2 * SGU_CHUNK)

    def bias(b):
        return jnp.repeat(b.T, SGU_WIDTH // groups, axis=1)

    ws = jnp.stack([pairs(w_prompt), pairs(w_sample)]).astype(BF16)
    return ws, jnp.stack([bias(b_s), bias(jnp.tile(b_s[:, :t_new], (1, reps)))])


def kernel(x_prompt, x_sample, cache_k_sb, cache_v_sb, ffn1_pre_g, ffn1_post_g, ffn1_w_gate, ffn1_w_up, ffn1_w_down, mix_pre_g, mix_post_g, w_in, sgu_ln_g, sgu_ln_b, sgu_w_s, sgu_b_s, g_out_sb, g_out_sgu, w_out, ffn2_pre_g, ffn2_post_g, ffn2_w_gate, ffn2_w_up, ffn2_w_down):
    depth, batch, seq = w_in.shape[0], x_prompt.shape[0], x_prompt.shape[1]
    dec_batch, t_new, past_len = x_sample.shape[0], x_sample.shape[1], cache_k_sb.shape[2]
    assert depth == 1 and dec_batch * t_new == ROW_TILE and seq % ROW_TILE == 0
    assert SGU_CHUNK % t_new == 0 and KEY_BLOCK % t_new == 0
    assert past_len % (JOINT_PAST_BLOCKS * KEY_BLOCK) == 0
    n_prompt_tiles = batch * seq // ROW_TILE

    mat = lambda w: w[0].astype(BF16)
    vec = lambda g: g[0][None, :]
    w_qug = jnp.concatenate([w_in[0][:, :SB_WIDTH], w_in[0][:, 3 * SB_WIDTH:]], axis=1).astype(BF16)
    w_kvt = w_in[0][:, SB_WIDTH:3 * SB_WIDTH].T.astype(BF16)

    x = _ffn(x_prompt.reshape(batch * seq, D_MODEL), x_sample.reshape(ROW_TILE, D_MODEL),
             vec(ffn1_pre_g), vec(ffn1_post_g), ffn1_w_gate[0], ffn1_w_up[0], ffn1_w_down[0],
             n_prompt_tiles)
    q, u, gn, kt3, vt3, kt_p, vt_p, kt_s, vt_s = _inproj(
        x, vec(mix_pre_g), w_qug, w_kvt, vec(sgu_ln_g), vec(sgu_ln_b), n_prompt_tiles, seq // ROW_TILE)

    o_prompt = _sb_prompt(q, kt3, vt3, batch, seq)
    cache_t = lambda c: c[0].transpose(0, 2, 3, 1).reshape(dec_batch, SB_WIDTH, past_len)
    o_sample = _sb_sample(q, kt3, vt3, cache_t(cache_k_sb), cache_t(cache_v_sb), t_new, batch * seq)

    ws_pairs, bias = _sgu_operands(sgu_w_s[0], sgu_b_s[0], t_new)
    y_p, y_s = _mix_ffn(x, o_prompt, o_sample, u, gn, ws_pairs, bias, vec(g_out_sb), vec(g_out_sgu),
                        mat(w_out), vec(mix_post_g), vec(ffn2_pre_g), vec(ffn2_post_g),
                        ffn2_w_gate[0], ffn2_w_up[0], ffn2_w_down[0], n_prompt_tiles)

    def prompt_heads(t):
        return t.reshape(batch, N_HEADS, HEAD_DIM, seq).transpose(0, 3, 1, 2)[None]

    def sample_heads(t):
        return t.reshape(N_HEADS, HEAD_DIM, dec_batch, t_new).transpose(2, 3, 0, 1)[None]

    g_s = gn[batch * seq:].reshape(1, dec_batch, t_new, N_HEADS, HEAD_DIM)
    return (y_p.reshape(batch, seq, D_MODEL), y_s.reshape(dec_batch, t_new, D_MODEL),
            prompt_heads(kt_p), prompt_heads(vt_p), sample_heads(kt_s), sample_heads(vt_s), g_s)
```

```python
import functools

import jax
import jax.numpy as jnp
from jax import lax
from jax.experimental import pallas as pl
from jax.experimental.pallas import tpu as pltpu

D_MODEL = 1024
D_FF = 2816
SB_WIDTH = 512
SGU_WIDTH = 512
HEAD_DIM = 64
N_HEADS = SB_WIDTH // HEAD_DIM
SGU_CHUNK = 128
SGU_CAUSAL_CHUNK = 64
FFN_RES = 0.5
EPS = 1e-6

LANES = 128
N_PAIRS = SB_WIDTH // LANES
KEY_BLOCK = 128
JOINT_PAST_BLOCKS = 2
TOP_ROWS = 32
Q_BLOCKS_PER_STEP = 2
FF_CHUNKS = (1536, 1280)
ROW_TILE = 512
WEIGHT_CHUNK_ROWS = 128
FFN_ROW_GROUPS = (slice(0, 256), slice(256, 512))
VMEM_LIMIT = 56 * 1024 * 1024

LOG2_E = 1.4426950408889634
USED_STICK_CUTOFF = 105.0 * LOG2_E
HIDDEN_LOGIT = -1e30

F32 = jnp.float32
BF16 = jnp.bfloat16


def _rms(x, g):
    return x * lax.rsqrt(jnp.mean(x * x, axis=-1, keepdims=True) + EPS) * g


def _gelu(x):
    return 0.5 * x * (1.0 + lax.erf(x * (0.5 ** 0.5)))


def _const_spec(shape):
    return pl.BlockSpec(shape, lambda *_: (0,) * len(shape), pipeline_mode=pl.Buffered(1))


def _params(n_axes):
    return pltpu.CompilerParams(dimension_semantics=("arbitrary",) * n_axes,
                                vmem_limit_bytes=VMEM_LIMIT)


def _is_sample_step():
    return pl.program_id(0) == 0


def _merged_tile_spec(width, n_prompt_tiles):
    return pl.BlockSpec((ROW_TILE, width), lambda s: ((s + n_prompt_tiles) % (n_prompt_tiles + 1), 0))


def _prompt_tile_spec(width, rows=ROW_TILE):
    return pl.BlockSpec((rows, width), lambda s: (jnp.maximum(s - 1, 0), 0))


def _sample_tile_spec(width, rows=ROW_TILE):
    return pl.BlockSpec((rows, width), lambda s: (0, 0))


def _ffn_stages(x, pre_ref, post_ref, wg_ref, wu_ref, wd_ref, act_ref, rows):
    state = {}

    def hidden():
        state['h'] = _rms(x, pre_ref[...]).astype(BF16)

    def chunk(cols):
        gate = jnp.dot(state['h'], wg_ref[:, cols], preferred_element_type=F32)
        up = jnp.dot(state['h'], wu_ref[:, cols], preferred_element_type=F32)
        act_ref[rows, cols] = (gate * jax.nn.sigmoid(gate) * up).astype(BF16)

    def finish():
        f = jnp.dot(act_ref[rows, :], wd_ref[...], preferred_element_type=F32)
        return x + FFN_RES * _rms(f, post_ref[...])

    stages, start = [hidden], 0
    for width in FF_CHUNKS:
        stages.append(functools.partial(chunk, slice(start, start + width)))
        start += width
    return stages + [finish]


def _ffn_rows(x, pre_ref, post_ref, wg_ref, wu_ref, wd_ref, act_ref, rows):
    for stage in _ffn_stages(x, pre_ref, post_ref, wg_ref, wu_ref, wd_ref, act_ref, rows):
        y = stage()
    return y


_HBM_SPEC = pl.BlockSpec(memory_space=pl.ANY)
_FFN_WEIGHT_SPECS = [_const_spec((1, D_MODEL)), _const_spec((1, D_MODEL)), _HBM_SPEC, _HBM_SPEC, _HBM_SPEC]
_FFN_SCRATCH = [pltpu.VMEM((D_MODEL, D_FF), BF16), pltpu.VMEM((D_MODEL, D_FF), BF16),
                pltpu.VMEM((D_FF, D_MODEL), BF16),
                pltpu.VMEM((2, WEIGHT_CHUNK_ROWS, D_FF), F32),
                pltpu.VMEM((2, WEIGHT_CHUNK_ROWS, D_FF), F32),
                pltpu.VMEM((2, WEIGHT_CHUNK_ROWS, D_MODEL), F32),
                pltpu.SemaphoreType.DMA((3, 2)),
                pltpu.VMEM((ROW_TILE, D_FF), BF16)]


def _load_bf16(streams):
    def n_chunks(stream):
        return stream[0].shape[0] // stream[2].shape[1]

    def copy(stream, i):
        hbm_ref, _, stage_ref, sem_ref = stream
        chunk = stage_ref.shape[1]
        return pltpu.make_async_copy(hbm_ref.at[pl.ds(i * chunk, chunk), :], stage_ref.at[i % 2],
                                     sem_ref.at[i % 2])

    for stream in streams:
        copy(stream, 0).start()
    order = sorted(((i + 1) / n_chunks(stream), k, i)
                   for k, stream in enumerate(streams) for i in range(n_chunks(stream)))
    for _, k, i in order:
        stream = streams[k]
        if i + 1 < n_chunks(stream):
            copy(stream, i + 1).start()
        copy(stream, i).wait()
        chunk = stream[2].shape[1]
        stream[1][pl.ds(i * chunk, chunk), :] = stream[2][i % 2].astype(BF16)


def _load_ffn_weights(wg_hbm, wu_hbm, wd_hbm, wg_ref, wu_ref, wd_ref, stage_refs, sem_ref):
    @pl.when(pl.program_id(0) == 0)
    def _():
        _load_bf16([(hbm, dst, stage, sem_ref.at[k]) for k, (hbm, dst, stage) in enumerate(
            zip((wg_hbm, wu_hbm, wd_hbm), (wg_ref, wu_ref, wd_ref), stage_refs))])


def _ffn_kernel(xp_ref, xs_ref, pre_ref, post_ref, wg_hbm, wu_hbm, wd_hbm, o_ref,
                wg_ref, wu_ref, wd_ref, stage_g_ref, stage_u_ref, stage_d_ref, sem_ref, act_ref):
    _load_ffn_weights(wg_hbm, wu_hbm, wd_hbm, wg_ref, wu_ref, wd_ref,
                      (stage_g_ref, stage_u_ref, stage_d_ref), sem_ref)
    for rows in FFN_ROW_GROUPS:
        x = jnp.where(_is_sample_step(), xs_ref[rows, :], xp_ref[rows, :])
        o_ref[rows, :] = _ffn_rows(x, pre_ref, post_ref, wg_ref, wu_ref, wd_ref, act_ref, rows)


def _ffn(x_prompt, x_sample, pre_g, post_g, wg, wu, wd, n_prompt_tiles):
    n_tiles = n_prompt_tiles + 1
    return pl.pallas_call(
        _ffn_kernel,
        grid=(n_tiles,),
        in_specs=[_prompt_tile_spec(D_MODEL), _sample_tile_spec(D_MODEL)] + _FFN_WEIGHT_SPECS,
        out_specs=_merged_tile_spec(D_MODEL, n_prompt_tiles),
        out_shape=jax.ShapeDtypeStruct((n_tiles * ROW_TILE, D_MODEL), F32),
        scratch_shapes=_FFN_SCRATCH,
        compiler_params=_params(1),
        name="ffn",
    )(x_prompt, x_sample, pre_g, post_g, wg, wu, wd)


def _inproj_kernel(x_ref, g_ref, win_hbm, lng_ref, lnb_ref,
                   q_ref, u_ref, gn_ref, kt3_ref, vt3_ref, ktp_ref, vtp_ref, kts_ref, vts_ref,
                   wqug_ref, wkvt_ref, stage_ref, sem_ref):
    w = SB_WIDTH

    @pl.when(pl.program_id(0) == 0)
    def _():
        chunk = stage_ref.shape[1]
        n_chunks = win_hbm.shape[0] // chunk

        def copy(i):
            return pltpu.make_async_copy(win_hbm.at[pl.ds(i * chunk, chunk), :], stage_ref.at[i % 2],
                                         sem_ref.at[i % 2])

        copy(0).start()
        for i in range(n_chunks):
            if i + 1 < n_chunks:
                copy(i + 1).start()
            copy(i).wait()
            rows = pl.ds(i * chunk, chunk)
            part = stage_ref[i % 2]
            wqug_ref[rows, 0:w] = part[:, 0:w].astype(BF16)
            wqug_ref[rows, w:] = part[:, 3 * w:].astype(BF16)
            wkvt_ref[:, rows] = part[:, w:3 * w].T.astype(BF16)

    h = _rms(x_ref[...], g_ref[...]).astype(BF16)
    z = jnp.dot(h, wqug_ref[...], preferred_element_type=F32)
    kvt = lax.dot_general(wkvt_ref[...], h, (((1,), (1,)), ((), ())), preferred_element_type=F32)
    q_ref[...] = (z[:, 0:w] * (HEAD_DIM ** -0.5 * LOG2_E)).astype(BF16)
    u_ref[...] = _gelu(z[:, w:w + SGU_WIDTH])
    ge = _gelu(z[:, w + SGU_WIDTH:])
    xc = ge - jnp.mean(ge, axis=-1, keepdims=True)
    y = xc * lax.rsqrt(jnp.mean(xc * xc, axis=-1, keepdims=True) + EPS)
    gn_ref[...] = y * lng_ref[...] + lnb_ref[...]
    for t_ref, t3_ref, rows in ((ktp_ref, kt3_ref, slice(0, w)), (vtp_ref, vt3_ref, slice(w, 2 * w))):
        t = kvt[rows]
        t_ref[...] = t
        for c in range(ROW_TILE // KEY_BLOCK):
            t3_ref[c] = t[:, c * KEY_BLOCK:(c + 1) * KEY_BLOCK].astype(BF16)

    @pl.when(_is_sample_step())
    def _():
        kts_ref[...] = ktp_ref[...]
        vts_ref[...] = vtp_ref[...]


def _inproj(x, g, w_in, ln_g, ln_b, n_prompt_tiles, prompt_tiles_per_row):
    n = x.shape[0]
    half = _merged_tile_spec(SB_WIDTH, n_prompt_tiles)
    blocks_per_tile = ROW_TILE // KEY_BLOCK
    key_blocks = pl.BlockSpec((blocks_per_tile, SB_WIDTH, KEY_BLOCK),
                              lambda s: ((s + n_prompt_tiles) % (n_prompt_tiles + 1), 0, 0))

    def prompt_t_index(s):
        tile = jnp.maximum(s - 1, 0)
        return tile // prompt_tiles_per_row, tile % prompt_tiles_per_row

    prompt_t = pl.BlockSpec((SB_WIDTH, ROW_TILE), prompt_t_index)
    prompt_t_sds = jax.ShapeDtypeStruct(
        (n_prompt_tiles // prompt_tiles_per_row * SB_WIDTH, prompt_tiles_per_row * ROW_TILE), F32)
    sample_t_sds = jax.ShapeDtypeStruct((SB_WIDTH, ROW_TILE), F32)
    f32_half = jax.ShapeDtypeStruct((n, SB_WIDTH), F32)
    key_blocks_sds = jax.ShapeDtypeStruct((n // KEY_BLOCK, SB_WIDTH, KEY_BLOCK), BF16)
    return pl.pallas_call(
        _inproj_kernel,
        grid=(n // ROW_TILE,),
        in_specs=[_merged_tile_spec(D_MODEL, n_prompt_tiles), _const_spec((1, D_MODEL)),
                  _HBM_SPEC, _const_spec((1, SGU_WIDTH)), _const_spec((1, SGU_WIDTH))],
        out_specs=[half] * 3 + [key_blocks] * 2 + [prompt_t] * 2 + [_sample_tile_spec(ROW_TILE, SB_WIDTH)] * 2,
        out_shape=[jax.ShapeDtypeStruct((n, SB_WIDTH), BF16), f32_half, f32_half,
                   key_blocks_sds, key_blocks_sds, prompt_t_sds, prompt_t_sds,
                   sample_t_sds, sample_t_sds],
        scratch_shapes=[pltpu.VMEM((D_MODEL, SB_WIDTH + 2 * SGU_WIDTH), BF16),
                        pltpu.VMEM((2 * SB_WIDTH, D_MODEL), BF16),
                        pltpu.VMEM((2, WEIGHT_CHUNK_ROWS, w_in.shape[1]), F32),
                        pltpu.SemaphoreType.DMA((2,))],
        compiler_params=_params(1),
        name="inproj",
    )(x, g, w_in, ln_g, ln_b)


def _pair_rows(x, lane_lo):
    zero = jnp.zeros_like(x)
    return jnp.concatenate([jnp.where(lane_lo, x, zero), jnp.where(lane_lo, zero, x)], axis=0)


def _sb_blocks(qqs, kts, vts, carries, suffix_ones, lane_lo, masks, tops=None):
    n_blocks = len(masks)
    tops = tops or [None] * n_blocks
    t = qqs[0].shape[0] // 2
    nt_dims = (((1,), (1,)), ((), ()))

    def head_rows(x, top):
        return x if top is None else jnp.concatenate([x[:top], x[t:t + top]], axis=0)

    def weighted_values(a, vt):
        both = lax.dot_general(a, vt, nt_dims, preferred_element_type=F32)
        rows = a.shape[0] // 2
        return jnp.where(lane_lo, both[:rows], both[rows:])

    zs = []
    for qq, kts_p in zip(qqs, kts):
        zs_p, b = [], 0
        while b < n_blocks:
            if b + 1 < n_blocks and tops[b] is None and tops[b + 1] is None:
                z2 = jnp.dot(qq, jnp.concatenate([kts_p[b], kts_p[b + 1]], axis=1),
                             preferred_element_type=F32)
                zs_p += [z2[:, :KEY_BLOCK], z2[:, KEY_BLOCK:]]
                b += 2
            else:
                zs_p.append(jnp.dot(head_rows(qq, tops[b]), kts_p[b], preferred_element_type=F32))
                b += 1
        zs.append(zs_p)
    useds, splits = [], []
    for zs_p in zs:
        useds.append([])
        splits.append([])
        for i, mask in enumerate(masks):
            z = zs_p[i]
            if mask is not None:
                z = zs_p[i] = jnp.where(mask, z, HIDDEN_LOGIT)
            used = jnp.maximum(z, 0.0) + jnp.log(1.0 + jnp.exp2(-jnp.abs(z))) * LOG2_E
            hi = used.astype(BF16)
            lo = (used - hi.astype(F32)).astype(BF16)
            useds[-1].append(used)
            splits[-1].append(jnp.concatenate([hi, lo], axis=1))
    sums = [[jnp.dot(s, suffix_ones, preferred_element_type=F32) for s in splits_p]
            for splits_p in splits]
    outs, new_carries = [], []
    for zs_p, useds_p, sums_p, vts_p, carry in zip(zs, useds, sums, vts, carries):
        weights, values_t, top_outs = [], [], []
        for z, used, s, vt, top in zip(zs_p, useds_p, sums_p, vts_p, tops):
            a = jnp.exp2(z - used - (s[:, :KEY_BLOCK] + head_rows(carry, top))).astype(BF16)
            row_sum = s[:, KEY_BLOCK:]
            if top is None:
                weights.append(a)
                values_t.append(vt)
                carry = carry + row_sum
            else:
                top_outs.append(weighted_values(a, vt))
                carry = jnp.concatenate([carry[:top] + row_sum[:top], carry[top:t],
                                         carry[t:t + top] + row_sum[top:], carry[t + top:]], axis=0)
        out = weighted_values(jnp.concatenate(weights, axis=1), jnp.concatenate(values_t, axis=1))
        for o_top in top_outs:
            top = o_top.shape[0]
            out = jnp.concatenate([out[:top] + o_top, out[top:]], axis=0)
        outs.append(out)
        new_carries.append(carry)
    return outs, new_carries


def _suffix_ones():
    r = lax.broadcasted_iota(jnp.int32, (2 * KEY_BLOCK, 2 * KEY_BLOCK), 0) % KEY_BLOCK
    c = lax.broadcasted_iota(jnp.int32, (2 * KEY_BLOCK, 2 * KEY_BLOCK), 1)
    return jnp.where((c >= KEY_BLOCK) | (r > c), 1.0, 0.0).astype(BF16)


_PAIR_SLICES = [slice(p * LANES, (p + 1) * LANES) for p in range(N_PAIRS)]


class _QueryBlock:
    def __init__(self, q_ref, o_ref, carry_ref, n_past, diag_tiles, joint_tiles, walk_tiles):
        self.q_ref, self.o_ref, self.carry_ref, self.n_past = q_ref, o_ref, carry_ref, n_past
        self.diag_tiles, self.joint_tiles, self.walk_tiles = diag_tiles, joint_tiles, walk_tiles


class _StickBreaking:
    def __init__(self, tq, diag_key_offset):
        self.tq = tq
        self.lane_lo = lax.broadcasted_iota(jnp.int32, (1, LANES), 1) < HEAD_DIM
        self.suffix_ones = _suffix_ones()
        self.row = lax.broadcasted_iota(jnp.int32, (2 * tq, KEY_BLOCK), 0) % tq
        col = lax.broadcasted_iota(jnp.int32, (2 * tq, KEY_BLOCK), 1) - diag_key_offset
        self.causal = (col >= 0) & (col < self.row)

    def _qq_tiles(self, blk):
        return [_pair_rows(blk.q_ref[:, cols], self.lane_lo) for cols in _PAIR_SLICES]

    def _store(self, blk, outs, carries, first):
        for p, cols in enumerate(_PAIR_SLICES):
            if first:
                blk.o_ref[:, cols] = outs[p]
            else:
                blk.o_ref[:, cols] += outs[p]
            blk.carry_ref[p] = carries[p]

    def first_step(self, blks, n_joint):
        qqs, kts, vts = [], [], []
        for blk in blks:
            kd, vd = blk.diag_tiles()
            kts_b, vts_b = [[t] for t in kd], [[t] for t in vd]
            for b in range(n_joint):
                kj, vj = blk.joint_tiles(blk.n_past - 1 - b)
                for p in range(N_PAIRS):
                    kts_b[p].append(kj[p])
                    vts_b[p].append(vj[p])
            qqs += self._qq_tiles(blk)
            kts += kts_b
            vts += vts_b
        tops = [None] * n_joint + ([TOP_ROWS] if n_joint > 1 else [None])
        outs, carries = _sb_blocks(qqs, kts, vts, [jnp.zeros((2 * self.tq, LANES), F32)] * len(qqs),
                                   self.suffix_ones, self.lane_lo, [self.causal] + [None] * n_joint, tops)
        for i, blk in enumerate(blks):
            pairs = slice(i * N_PAIRS, (i + 1) * N_PAIRS)
            self._store(blk, outs[pairs], carries[pairs], True)

    def _single_block(self, blk, tiles, mask):
        kts, vts = tiles
        outs, carries = _sb_blocks(self._qq_tiles(blk), [[t] for t in kts], [[t] for t in vts],
                                   [blk.carry_ref[p] for p in range(N_PAIRS)],
                                   self.suffix_ones, self.lane_lo, [mask])
        self._store(blk, outs, carries, False)

    def _least_used(self, blk, rows=None):
        tq = self.tq

        def pick(x):
            return x if rows is None else jnp.concatenate([x[rows], x[tq + rows.start:tq + rows.stop]], axis=0)
        return jnp.min(functools.reduce(jnp.minimum, [pick(blk.carry_ref[p]) for p in range(N_PAIRS)]))

    def finish_partial_block(self, blk):
        @pl.when(self._least_used(blk, slice(TOP_ROWS, self.tq)) < USED_STICK_CUTOFF)
        def _():
            self._single_block(blk, blk.joint_tiles(blk.n_past - JOINT_PAST_BLOCKS), self.row >= TOP_ROWS)

    def walk(self, blk, next_block):
        def body(state):
            j, _ = state
            self._single_block(blk, blk.walk_tiles(j), None)
            return j - 1, self._least_used(blk)

        lax.while_loop(lambda s: (s[0] >= 0) & (s[1] < USED_STICK_CUTOFF), body,
                       (next_block, self._least_used(blk)))


def _sb_prompt_kernel(q_ref, kt_ref, vt_ref, o_ref, carry_ref):
    def tiles(j):
        return ([kt_ref[j, rows, :] for rows in _PAIR_SLICES],
                [vt_ref[j, rows, :] for rows in _PAIR_SLICES])

    step = pl.program_id(1)
    sb = _StickBreaking(KEY_BLOCK, 0)
    blks = []
    for r in range(Q_BLOCKS_PER_STEP):
        i = step * Q_BLOCKS_PER_STEP + r
        rows = pl.ds(r * KEY_BLOCK, KEY_BLOCK)
        blks.append(_QueryBlock(q_ref.at[rows, :], o_ref.at[rows, :], carry_ref.at[r], i,
                                functools.partial(tiles, i), tiles, tiles))

    assert Q_BLOCKS_PER_STEP >= JOINT_PAST_BLOCKS > 1

    @pl.when(step > 0)
    def _():
        sb.first_step(blks, JOINT_PAST_BLOCKS)
        for blk in blks:
            sb.finish_partial_block(blk)

    @pl.when(step == 0)
    def _():
        for r, blk in enumerate(blks):
            sb.first_step([blk], min(r, JOINT_PAST_BLOCKS))
            if r >= JOINT_PAST_BLOCKS:
                sb.finish_partial_block(blk)

    for r, blk in enumerate(blks):
        sb.walk(blk, jnp.where(step > 0, blk.n_past - JOINT_PAST_BLOCKS,
                               r - min(r, JOINT_PAST_BLOCKS)) - 1)


def _sb_prompt(q, kt3, vt3, batch, seq):
    nq = seq // (KEY_BLOCK * Q_BLOCKS_PER_STEP)
    blk = pl.BlockSpec((KEY_BLOCK * Q_BLOCKS_PER_STEP, SB_WIDTH), lambda b, i: (b * nq + i, 0))
    keys = pl.BlockSpec((seq // KEY_BLOCK, SB_WIDTH, KEY_BLOCK), lambda b, i: (b, 0, 0))
    return pl.pallas_call(
        _sb_prompt_kernel,
        grid=(batch, nq),
        in_specs=[blk, keys, keys],
        out_specs=blk,
        out_shape=jax.ShapeDtypeStruct((batch * seq, SB_WIDTH), F32),
        scratch_shapes=[pltpu.VMEM((Q_BLOCKS_PER_STEP, N_PAIRS, 2 * KEY_BLOCK, LANES), F32)],
        compiler_params=_params(2),
        name="sb_prompt",
    )(q, kt3, vt3)


def _sb_sample_kernel(q_ref, kd_ref, vd_ref, kr_ref, vr_ref, kc_hbm, vc_hbm, o_ref,
                      carry_ref, kbuf_ref, vbuf_ref, sem_ref, *, past_len):
    tq = q_ref.shape[0]
    n_past = past_len // KEY_BLOCK
    n_recent = kr_ref.shape[2] // KEY_BLOCK
    b = pl.program_id(0)

    def diag_tiles():
        return ([kd_ref[0, rows, :] for rows in _PAIR_SLICES],
                [vd_ref[0, rows, :] for rows in _PAIR_SLICES])

    def recent_tiles(j):
        first = (j - (n_past - n_recent)) * KEY_BLOCK
        keys = slice(first, first + KEY_BLOCK)
        return ([kr_ref[0, rows, keys].astype(BF16) for rows in _PAIR_SLICES],
                [vr_ref[0, rows, keys].astype(BF16) for rows in _PAIR_SLICES])

    def cache_tiles(j):
        keys = pl.ds(pl.multiple_of(j * KEY_BLOCK, KEY_BLOCK), KEY_BLOCK)
        copies = [pltpu.make_async_copy(hbm.at[b, :, keys], buf, sem_ref.at[s])
                  for s, (hbm, buf) in enumerate(((kc_hbm, kbuf_ref), (vc_hbm, vbuf_ref)))]
        for c in copies:
            c.start()
        for c in copies:
            c.wait()
        return ([kbuf_ref[rows, :].astype(BF16) for rows in _PAIR_SLICES],
                [vbuf_ref[rows, :].astype(BF16) for rows in _PAIR_SLICES])

    assert n_recent == JOINT_PAST_BLOCKS > 1
    sb = _StickBreaking(tq, (b % (KEY_BLOCK // tq)) * tq)
    blk = _QueryBlock(q_ref, o_ref, carry_ref, n_past, diag_tiles, recent_tiles, cache_tiles)
    sb.first_step([blk], JOINT_PAST_BLOCKS)
    sb.finish_partial_block(blk)
    sb.walk(blk, n_past - JOINT_PAST_BLOCKS - 1)


def _sb_sample(q, kt3, vt3, cache_kt, cache_vt, t_new, first_row):
    batch, _, past_len = cache_kt.shape
    n_recent = JOINT_PAST_BLOCKS * KEY_BLOCK
    seqs_per_block = KEY_BLOCK // t_new
    new = pl.BlockSpec((1, SB_WIDTH, KEY_BLOCK),
                       lambda b: (first_row // KEY_BLOCK + b // seqs_per_block, 0, 0))
    recent = pl.BlockSpec((1, SB_WIDTH, n_recent), lambda b: (b, 0, past_len // n_recent - 1))
    hbm = pl.BlockSpec(memory_space=pl.ANY)
    return pl.pallas_call(
        functools.partial(_sb_sample_kernel, past_len=past_len),
        grid=(batch,),
        in_specs=[pl.BlockSpec((t_new, SB_WIDTH), lambda b: (first_row // t_new + b, 0)),
                  new, new, recent, recent, hbm, hbm],
        out_specs=pl.BlockSpec((t_new, SB_WIDTH), lambda b: (b, 0)),
        out_shape=jax.ShapeDtypeStruct((batch * t_new, SB_WIDTH), F32),
        scratch_shapes=[pltpu.VMEM((N_PAIRS, 2 * t_new, LANES), F32),
                        pltpu.VMEM((SB_WIDTH, KEY_BLOCK), F32),
                        pltpu.VMEM((SB_WIDTH, KEY_BLOCK), F32),
                        pltpu.SemaphoreType.DMA((2,))],
        compiler_params=_params(1),
        name="sb_sample",
    )(q, kt3, vt3, cache_kt, cache_vt, cache_kt, cache_vt)


def _mix_rows(x_ref, op_ref, os_ref, u_ref, gn_ref, ws_ref, bias_ref, gsb_ref, gsgu_ref, wout_ref,
              post_ref, sgu_ref, rows):
    lane_lo = lax.broadcasted_iota(jnp.int32, (1, LANES), 1) < HEAD_DIM
    for p in range(N_PAIRS):
        cols = slice(p * LANES, (p + 1) * LANES)
        for first in range(rows.start, rows.stop, SGU_CHUNK):
            chunk = slice(first, first + SGU_CHUNK)
            gg = _pair_rows(gn_ref[chunk, cols].astype(BF16), lane_lo)
            mixed = jnp.dot(ws_ref[0, p], gg, preferred_element_type=F32) + bias_ref[0, :, cols]
            sgu_ref[chunk, cols] = u_ref[chunk, cols] * mixed
    o_sb = jnp.where(_is_sample_step(), os_ref[rows, :], op_ref[rows, :])
    merged = jnp.concatenate([_rms(o_sb, gsb_ref[...]).astype(BF16),
                              _rms(sgu_ref[rows, :], gsgu_ref[...]).astype(BF16)], axis=1)
    y = jnp.dot(merged, wout_ref[...], preferred_element_type=F32)
    return x_ref[rows, :] + _rms(y, post_ref[...])


def _mix_ffn_kernel(x_ref, op_ref, os_ref, u_ref, gn_ref, ws_ref, bias_ref, gsb_ref, gsgu_ref,
                    wout_ref, mix_post_ref, pre_ref, post_ref, wg_hbm, wu_hbm, wd_hbm,
                    yp_ref, ys_ref, sgu_ref,
                    wg_ref, wu_ref, wd_ref, stage_g_ref, stage_u_ref, stage_d_ref, sem_ref, act_ref):
    _load_ffn_weights(wg_hbm, wu_hbm, wd_hbm, wg_ref, wu_ref, wd_ref,
                      (stage_g_ref, stage_u_ref, stage_d_ref), sem_ref)

    def mix(rows):
        return _mix_rows(x_ref, op_ref, os_ref, u_ref, gn_ref, ws_ref, bias_ref, gsb_ref, gsgu_ref,
                         wout_ref, mix_post_ref, sgu_ref, rows)

    groups = FFN_ROW_GROUPS
    x = mix(groups[0])
    for g, rows in enumerate(groups):
        stages = _ffn_stages(x, pre_ref, post_ref, wg_ref, wu_ref, wd_ref, act_ref, rows)
        stages[0]()
        stages[1]()
        if g + 1 < len(groups):
            x = mix(groups[g + 1])
        for stage in stages[2:]:
            y = stage()
        yp_ref[rows, :] = y

    @pl.when(_is_sample_step())
    def _():
        ys_ref[...] = yp_ref[...]


def _mix_ffn(x, o_prompt, o_sample, u, gn, ws_pairs, bias, g_sb, g_sgu, w_out, mix_post_g,
             pre_g, post_g, wg, wu, wd, n_prompt_tiles):
    row = _merged_tile_spec(D_MODEL, n_prompt_tiles)
    half = _merged_tile_spec(SB_WIDTH, n_prompt_tiles)
    which = lambda s: jnp.where(s == 0, 1, 0)
    sds = lambda rows: jax.ShapeDtypeStruct((rows, D_MODEL), F32)
    return pl.pallas_call(
        _mix_ffn_kernel,
        grid=(n_prompt_tiles + 1,),
        in_specs=[row, _prompt_tile_spec(SB_WIDTH), _sample_tile_spec(SB_WIDTH), half, half,
                  pl.BlockSpec((1,) + ws_pairs.shape[1:], lambda s: (which(s), 0, 0, 0)),
                  pl.BlockSpec((1,) + bias.shape[1:], lambda s: (which(s), 0, 0)),
                  _const_spec((1, SB_WIDTH)), _const_spec((1, SGU_WIDTH)),
                  _const_spec((D_MODEL, D_MODEL)), _const_spec((1, D_MODEL))] + _FFN_WEIGHT_SPECS,
        out_specs=[_prompt_tile_spec(D_MODEL), _sample_tile_spec(D_MODEL)],
        out_shape=[sds(n_prompt_tiles * ROW_TILE), sds(ROW_TILE)],
        scratch_shapes=[pltpu.VMEM((ROW_TILE, SGU_WIDTH), F32)] + _FFN_SCRATCH,
        compiler_params=_params(1),
        name="mix_ffn",
    )(x, o_prompt, o_sample, u, gn, ws_pairs, bias, g_sb, g_sgu, w_out, mix_post_g,
      pre_g, post_g, wg, wu, wd)


def _sgu_operands(w_s, b_s, t_new):
    i = jnp.arange(SGU_CHUNK)
    mask = (i[None, :] // SGU_CAUSAL_CHUNK) <= (i[:, None] // SGU_CAUSAL_CHUNK)
    w_prompt = w_s * mask[None].astype(w_s.dtype)
    reps = SGU_CHUNK // t_new
    corner = w_prompt[:, :t_new, :t_new]
    eye = jnp.eye(reps, dtype=w_s.dtype)
    w_sample = jnp.einsum('ab,gij->gaibj', eye, corner).reshape(w_s.shape)
    groups = w_s.shape[0]

    def pairs(w):
        w = w.reshape(groups // 2, 2, SGU_CHUNK, SGU_CHUNK).transpose(0, 2, 1, 3)
        return w.reshape(groups // 2, SGU_CHUNK, 2 * SGU_CHUNK)

    def bias(b):
        return jnp.repeat(b.T, SGU_WIDTH // groups, axis=1)

    ws = jnp.stack([pairs(w_prompt), pairs(w_sample)]).astype(BF16)
    return ws, jnp.stack([bias(b_s), bias(jnp.tile(b_s[:, :t_new], (1, reps)))])


def kernel(x_prompt, x_sample, cache_k_sb, cache_v_sb, ffn1_pre_g, ffn1_post_g, ffn1_w_gate, ffn1_w_up, ffn1_w_down, mix_pre_g, mix_post_g, w_in, sgu_ln_g, sgu_ln_b, sgu_w_s, sgu_b_s, g_out_sb, g_out_sgu, w_out, ffn2_pre_g, ffn2_post_g, ffn2_w_gate, ffn2_w_up, ffn2_w_down):
    depth, batch, seq = w_in.shape[0], x_prompt.shape[0], x_prompt.shape[1]
    dec_batch, t_new, past_len = x_sample.shape[0], x_sample.shape[1], cache_k_sb.shape[2]
    assert depth == 1 and dec_batch * t_new == ROW_TILE and seq % ROW_TILE == 0
    assert SGU_CHUNK % t_new == 0 and KEY_BLOCK % t_new == 0
    assert past_len % (JOINT_PAST_BLOCKS * KEY_BLOCK) == 0
    n_prompt_tiles = batch * seq // ROW_TILE

    mat = lambda w: w[0].astype(BF16)
    vec = lambda g: g[0][None, :]

    x = _ffn(x_prompt.reshape(batch * seq, D_MODEL), x_sample.reshape(ROW_TILE, D_MODEL),
             vec(ffn1_pre_g), vec(ffn1_post_g), ffn1_w_gate[0], ffn1_w_up[0], ffn1_w_down[0],
             n_prompt_tiles)
    q, u, gn, kt3, vt3, kt_p, vt_p, kt_s, vt_s = _inproj(
        x, vec(mix_pre_g), w_in[0], vec(sgu_ln_g), vec(sgu_ln_b), n_prompt_tiles, seq // ROW_TILE)

    o_prompt = _sb_prompt(q, kt3, vt3, batch, seq)
    cache_t = lambda c: c[0].transpose(0, 2, 3, 1).reshape(dec_batch, SB_WIDTH, past_len)
    o_sample = _sb_sample(q, kt3, vt3, cache_t(cache_k_sb), cache_t(cache_v_sb), t_new, batch * seq)

    ws_pairs, bias = _sgu_operands(sgu_w_s[0], sgu_b_s[0], t_new)
    y_p, y_s = _mix_ffn(x, o_prompt, o_sample, u, gn, ws_pairs, bias, vec(g_out_sb), vec(g_out_sgu),
                        mat(w_out), vec(mix_post_g), vec(ffn2_pre_g), vec(ffn2_post_g),
                        ffn2_w_gate[0], ffn2_w_up[0], ffn2_w_down[0], n_prompt_tiles)

    def prompt_heads(t):
        return t.reshape(batch, N_HEADS, HEAD_DIM, seq).transpose(0, 3, 1, 2)[None]

    def sample_heads(t):
        return t.reshape(N_HEADS, HEAD_DIM, dec_batch, t_new).transpose(2, 3, 0, 1)[None]

    g_s = gn[batch * seq:].reshape(1, dec_batch, t_new, N_HEADS, HEAD_DIM)
    return (y_p.reshape(batch, seq, D_MODEL), y_s.reshape(dec_batch, t_new, D_MODEL),
            prompt_heads(kt_p), prompt_heads(vt_p), sample_heads(kt_s), sample_heads(vt_s), g_s)
```

```python
import functools

import jax
import jax.numpy as jnp
from jax import lax
from jax.experimental import pallas as pl
from jax.experimental.pallas import tpu as pltpu

D_MODEL = 1024
D_FF = 2816
SB_WIDTH = 512
SGU_WIDTH = 512
HEAD_DIM = 64
N_HEADS = SB_WIDTH // HEAD_DIM
SGU_CHUNK = 128
SGU_CAUSAL_CHUNK = 64
FFN_RES = 0.5
EPS = 1e-6

LANES = 128
N_PAIRS = SB_WIDTH // LANES
KEY_BLOCK = 128
JOINT_PAST_BLOCKS = 2
TOP_ROWS = 32
Q_BLOCKS_PER_STEP = 4
Q_BLOCKS_PER_PASS = 2
FF_CHUNKS = (1536, 1280)
ROW_TILE = 512
WEIGHT_CHUNK_ROWS = 128
FFN_ROW_GROUPS = (slice(0, 256), slice(256, 512))
VMEM_LIMIT = 56 * 1024 * 1024

LOG2_E = 1.4426950408889634
USED_STICK_CUTOFF = 105.0 * LOG2_E
HIDDEN_LOGIT = -1e30

F32 = jnp.float32
BF16 = jnp.bfloat16


def _rms(x, g):
    return x * lax.rsqrt(jnp.mean(x * x, axis=-1, keepdims=True) + EPS) * g


def _gelu(x):
    return 0.5 * x * (1.0 + lax.erf(x * (0.5 ** 0.5)))


def _const_spec(shape):
    return pl.BlockSpec(shape, lambda *_: (0,) * len(shape), pipeline_mode=pl.Buffered(1))


def _params(n_axes):
    return pltpu.CompilerParams(dimension_semantics=("arbitrary",) * n_axes,
                                vmem_limit_bytes=VMEM_LIMIT)


def _is_sample_step():
    return pl.program_id(0) == 0


def _merged_tile_spec(width, n_prompt_tiles):
    return pl.BlockSpec((ROW_TILE, width), lambda s: ((s + n_prompt_tiles) % (n_prompt_tiles + 1), 0))


def _prompt_tile_spec(width, rows=ROW_TILE):
    return pl.BlockSpec((rows, width), lambda s: (jnp.maximum(s - 1, 0), 0))


def _sample_tile_spec(width, rows=ROW_TILE):
    return pl.BlockSpec((rows, width), lambda s: (0, 0))


def _ffn_stages(x, pre_ref, post_ref, wg_ref, wu_ref, wd_ref, act_ref, rows):
    state = {}

    def hidden():
        state['h'] = _rms(x, pre_ref[...]).astype(BF16)

    def chunk(cols):
        gate = jnp.dot(state['h'], wg_ref[:, cols], preferred_element_type=F32)
        up = jnp.dot(state['h'], wu_ref[:, cols], preferred_element_type=F32)
        act_ref[rows, cols] = (gate * jax.nn.sigmoid(gate) * up).astype(BF16)

    def finish():
        f = jnp.dot(act_ref[rows, :], wd_ref[...], preferred_element_type=F32)
        return x + FFN_RES * _rms(f, post_ref[...])

    stages, start = [hidden], 0
    for width in FF_CHUNKS:
        stages.append(functools.partial(chunk, slice(start, start + width)))
        start += width
    return stages + [finish]


def _ffn_rows(x, pre_ref, post_ref, wg_ref, wu_ref, wd_ref, act_ref, rows):
    for stage in _ffn_stages(x, pre_ref, post_ref, wg_ref, wu_ref, wd_ref, act_ref, rows):
        y = stage()
    return y


_HBM_SPEC = pl.BlockSpec(memory_space=pl.ANY)
_FFN_WEIGHT_SPECS = [_const_spec((1, D_MODEL)), _const_spec((1, D_MODEL)), _HBM_SPEC, _HBM_SPEC, _HBM_SPEC]
_FFN_SCRATCH = [pltpu.VMEM((D_MODEL, D_FF), BF16), pltpu.VMEM((D_MODEL, D_FF), BF16),
                pltpu.VMEM((D_FF, D_MODEL), BF16),
                pltpu.VMEM((2, WEIGHT_CHUNK_ROWS, D_FF), F32),
                pltpu.VMEM((2, WEIGHT_CHUNK_ROWS, D_FF), F32),
                pltpu.VMEM((2, WEIGHT_CHUNK_ROWS, D_MODEL), F32),
                pltpu.SemaphoreType.DMA((6,)),
                pltpu.VMEM((ROW_TILE, D_FF), BF16)]


def _load_bf16(streams):
    def n_chunks(stream):
        return stream[0].shape[0] // stream[2].shape[1]

    def copy(stream, i):
        hbm_ref, _, stage_ref, sem_ref, first_sem = stream
        chunk = stage_ref.shape[1]
        return pltpu.make_async_copy(hbm_ref.at[pl.ds(i * chunk, chunk), :], stage_ref.at[i % 2],
                                     sem_ref.at[first_sem + i % 2])

    for stream in streams:
        copy(stream, 0).start()
    order = sorted(((i + 1) / n_chunks(stream), k, i)
                   for k, stream in enumerate(streams) for i in range(n_chunks(stream)))
    for _, k, i in order:
        stream = streams[k]
        if i + 1 < n_chunks(stream):
            copy(stream, i + 1).start()
        copy(stream, i).wait()
        chunk = stream[2].shape[1]
        stream[1][pl.ds(i * chunk, chunk), :] = stream[2][i % 2].astype(BF16)


def _load_ffn_weights(wg_hbm, wu_hbm, wd_hbm, wg_ref, wu_ref, wd_ref, stage_refs, sem_ref):
    @pl.when(pl.program_id(0) == 0)
    def _():
        _load_bf16([(hbm, dst, stage, sem_ref, 2 * k) for k, (hbm, dst, stage) in enumerate(
            zip((wg_hbm, wu_hbm, wd_hbm), (wg_ref, wu_ref, wd_ref), stage_refs))])


def _ffn_kernel(xp_ref, xs_ref, pre_ref, post_ref, wg_hbm, wu_hbm, wd_hbm, o_ref,
                wg_ref, wu_ref, wd_ref, stage_g_ref, stage_u_ref, stage_d_ref, sem_ref, act_ref):
    _load_ffn_weights(wg_hbm, wu_hbm, wd_hbm, wg_ref, wu_ref, wd_ref,
                      (stage_g_ref, stage_u_ref, stage_d_ref), sem_ref)
    for rows in FFN_ROW_GROUPS:
        x = jnp.where(_is_sample_step(), xs_ref[rows, :], xp_ref[rows, :])
        o_ref[rows, :] = _ffn_rows(x, pre_ref, post_ref, wg_ref, wu_ref, wd_ref, act_ref, rows)


def _ffn(x_prompt, x_sample, pre_g, post_g, wg, wu, wd, n_prompt_tiles):
    n_tiles = n_prompt_tiles + 1
    return pl.pallas_call(
        _ffn_kernel,
        grid=(n_tiles,),
        in_specs=[_prompt_tile_spec(D_MODEL), _sample_tile_spec(D_MODEL)] + _FFN_WEIGHT_SPECS,
        out_specs=_merged_tile_spec(D_MODEL, n_prompt_tiles),
        out_shape=jax.ShapeDtypeStruct((n_tiles * ROW_TILE, D_MODEL), F32),
        scratch_shapes=_FFN_SCRATCH,
        compiler_params=_params(1),
        name="ffn",
    )(x_prompt, x_sample, pre_g, post_g, wg, wu, wd)


def _inproj_kernel(x_ref, g_ref, win_hbm, lng_ref, lnb_ref,
                   q_ref, u_ref, gn_ref, kt3_ref, vt3_ref, ktp_ref, vtp_ref, kts_ref, vts_ref,
                   wqug_ref, wkvt_ref, stage_ref, sem_ref):
    w = SB_WIDTH

    @pl.when(pl.program_id(0) == 0)
    def _():
        chunk = stage_ref.shape[1]
        n_chunks = win_hbm.shape[0] // chunk

        def copy(i):
            return pltpu.make_async_copy(win_hbm.at[pl.ds(i * chunk, chunk), :], stage_ref.at[i % 2],
                                         sem_ref.at[i % 2])

        copy(0).start()
        for i in range(n_chunks):
            if i + 1 < n_chunks:
                copy(i + 1).start()
            copy(i).wait()
            rows = pl.ds(i * chunk, chunk)
            part = stage_ref[i % 2]
            wqug_ref[rows, 0:w] = part[:, 0:w].astype(BF16)
            wqug_ref[rows, w:] = part[:, 3 * w:].astype(BF16)
            wkvt_ref[:, rows] = part[:, w:3 * w].T.astype(BF16)

    h = _rms(x_ref[...], g_ref[...]).astype(BF16)
    z = jnp.dot(h, wqug_ref[...], preferred_element_type=F32)
    kvt = lax.dot_general(wkvt_ref[...], h, (((1,), (1,)), ((), ())), preferred_element_type=F32)
    q_ref[...] = (z[:, 0:w] * (HEAD_DIM ** -0.5 * LOG2_E)).astype(BF16)
    u_ref[...] = _gelu(z[:, w:w + SGU_WIDTH])
    ge = _gelu(z[:, w + SGU_WIDTH:])
    xc = ge - jnp.mean(ge, axis=-1, keepdims=True)
    y = xc * lax.rsqrt(jnp.mean(xc * xc, axis=-1, keepdims=True) + EPS)
    gn_ref[...] = y * lng_ref[...] + lnb_ref[...]
    for t_ref, t3_ref, rows in ((ktp_ref, kt3_ref, slice(0, w)), (vtp_ref, vt3_ref, slice(w, 2 * w))):
        t = kvt[rows]
        t_ref[...] = t
        for c in range(ROW_TILE // KEY_BLOCK):
            t3_ref[c] = t[:, c * KEY_BLOCK:(c + 1) * KEY_BLOCK].astype(BF16)

    @pl.when(_is_sample_step())
    def _():
        kts_ref[...] = ktp_ref[...]
        vts_ref[...] = vtp_ref[...]


def _inproj(x, g, w_in, ln_g, ln_b, n_prompt_tiles, prompt_tiles_per_row):
    n = x.shape[0]
    half = _merged_tile_spec(SB_WIDTH, n_prompt_tiles)
    blocks_per_tile = ROW_TILE // KEY_BLOCK
    key_blocks = pl.BlockSpec((blocks_per_tile, SB_WIDTH, KEY_BLOCK),
                              lambda s: ((s + n_prompt_tiles) % (n_prompt_tiles + 1), 0, 0))

    def prompt_t_index(s):
        tile = jnp.maximum(s - 1, 0)
        return tile // prompt_tiles_per_row, tile % prompt_tiles_per_row

    prompt_t = pl.BlockSpec((SB_WIDTH, ROW_TILE), prompt_t_index)
    prompt_t_sds = jax.ShapeDtypeStruct(
        (n_prompt_tiles // prompt_tiles_per_row * SB_WIDTH, prompt_tiles_per_row * ROW_TILE), F32)
    sample_t_sds = jax.ShapeDtypeStruct((SB_WIDTH, ROW_TILE), F32)
    f32_half = jax.ShapeDtypeStruct((n, SB_WIDTH), F32)
    key_blocks_sds = jax.ShapeDtypeStruct((n // KEY_BLOCK, SB_WIDTH, KEY_BLOCK), BF16)
    return pl.pallas_call(
        _inproj_kernel,
        grid=(n // ROW_TILE,),
        in_specs=[_merged_tile_spec(D_MODEL, n_prompt_tiles), _const_spec((1, D_MODEL)),
                  _HBM_SPEC, _const_spec((1, SGU_WIDTH)), _const_spec((1, SGU_WIDTH))],
        out_specs=[half] * 3 + [key_blocks] * 2 + [prompt_t] * 2 + [_sample_tile_spec(ROW_TILE, SB_WIDTH)] * 2,
        out_shape=[jax.ShapeDtypeStruct((n, SB_WIDTH), BF16), f32_half, f32_half,
                   key_blocks_sds, key_blocks_sds, prompt_t_sds, prompt_t_sds,
                   sample_t_sds, sample_t_sds],
        scratch_shapes=[pltpu.VMEM((D_MODEL, SB_WIDTH + 2 * SGU_WIDTH), BF16),
                        pltpu.VMEM((2 * SB_WIDTH, D_MODEL), BF16),
                        pltpu.VMEM((2, WEIGHT_CHUNK_ROWS, w_in.shape[1]), F32),
                        pltpu.SemaphoreType.DMA((2,))],
        compiler_params=_params(1),
        name="inproj",
    )(x, g, w_in, ln_g, ln_b)


def _pair_rows(x, lane_lo):
    zero = jnp.zeros_like(x)
    return jnp.concatenate([jnp.where(lane_lo, x, zero), jnp.where(lane_lo, zero, x)], axis=0)


def _sb_blocks(qqs, kts, vts, carries, suffix_ones, lane_lo, masks, tops=None):
    n_blocks = len(masks)
    tops = tops or [None] * n_blocks
    t = qqs[0].shape[0] // 2
    nt_dims = (((1,), (1,)), ((), ()))

    def head_rows(x, top):
        return x if top is None else jnp.concatenate([x[:top], x[t:t + top]], axis=0)

    def weighted_values(a, vt):
        both = lax.dot_general(a, vt, nt_dims, preferred_element_type=F32)
        rows = a.shape[0] // 2
        return jnp.where(lane_lo, both[:rows], both[rows:])

    zs = []
    for qq, kts_p in zip(qqs, kts):
        zs_p, b = [], 0
        while b < n_blocks:
            if b + 1 < n_blocks and tops[b] is None and tops[b + 1] is None:
                z2 = jnp.dot(qq, jnp.concatenate([kts_p[b], kts_p[b + 1]], axis=1),
                             preferred_element_type=F32)
                zs_p += [z2[:, :KEY_BLOCK], z2[:, KEY_BLOCK:]]
                b += 2
            else:
                zs_p.append(jnp.dot(head_rows(qq, tops[b]), kts_p[b], preferred_element_type=F32))
                b += 1
        zs.append(zs_p)
    useds, splits = [], []
    for zs_p in zs:
        useds.append([])
        splits.append([])
        for i, mask in enumerate(masks):
            z = zs_p[i]
            if mask is not None:
                z = zs_p[i] = jnp.where(mask, z, HIDDEN_LOGIT)
            used = jnp.maximum(z, 0.0) + jnp.log(1.0 + jnp.exp2(-jnp.abs(z))) * LOG2_E
            hi = used.astype(BF16)
            lo = (used - hi.astype(F32)).astype(BF16)
            useds[-1].append(used)
            splits[-1].append(jnp.concatenate([hi, lo], axis=1))
    sums = [[jnp.dot(s, suffix_ones, preferred_element_type=F32) for s in splits_p]
            for splits_p in splits]
    outs, new_carries = [], []
    for zs_p, useds_p, sums_p, vts_p, carry in zip(zs, useds, sums, vts, carries):
        weights, values_t, top_outs = [], [], []
        for z, used, s, vt, top in zip(zs_p, useds_p, sums_p, vts_p, tops):
            a = jnp.exp2(z - used - (s[:, :KEY_BLOCK] + head_rows(carry, top))).astype(BF16)
            row_sum = s[:, KEY_BLOCK:]
            if top is None:
                weights.append(a)
                values_t.append(vt)
                carry = carry + row_sum
            else:
                top_outs.append(weighted_values(a, vt))
                carry = jnp.concatenate([carry[:top] + row_sum[:top], carry[top:t],
                                         carry[t:t + top] + row_sum[top:], carry[t + top:]], axis=0)
        out = weighted_values(jnp.concatenate(weights, axis=1), jnp.concatenate(values_t, axis=1))
        for o_top in top_outs:
            top = o_top.shape[0]
            out = jnp.concatenate([out[:top] + o_top, out[top:]], axis=0)
        outs.append(out)
        new_carries.append(carry)
    return outs, new_carries


def _suffix_ones():
    r = lax.broadcasted_iota(jnp.int32, (2 * KEY_BLOCK, 2 * KEY_BLOCK), 0) % KEY_BLOCK
    c = lax.broadcasted_iota(jnp.int32, (2 * KEY_BLOCK, 2 * KEY_BLOCK), 1)
    return jnp.where((c >= KEY_BLOCK) | (r > c), 1.0, 0.0).astype(BF16)


_PAIR_SLICES = [slice(p * LANES, (p + 1) * LANES) for p in range(N_PAIRS)]


class _QueryBlock:
    def __init__(self, q_ref, o_ref, carry_ref, n_past, diag_tiles, joint_tiles, walk_tiles):
        self.q_ref, self.o_ref, self.carry_ref, self.n_past = q_ref, o_ref, carry_ref, n_past
        self.diag_tiles, self.joint_tiles, self.walk_tiles = diag_tiles, joint_tiles, walk_tiles


class _StickBreaking:
    def __init__(self, tq, diag_key_offset):
        self.tq = tq
        self.lane_lo = lax.broadcasted_iota(jnp.int32, (1, LANES), 1) < HEAD_DIM
        self.suffix_ones = _suffix_ones()
        self.row = lax.broadcasted_iota(jnp.int32, (2 * tq, KEY_BLOCK), 0) % tq
        col = lax.broadcasted_iota(jnp.int32, (2 * tq, KEY_BLOCK), 1) - diag_key_offset
        self.causal = (col >= 0) & (col < self.row)

    def _qq_tiles(self, blk):
        return [_pair_rows(blk.q_ref[:, cols], self.lane_lo) for cols in _PAIR_SLICES]

    def _store(self, blk, outs, carries, first):
        for p, cols in enumerate(_PAIR_SLICES):
            if first:
                blk.o_ref[:, cols] = outs[p]
            else:
                blk.o_ref[:, cols] += outs[p]
            blk.carry_ref[p] = carries[p]

    def first_step(self, blks, n_joint):
        qqs, kts, vts = [], [], []
        for blk in blks:
            kd, vd = blk.diag_tiles()
            kts_b, vts_b = [[t] for t in kd], [[t] for t in vd]
            for b in range(n_joint):
                kj, vj = blk.joint_tiles(blk.n_past - 1 - b)
                for p in range(N_PAIRS):
                    kts_b[p].append(kj[p])
                    vts_b[p].append(vj[p])
            qqs += self._qq_tiles(blk)
            kts += kts_b
            vts += vts_b
        tops = [None] * n_joint + ([TOP_ROWS] if n_joint > 1 else [None])
        outs, carries = _sb_blocks(qqs, kts, vts, [jnp.zeros((2 * self.tq, LANES), F32)] * len(qqs),
                                   self.suffix_ones, self.lane_lo, [self.causal] + [None] * n_joint, tops)
        for i, blk in enumerate(blks):
            pairs = slice(i * N_PAIRS, (i + 1) * N_PAIRS)
            self._store(blk, outs[pairs], carries[pairs], True)

    def _single_block(self, blk, tiles, mask):
        kts, vts = tiles
        outs, carries = _sb_blocks(self._qq_tiles(blk), [[t] for t in kts], [[t] for t in vts],
                                   [blk.carry_ref[p] for p in range(N_PAIRS)],
                                   self.suffix_ones, self.lane_lo, [mask])
        self._store(blk, outs, carries, False)

    def _least_used(self, blk, rows=None):
        tq = self.tq

        def pick(x):
            return x if rows is None else jnp.concatenate([x[rows], x[tq + rows.start:tq + rows.stop]], axis=0)
        return jnp.min(functools.reduce(jnp.minimum, [pick(blk.carry_ref[p]) for p in range(N_PAIRS)]))

    def finish_partial_block(self, blk):
        @pl.when(self._least_used(blk, slice(TOP_ROWS, self.tq)) < USED_STICK_CUTOFF)
        def _():
            self._single_block(blk, blk.joint_tiles(blk.n_past - JOINT_PAST_BLOCKS), self.row >= TOP_ROWS)

    def walk(self, blk, next_block):
        def body(state):
            j, _ = state
            self._single_block(blk, blk.walk_tiles(j), None)
            return j - 1, self._least_used(blk)

        lax.while_loop(lambda s: (s[0] >= 0) & (s[1] < USED_STICK_CUTOFF), body,
                       (next_block, self._least_used(blk)))


def _sb_prompt_kernel(q_ref, kt_ref, vt_ref, o_ref, carry_ref):
    def tiles(j):
        return ([kt_ref[j, rows, :] for rows in _PAIR_SLICES],
                [vt_ref[j, rows, :] for rows in _PAIR_SLICES])

    step = pl.program_id(1)
    sb = _StickBreaking(KEY_BLOCK, 0)
    blks = []
    for r in range(Q_BLOCKS_PER_STEP):
        i = step * Q_BLOCKS_PER_STEP + r
        rows = pl.ds(r * KEY_BLOCK, KEY_BLOCK)
        blks.append(_QueryBlock(q_ref.at[rows, :], o_ref.at[rows, :], carry_ref.at[r], i,
                                functools.partial(tiles, i), tiles, tiles))

    assert Q_BLOCKS_PER_STEP >= JOINT_PAST_BLOCKS > 1

    @pl.when(step > 0)
    def _():
        for first in range(0, Q_BLOCKS_PER_STEP, Q_BLOCKS_PER_PASS):
            sb.first_step(blks[first:first + Q_BLOCKS_PER_PASS], JOINT_PAST_BLOCKS)
        for blk in blks:
            sb.finish_partial_block(blk)

    @pl.when(step == 0)
    def _():
        for r, blk in enumerate(blks):
            sb.first_step([blk], min(r, JOINT_PAST_BLOCKS))
            if r >= JOINT_PAST_BLOCKS:
                sb.finish_partial_block(blk)

    for r, blk in enumerate(blks):
        sb.walk(blk, jnp.where(step > 0, blk.n_past - JOINT_PAST_BLOCKS,
                               r - min(r, JOINT_PAST_BLOCKS)) - 1)


def _sb_prompt(q, kt3, vt3, batch, seq):
    nq = seq // (KEY_BLOCK * Q_BLOCKS_PER_STEP)
    blk = pl.BlockSpec((KEY_BLOCK * Q_BLOCKS_PER_STEP, SB_WIDTH), lambda b, i: (b * nq + i, 0))
    keys = pl.BlockSpec((seq // KEY_BLOCK, SB_WIDTH, KEY_BLOCK), lambda b, i: (b, 0, 0))
    return pl.pallas_call(
        _sb_prompt_kernel,
        grid=(batch, nq),
        in_specs=[blk, keys, keys],
        out_specs=blk,
        out_shape=jax.ShapeDtypeStruct((batch * seq, SB_WIDTH), F32),
        scratch_shapes=[pltpu.VMEM((Q_BLOCKS_PER_STEP, N_PAIRS, 2 * KEY_BLOCK, LANES), F32)],
        compiler_params=_params(2),
        name="sb_prompt",
    )(q, kt3, vt3)


def _sb_sample_kernel(q_ref, kd_ref, vd_ref, kr_ref, vr_ref, kc_hbm, vc_hbm, o_ref,
                      carry_ref, kbuf_ref, vbuf_ref, sem_ref, *, past_len):
    tq = q_ref.shape[0]
    n_past = past_len // KEY_BLOCK
    n_recent = kr_ref.shape[2] // KEY_BLOCK
    b = pl.program_id(0)

    def diag_tiles():
        return ([kd_ref[0, rows, :] for rows in _PAIR_SLICES],
                [vd_ref[0, rows, :] for rows in _PAIR_SLICES])

    def recent_tiles(j):
        first = (j - (n_past - n_recent)) * KEY_BLOCK
        keys = slice(first, first + KEY_BLOCK)
        return ([kr_ref[0, rows, keys].astype(BF16) for rows in _PAIR_SLICES],
                [vr_ref[0, rows, keys].astype(BF16) for rows in _PAIR_SLICES])

    def cache_tiles(j):
        keys = pl.ds(pl.multiple_of(j * KEY_BLOCK, KEY_BLOCK), KEY_BLOCK)
        copies = [pltpu.make_async_copy(hbm.at[b, :, keys], buf, sem_ref.at[s])
                  for s, (hbm, buf) in enumerate(((kc_hbm, kbuf_ref), (vc_hbm, vbuf_ref)))]
        for c in copies:
            c.start()
        for c in copies:
            c.wait()
        return ([kbuf_ref[rows, :].astype(BF16) for rows in _PAIR_SLICES],
                [vbuf_ref[rows, :].astype(BF16) for rows in _PAIR_SLICES])

    assert n_recent == JOINT_PAST_BLOCKS > 1
    sb = _StickBreaking(tq, (b % (KEY_BLOCK // tq)) * tq)
    blk = _QueryBlock(q_ref, o_ref, carry_ref, n_past, diag_tiles, recent_tiles, cache_tiles)
    sb.first_step([blk], JOINT_PAST_BLOCKS)
    sb.finish_partial_block(blk)
    sb.walk(blk, n_past - JOINT_PAST_BLOCKS - 1)


def _sb_sample(q, kt3, vt3, cache_kt, cache_vt, t_new, first_row):
    batch, _, past_len = cache_kt.shape
    n_recent = JOINT_PAST_BLOCKS * KEY_BLOCK
    seqs_per_block = KEY_BLOCK // t_new
    new = pl.BlockSpec((1, SB_WIDTH, KEY_BLOCK),
                       lambda b: (first_row // KEY_BLOCK + b // seqs_per_block, 0, 0))
    recent = pl.BlockSpec((1, SB_WIDTH, n_recent), lambda b: (b, 0, past_len // n_recent - 1))
    hbm = pl.BlockSpec(memory_space=pl.ANY)
    return pl.pallas_call(
        functools.partial(_sb_sample_kernel, past_len=past_len),
        grid=(batch,),
        in_specs=[pl.BlockSpec((t_new, SB_WIDTH), lambda b: (first_row // t_new + b, 0)),
                  new, new, recent, recent, hbm, hbm],
        out_specs=pl.BlockSpec((t_new, SB_WIDTH), lambda b: (b, 0)),
        out_shape=jax.ShapeDtypeStruct((batch * t_new, SB_WIDTH), F32),
        scratch_shapes=[pltpu.VMEM((N_PAIRS, 2 * t_new, LANES), F32),
                        pltpu.VMEM((SB_WIDTH, KEY_BLOCK), F32),
                        pltpu.VMEM((SB_WIDTH, KEY_BLOCK), F32),
                        pltpu.SemaphoreType.DMA((2,))],
        compiler_params=_params(1),
        name="sb_sample",
    )(q, kt3, vt3, cache_kt, cache_vt, cache_kt, cache_vt)


def _mix_rows(x_ref, op_ref, os_ref, u_ref, gn_ref, ws_ref, bias_ref, gsb_ref, gsgu_ref, wout_ref,
              post_ref, sgu_ref, rows):
    lane_lo = lax.broadcasted_iota(jnp.int32, (1, LANES), 1) < HEAD_DIM
    for p in range(N_PAIRS):
        cols = slice(p * LANES, (p + 1) * LANES)
        for first in range(rows.start, rows.stop, SGU_CHUNK):
            chunk = slice(first, first + SGU_CHUNK)
            gg = _pair_rows(gn_ref[chunk, cols].astype(BF16), lane_lo)
            mixed = jnp.dot(ws_ref[0, p], gg, preferred_element_type=F32) + bias_ref[0, :, cols]
            sgu_ref[chunk, cols] = u_ref[chunk, cols] * mixed
    o_sb = jnp.where(_is_sample_step(), os_ref[rows, :], op_ref[rows, :])
    merged = jnp.concatenate([_rms(o_sb, gsb_ref[...]).astype(BF16),
                              _rms(sgu_ref[rows, :], gsgu_ref[...]).astype(BF16)], axis=1)
    y = jnp.dot(merged, wout_ref[...], preferred_element_type=F32)
    return x_ref[rows, :] + _rms(y, post_ref[...])


def _mix_ffn_kernel(x_ref, op_ref, os_ref, u_ref, gn_ref, ws_ref, bias_ref, gsb_ref, gsgu_ref,
                    wout_ref, mix_post_ref, pre_ref, post_ref, wg_hbm, wu_hbm, wd_hbm,
                    yp_ref, ys_ref, sgu_ref,
                    wg_ref, wu_ref, wd_ref, stage_g_ref, stage_u_ref, stage_d_ref, sem_ref, act_ref):
    _load_ffn_weights(wg_hbm, wu_hbm, wd_hbm, wg_ref, wu_ref, wd_ref,
                      (stage_g_ref, stage_u_ref, stage_d_ref), sem_ref)

    def mix(rows):
        return _mix_rows(x_ref, op_ref, os_ref, u_ref, gn_ref, ws_ref, bias_ref, gsb_ref, gsgu_ref,
                         wout_ref, mix_post_ref, sgu_ref, rows)

    groups = FFN_ROW_GROUPS
    x = mix(groups[0])
    for g, rows in enumerate(groups):
        stages = _ffn_stages(x, pre_ref, post_ref, wg_ref, wu_ref, wd_ref, act_ref, rows)
        stages[0]()
        stages[1]()
        if g + 1 < len(groups):
            x = mix(groups[g + 1])
        for stage in stages[2:]:
            y = stage()
        yp_ref[rows, :] = y

    @pl.when(_is_sample_step())
    def _():
        ys_ref[...] = yp_ref[...]


def _mix_ffn(x, o_prompt, o_sample, u, gn, ws_pairs, bias, g_sb, g_sgu, w_out, mix_post_g,
             pre_g, post_g, wg, wu, wd, n_prompt_tiles):
    row = _merged_tile_spec(D_MODEL, n_prompt_tiles)
    half = _merged_tile_spec(SB_WIDTH, n_prompt_tiles)
    which = lambda s: jnp.where(s == 0, 1, 0)
    sds = lambda rows: jax.ShapeDtypeStruct((rows, D_MODEL), F32)
    return pl.pallas_call(
        _mix_ffn_kernel,
        grid=(n_prompt_tiles + 1,),
        in_specs=[row, _prompt_tile_spec(SB_WIDTH), _sample_tile_spec(SB_WIDTH), half, half,
                  pl.BlockSpec((1,) + ws_pairs.shape[1:], lambda s: (which(s), 0, 0, 0)),
                  pl.BlockSpec((1,) + bias.shape[1:], lambda s: (which(s), 0, 0)),
                  _const_spec((1, SB_WIDTH)), _const_spec((1, SGU_WIDTH)),
                  _const_spec((D_MODEL, D_MODEL)), _const_spec((1, D_MODEL))] + _FFN_WEIGHT_SPECS,
        out_specs=[_prompt_tile_spec(D_MODEL), _sample_tile_spec(D_MODEL)],
        out_shape=[sds(n_prompt_tiles * ROW_TILE), sds(ROW_TILE)],
        scratch_shapes=[pltpu.VMEM((ROW_TILE, SGU_WIDTH), F32)] + _FFN_SCRATCH,
        compiler_params=_params(1),
        name="mix_ffn",
    )(x, o_prompt, o_sample, u, gn, ws_pairs, bias, g_sb, g_sgu, w_out, mix_post_g,
      pre_g, post_g, wg, wu, wd)


def _sgu_operands(w_s, b_s, t_new):
    i = jnp.arange(SGU_CHUNK)
    mask = (i[None, :] // SGU_CAUSAL_CHUNK) <= (i[:, None] // SGU_CAUSAL_CHUNK)
    w_prompt = w_s * mask[None].astype(w_s.dtype)
    reps = SGU_CHUNK // t_new
    corner = w_prompt[:, :t_new, :t_new]
    eye = jnp.eye(reps, dtype=w_s.dtype)
    w_sample = jnp.einsum('ab,gij->gaibj', eye, corner).reshape(w_s.shape)
    groups = w_s.shape[0]

    def pairs(w):
        w = w.reshape(groups // 2, 2, SGU_CHUNK, SGU_CHUNK).transpose(0, 2, 1, 3)
        return w.reshape(groups // 2, SGU_CHUNK, 2 * SGU_CHUNK)

    def bias(b):
        return jnp.repeat(b.T, SGU_WIDTH // groups, axis=1)

    ws = jnp.stack([pairs(w_prompt), pairs(w_sample)]).astype(BF16)
    return ws, jnp.stack([bias(b_s), bias(jnp.tile(b_s[:, :t_new], (1, reps)))])


def kernel(x_prompt, x_sample, cache_k_sb, cache_v_sb, ffn1_pre_g, ffn1_post_g, ffn1_w_gate, ffn1_w_up, ffn1_w_down, mix_pre_g, mix_post_g, w_in, sgu_ln_g, sgu_ln_b, sgu_w_s, sgu_b_s, g_out_sb, g_out_sgu, w_out, ffn2_pre_g, ffn2_post_g, ffn2_w_gate, ffn2_w_up, ffn2_w_down):
    depth, batch, seq = w_in.shape[0], x_prompt.shape[0], x_prompt.shape[1]
    dec_batch, t_new, past_len = x_sample.shape[0], x_sample.shape[1], cache_k_sb.shape[2]
    assert depth == 1 and dec_batch * t_new == ROW_TILE and seq % ROW_TILE == 0
    assert SGU_CHUNK % t_new == 0 and KEY_BLOCK % t_new == 0
    assert past_len % (JOINT_PAST_BLOCKS * KEY_BLOCK) == 0
    n_prompt_tiles = batch * seq // ROW_TILE

    mat = lambda w: w[0].astype(BF16)
    vec = lambda g: g[0][None, :]

    x = _ffn(x_prompt.reshape(batch * seq, D_MODEL), x_sample.reshape(ROW_TILE, D_MODEL),
             vec(ffn1_pre_g), vec(ffn1_post_g), ffn1_w_gate[0], ffn1_w_up[0], ffn1_w_down[0],
             n_prompt_tiles)
    q, u, gn, kt3, vt3, kt_p, vt_p, kt_s, vt_s = _inproj(
        x, vec(mix_pre_g), w_in[0], vec(sgu_ln_g), vec(sgu_ln_b), n_prompt_tiles, seq // ROW_TILE)

    o_prompt = _sb_prompt(q, kt3, vt3, batch, seq)
    cache_t = lambda c: c[0].transpose(0, 2, 3, 1).reshape(dec_batch, SB_WIDTH, past_len)
    o_sample = _sb_sample(q, kt3, vt3, cache_t(cache_k_sb), cache_t(cache_v_sb), t_new, batch * seq)

    ws_pairs, bias = _sgu_operands(sgu_w_s[0], sgu_b_s[0], t_new)
    y_p, y_s = _mix_ffn(x, o_prompt, o_sample, u, gn, ws_pairs, bias, vec(g_out_sb), vec(g_out_sgu),
                        mat(w_out), vec(mix_post_g), vec(ffn2_pre_g), vec(ffn2_post_g),
                        ffn2_w_gate[0], ffn2_w_up[0], ffn2_w_down[0], n_prompt_tiles)

    def prompt_heads(t):
        return t.reshape(batch, N_HEADS, HEAD_DIM, seq).transpose(0, 3, 1, 2)[None]

    def sample_heads(t):
        return t.reshape(N_HEADS, HEAD_DIM, dec_batch, t_new).transpose(2, 3, 0, 1)[None]

    g_s = gn[batch * seq:].reshape(1, dec_batch, t_new, N_HEADS, HEAD_DIM)
    return (y_p.reshape(batch, seq, D_MODEL), y_s.reshape(dec_batch, t_new, D_MODEL),
            prompt_heads(kt_p), prompt_heads(vt_p), sample_heads(kt_s), sample_heads(vt_s), g_s)
```

```python
import functools

import jax
import jax.numpy as jnp
from jax import lax
from jax.experimental import pallas as pl
from jax.experimental.pallas import tpu as pltpu

D_MODEL = 1024
D_FF = 2816
SB_WIDTH = 512
SGU_WIDTH = 512
HEAD_DIM = 64
N_HEADS = SB_WIDTH // HEAD_DIM
SGU_CHUNK = 128
SGU_CAUSAL_CHUNK = 64
FFN_RES = 0.5
EPS = 1e-6

LANES = 128
N_PAIRS = SB_WIDTH // LANES
KEY_BLOCK = 128
JOINT_PAST_BLOCKS = 2
TOP_ROWS = 32
Q_BLOCKS_PER_STEP = 4
Q_BLOCKS_PER_PASS = 2
FF_CHUNKS = (1536, 1280)
ROW_TILE = 512
WEIGHT_CHUNK_ROWS = 64
FFN_ROW_GROUPS = (slice(0, 256), slice(256, 512))
VMEM_LIMIT = 60 * 1024 * 1024

LOG2_E = 1.4426950408889634
USED_STICK_CUTOFF = 105.0 * LOG2_E
HIDDEN_LOGIT = -1e30

F32 = jnp.float32
BF16 = jnp.bfloat16


def _rms(x, g):
    return x * lax.rsqrt(jnp.mean(x * x, axis=-1, keepdims=True) + EPS) * g


def _gelu(x):
    return 0.5 * x * (1.0 + lax.erf(x * (0.5 ** 0.5)))


def _const_spec(shape):
    return pl.BlockSpec(shape, lambda *_: (0,) * len(shape), pipeline_mode=pl.Buffered(1))


def _params(n_axes):
    return pltpu.CompilerParams(dimension_semantics=("arbitrary",) * n_axes,
                                vmem_limit_bytes=VMEM_LIMIT)


def _is_sample_step():
    return pl.program_id(0) == 0


def _merged_tile_spec(width, n_prompt_tiles):
    return pl.BlockSpec((ROW_TILE, width), lambda s: ((s + n_prompt_tiles) % (n_prompt_tiles + 1), 0))


def _prompt_tile_spec(width, rows=ROW_TILE):
    return pl.BlockSpec((rows, width), lambda s: (jnp.maximum(s - 1, 0), 0))


def _sample_tile_spec(width, rows=ROW_TILE):
    return pl.BlockSpec((rows, width), lambda s: (0, 0))


def _ffn_hidden(x, pre_ref):
    return _rms(x, pre_ref[...]).astype(BF16)


def _ffn_chunk(h, wg_ref, wu_ref, act_ref, rows, c):
    cols = slice(sum(FF_CHUNKS[:c]), sum(FF_CHUNKS[:c + 1]))
    gate = jnp.dot(h, wg_ref[:, cols], preferred_element_type=F32)
    up = jnp.dot(h, wu_ref[:, cols], preferred_element_type=F32)
    act_ref[rows, cols] = (gate * jax.nn.sigmoid(gate) * up).astype(BF16)


def _ffn_down(act_ref, wd_ref, rows):
    return jnp.dot(act_ref[rows, :], wd_ref[...], preferred_element_type=F32)


def _ffn_out(x, f, post_ref):
    return x + FFN_RES * _rms(f, post_ref[...])


class _RotatedFfn:
    def __init__(self, pre_ref, post_ref, wg_ref, wu_ref, wd_ref, act_ref, h_ref):
        self.pre_ref, self.post_ref = pre_ref, post_ref
        self.wg_ref, self.wu_ref, self.wd_ref = wg_ref, wu_ref, wd_ref
        self.act_ref, self.h_ref = act_ref, h_ref
        self.first, self.second = FFN_ROW_GROUPS

    def _chunk(self, h, rows, c):
        _ffn_chunk(h, self.wg_ref, self.wu_ref, self.act_ref, rows, c)

    def prepare_first(self, x):
        self.h_ref[...] = _ffn_hidden(x, self.pre_ref)
        self._chunk(self.h_ref[...], self.first, 0)

    def run(self, first_x, second_x, next_first_x, prepare_first=None):
        prepare_first = prepare_first or self.prepare_first
        x_second = second_x()
        for c in range(1, len(FF_CHUNKS)):
            self._chunk(self.h_ref[...], self.first, c)
        h_second = _ffn_hidden(x_second, self.pre_ref)
        for c in range(len(FF_CHUNKS)):
            self._chunk(h_second, self.second, c)
        f_first = _ffn_down(self.act_ref, self.wd_ref, self.first)
        x_next = next_first_x()
        f_second = _ffn_down(self.act_ref, self.wd_ref, self.second)
        y_first = _ffn_out(first_x(), f_first, self.post_ref)
        prepare_first(x_next)
        return y_first, _ffn_out(x_second, f_second, self.post_ref)


_HBM_SPEC = pl.BlockSpec(memory_space=pl.ANY)
_FFN_WEIGHT_SPECS = [_const_spec((1, D_MODEL)), _const_spec((1, D_MODEL)), _HBM_SPEC, _HBM_SPEC, _HBM_SPEC]
_FFN_SCRATCH = [pltpu.VMEM((D_MODEL, D_FF), BF16), pltpu.VMEM((D_MODEL, D_FF), BF16),
                pltpu.VMEM((D_FF, D_MODEL), BF16),
                pltpu.VMEM((2, WEIGHT_CHUNK_ROWS, D_FF), F32),
                pltpu.VMEM((2, WEIGHT_CHUNK_ROWS, D_FF), F32),
                pltpu.VMEM((2, WEIGHT_CHUNK_ROWS, D_MODEL), F32),
                pltpu.SemaphoreType.DMA((6,)),
                pltpu.VMEM((ROW_TILE, D_FF), BF16),
                pltpu.VMEM((FFN_ROW_GROUPS[0].stop, D_MODEL), BF16)]


def _load_bf16(streams):
    def n_chunks(stream):
        return stream[0].shape[0] // stream[2].shape[1]

    def copy(stream, i):
        hbm_ref, _, stage_ref, sem_ref, first_sem = stream
        chunk = stage_ref.shape[1]
        return pltpu.make_async_copy(hbm_ref.at[pl.ds(i * chunk, chunk), :], stage_ref.at[i % 2],
                                     sem_ref.at[first_sem + i % 2])

    for stream in streams:
        copy(stream, 0).start()
    order = sorted(((i + 1) / n_chunks(stream), k, i)
                   for k, stream in enumerate(streams) for i in range(n_chunks(stream)))
    for _, k, i in order:
        stream = streams[k]
        if i + 1 < n_chunks(stream):
            copy(stream, i + 1).start()
        copy(stream, i).wait()
        chunk = stream[2].shape[1]
        stream[1][pl.ds(i * chunk, chunk), :] = stream[2][i % 2].astype(BF16)


def _load_ffn_weights(wg_hbm, wu_hbm, wd_hbm, wg_ref, wu_ref, wd_ref, stage_refs, sem_ref):
    @pl.when(pl.program_id(0) == 0)
    def _():
        _load_bf16([(hbm, dst, stage, sem_ref, 2 * k) for k, (hbm, dst, stage) in enumerate(
            zip((wg_hbm, wu_hbm, wd_hbm), (wg_ref, wu_ref, wd_ref), stage_refs))])


def _ffn_kernel(xp_ref, xs_ref, xnext_ref, pre_ref, post_ref, wg_hbm, wu_hbm, wd_hbm, o_ref,
                wg_ref, wu_ref, wd_ref, stage_g_ref, stage_u_ref, stage_d_ref, sem_ref, act_ref, h_ref):
    _load_ffn_weights(wg_hbm, wu_hbm, wd_hbm, wg_ref, wu_ref, wd_ref,
                      (stage_g_ref, stage_u_ref, stage_d_ref), sem_ref)
    ffn = _RotatedFfn(pre_ref, post_ref, wg_ref, wu_ref, wd_ref, act_ref, h_ref)

    def x_rows(rows):
        return jnp.where(_is_sample_step(), xs_ref[rows, :], xp_ref[rows, :])

    @pl.when(pl.program_id(0) == 0)
    def _():
        ffn.prepare_first(x_rows(ffn.first))

    o_ref[ffn.first, :], o_ref[ffn.second, :] = ffn.run(
        lambda: x_rows(ffn.first), lambda: x_rows(ffn.second), lambda: xnext_ref[...])


def _next_first_group_spec(width, n_tiles):
    rows = FFN_ROW_GROUPS[0].stop
    return pl.BlockSpec((rows, width), lambda s: (jnp.minimum(s, n_tiles - 1) * (ROW_TILE // rows), 0))


def _ffn(x_prompt, x_sample, pre_g, post_g, wg, wu, wd, n_prompt_tiles):
    n_tiles = n_prompt_tiles + 1
    return pl.pallas_call(
        _ffn_kernel,
        grid=(n_tiles,),
        in_specs=[_prompt_tile_spec(D_MODEL), _sample_tile_spec(D_MODEL),
                  _next_first_group_spec(D_MODEL, n_prompt_tiles)] + _FFN_WEIGHT_SPECS,
        out_specs=_merged_tile_spec(D_MODEL, n_prompt_tiles),
        out_shape=jax.ShapeDtypeStruct((n_tiles * ROW_TILE, D_MODEL), F32),
        scratch_shapes=_FFN_SCRATCH,
        compiler_params=_params(1),
        name="ffn",
    )(x_prompt, x_sample, x_prompt, pre_g, post_g, wg, wu, wd)


def _inproj_kernel(x_ref, g_ref, win_hbm, lng_ref, lnb_ref,
                   q_ref, u_ref, gn_ref, kt3_ref, vt3_ref, ktp_ref, vtp_ref, kts_ref, vts_ref,
                   wqug_ref, wkvt_ref, stage_ref, sem_ref):
    w = SB_WIDTH

    @pl.when(pl.program_id(0) == 0)
    def _():
        chunk = stage_ref.shape[1]
        n_chunks = win_hbm.shape[0] // chunk

        def copy(i):
            return pltpu.make_async_copy(win_hbm.at[pl.ds(i * chunk, chunk), :], stage_ref.at[i % 2],
                                         sem_ref.at[i % 2])

        copy(0).start()
        for i in range(n_chunks):
            if i + 1 < n_chunks:
                copy(i + 1).start()
            copy(i).wait()
            rows = pl.ds(i * chunk, chunk)
            part = stage_ref[i % 2]
            wqug_ref[rows, 0:w] = part[:, 0:w].astype(BF16)
            wqug_ref[rows, w:] = part[:, 3 * w:].astype(BF16)
            wkvt_ref[:, rows] = part[:, w:3 * w].T.astype(BF16)

    h = _rms(x_ref[...], g_ref[...]).astype(BF16)
    z = jnp.dot(h, wqug_ref[:, w:], preferred_element_type=F32)
    u_ref[...] = _gelu(z[:, :SGU_WIDTH])
    ge = _gelu(z[:, SGU_WIDTH:])
    xc = ge - jnp.mean(ge, axis=-1, keepdims=True)
    y = xc * lax.rsqrt(jnp.mean(xc * xc, axis=-1, keepdims=True) + EPS)
    gn_ref[...] = y * lng_ref[...] + lnb_ref[...]
    kvt = lax.dot_general(wkvt_ref[...], h, (((1,), (1,)), ((), ())), preferred_element_type=F32)
    for t_ref, t3_ref, rows in ((ktp_ref, kt3_ref, slice(0, w)), (vtp_ref, vt3_ref, slice(w, 2 * w))):
        t = kvt[rows]
        t_ref[...] = t
        for c in range(ROW_TILE // KEY_BLOCK):
            t3_ref[c] = t[:, c * KEY_BLOCK:(c + 1) * KEY_BLOCK].astype(BF16)
    q = jnp.dot(h, wqug_ref[:, :w], preferred_element_type=F32)
    q_ref[...] = (q * (HEAD_DIM ** -0.5 * LOG2_E)).astype(BF16)

    @pl.when(_is_sample_step())
    def _():
        kts_ref[...] = ktp_ref[...]
        vts_ref[...] = vtp_ref[...]


def _inproj(x, g, w_in, ln_g, ln_b, n_prompt_tiles, prompt_tiles_per_row):
    n = x.shape[0]
    half = _merged_tile_spec(SB_WIDTH, n_prompt_tiles)
    blocks_per_tile = ROW_TILE // KEY_BLOCK
    key_blocks = pl.BlockSpec((blocks_per_tile, SB_WIDTH, KEY_BLOCK),
                              lambda s: ((s + n_prompt_tiles) % (n_prompt_tiles + 1), 0, 0))

    def prompt_t_index(s):
        tile = jnp.maximum(s - 1, 0)
        return tile // prompt_tiles_per_row, tile % prompt_tiles_per_row

    prompt_t = pl.BlockSpec((SB_WIDTH, ROW_TILE), prompt_t_index)
    prompt_t_sds = jax.ShapeDtypeStruct(
        (n_prompt_tiles // prompt_tiles_per_row * SB_WIDTH, prompt_tiles_per_row * ROW_TILE), F32)
    sample_t_sds = jax.ShapeDtypeStruct((SB_WIDTH, ROW_TILE), F32)
    f32_half = jax.ShapeDtypeStruct((n, SB_WIDTH), F32)
    key_blocks_sds = jax.ShapeDtypeStruct((n // KEY_BLOCK, SB_WIDTH, KEY_BLOCK), BF16)
    return pl.pallas_call(
        _inproj_kernel,
        grid=(n // ROW_TILE,),
        in_specs=[_merged_tile_spec(D_MODEL, n_prompt_tiles), _const_spec((1, D_MODEL)),
                  _HBM_SPEC, _const_spec((1, SGU_WIDTH)), _const_spec((1, SGU_WIDTH))],
        out_specs=[half] * 3 + [key_blocks] * 2 + [prompt_t] * 2 + [_sample_tile_spec(ROW_TILE, SB_WIDTH)] * 2,
        out_shape=[jax.ShapeDtypeStruct((n, SB_WIDTH), BF16), f32_half, f32_half,
                   key_blocks_sds, key_blocks_sds, prompt_t_sds, prompt_t_sds,
                   sample_t_sds, sample_t_sds],
        scratch_shapes=[pltpu.VMEM((D_MODEL, SB_WIDTH + 2 * SGU_WIDTH), BF16),
                        pltpu.VMEM((2 * SB_WIDTH, D_MODEL), BF16),
                        pltpu.VMEM((2, WEIGHT_CHUNK_ROWS, w_in.shape[1]), F32),
                        pltpu.SemaphoreType.DMA((2,))],
        compiler_params=_params(1),
        name="inproj",
    )(x, g, w_in, ln_g, ln_b)


def _pair_rows(x, lane_lo):
    zero = jnp.zeros_like(x)
    return jnp.concatenate([jnp.where(lane_lo, x, zero), jnp.where(lane_lo, zero, x)], axis=0)


def _sb_blocks(qqs, kts, vts, carries, suffix_ones, lane_lo, masks, tops=None):
    n_blocks = len(masks)
    tops = tops or [None] * n_blocks
    t = qqs[0].shape[0] // 2
    nt_dims = (((1,), (1,)), ((), ()))

    def head_rows(x, top):
        return x if top is None else jnp.concatenate([x[:top], x[t:t + top]], axis=0)

    def weighted_values(a, vt):
        both = lax.dot_general(a, vt, nt_dims, preferred_element_type=F32)
        rows = a.shape[0] // 2
        return jnp.where(lane_lo, both[:rows], both[rows:])

    zs = []
    for qq, kts_p in zip(qqs, kts):
        zs_p, b = [], 0
        while b < n_blocks:
            if b + 1 < n_blocks and tops[b] is None and tops[b + 1] is None:
                z2 = jnp.dot(qq, jnp.concatenate([kts_p[b], kts_p[b + 1]], axis=1),
                             preferred_element_type=F32)
                zs_p += [z2[:, :KEY_BLOCK], z2[:, KEY_BLOCK:]]
                b += 2
            else:
                zs_p.append(jnp.dot(head_rows(qq, tops[b]), kts_p[b], preferred_element_type=F32))
                b += 1
        zs.append(zs_p)
    useds, splits = [], []
    for zs_p in zs:
        useds.append([])
        splits.append([])
        for i, mask in enumerate(masks):
            z = zs_p[i]
            if mask is not None:
                z = zs_p[i] = jnp.where(mask, z, HIDDEN_LOGIT)
            used = jnp.maximum(z, 0.0) + jnp.log(1.0 + jnp.exp2(-jnp.abs(z))) * LOG2_E
            hi = used.astype(BF16)
            lo = (used - hi.astype(F32)).astype(BF16)
            useds[-1].append(used)
            splits[-1].append(jnp.concatenate([hi, lo], axis=1))
    sums = [[jnp.dot(s, suffix_ones, preferred_element_type=F32) for s in splits_p]
            for splits_p in splits]
    outs, new_carries = [], []
    for zs_p, useds_p, sums_p, vts_p, carry in zip(zs, useds, sums, vts, carries):
        weights, values_t, top_outs = [], [], []
        for z, used, s, vt, top in zip(zs_p, useds_p, sums_p, vts_p, tops):
            a = jnp.exp2(z - used - (s[:, :KEY_BLOCK] + head_rows(carry, top))).astype(BF16)
            row_sum = s[:, KEY_BLOCK:]
            if top is None:
                weights.append(a)
                values_t.append(vt)
                carry = carry + row_sum
            else:
                top_outs.append(weighted_values(a, vt))
                carry = jnp.concatenate([carry[:top] + row_sum[:top], carry[top:t],
                                         carry[t:t + top] + row_sum[top:], carry[t + top:]], axis=0)
        out = weighted_values(jnp.concatenate(weights, axis=1), jnp.concatenate(values_t, axis=1))
        for o_top in top_outs:
            top = o_top.shape[0]
            out = jnp.concatenate([out[:top] + o_top, out[top:]], axis=0)
        outs.append(out)
        new_carries.append(carry)
    return outs, new_carries


def _suffix_ones():
    r = lax.broadcasted_iota(jnp.int32, (2 * KEY_BLOCK, 2 * KEY_BLOCK), 0) % KEY_BLOCK
    c = lax.broadcasted_iota(jnp.int32, (2 * KEY_BLOCK, 2 * KEY_BLOCK), 1)
    return jnp.where((c >= KEY_BLOCK) | (r > c), 1.0, 0.0).astype(BF16)


_PAIR_SLICES = [slice(p * LANES, (p + 1) * LANES) for p in range(N_PAIRS)]


class _QueryBlock:
    def __init__(self, q_ref, o_ref, carry_ref, n_past, diag_tiles, joint_tiles, walk_tiles):
        self.q_ref, self.o_ref, self.carry_ref, self.n_past = q_ref, o_ref, carry_ref, n_past
        self.diag_tiles, self.joint_tiles, self.walk_tiles = diag_tiles, joint_tiles, walk_tiles


class _StickBreaking:
    def __init__(self, tq, diag_key_offset):
        self.tq = tq
        self.lane_lo = lax.broadcasted_iota(jnp.int32, (1, LANES), 1) < HEAD_DIM
        self.suffix_ones = _suffix_ones()
        self.row = lax.broadcasted_iota(jnp.int32, (2 * tq, KEY_BLOCK), 0) % tq
        col = lax.broadcasted_iota(jnp.int32, (2 * tq, KEY_BLOCK), 1) - diag_key_offset
        self.causal = (col >= 0) & (col < self.row)

    def _qq_tiles(self, blk):
        return [_pair_rows(blk.q_ref[:, cols], self.lane_lo) for cols in _PAIR_SLICES]

    def _store(self, blk, outs, carries, first):
        for p, cols in enumerate(_PAIR_SLICES):
            if first:
                blk.o_ref[:, cols] = outs[p]
            else:
                blk.o_ref[:, cols] += outs[p]
            blk.carry_ref[p] = carries[p]

    def first_step(self, blks, n_joint):
        qqs, kts, vts = [], [], []
        for blk in blks:
            kd, vd = blk.diag_tiles()
            kts_b, vts_b = [[t] for t in kd], [[t] for t in vd]
            for b in range(n_joint):
                kj, vj = blk.joint_tiles(blk.n_past - 1 - b)
                for p in range(N_PAIRS):
                    kts_b[p].append(kj[p])
                    vts_b[p].append(vj[p])
            qqs += self._qq_tiles(blk)
            kts += kts_b
            vts += vts_b
        tops = [None] * n_joint + ([TOP_ROWS] if n_joint > 1 else [None])
        outs, carries = _sb_blocks(qqs, kts, vts, [jnp.zeros((2 * self.tq, LANES), F32)] * len(qqs),
                                   self.suffix_ones, self.lane_lo, [self.causal] + [None] * n_joint, tops)
        for i, blk in enumerate(blks):
            pairs = slice(i * N_PAIRS, (i + 1) * N_PAIRS)
            self._store(blk, outs[pairs], carries[pairs], True)

    def _single_block(self, blk, tiles, mask):
        kts, vts = tiles
        outs, carries = _sb_blocks(self._qq_tiles(blk), [[t] for t in kts], [[t] for t in vts],
                                   [blk.carry_ref[p] for p in range(N_PAIRS)],
                                   self.suffix_ones, self.lane_lo, [mask])
        self._store(blk, outs, carries, False)

    def _least_used(self, blk, rows=None):
        tq = self.tq

        def pick(x):
            return x if rows is None else jnp.concatenate([x[rows], x[tq + rows.start:tq + rows.stop]], axis=0)
        return jnp.min(functools.reduce(jnp.minimum, [pick(blk.carry_ref[p]) for p in range(N_PAIRS)]))

    def finish_partial_block(self, blk):
        @pl.when(self._least_used(blk, slice(TOP_ROWS, self.tq)) < USED_STICK_CUTOFF)
        def _():
            self._single_block(blk, blk.joint_tiles(blk.n_past - JOINT_PAST_BLOCKS), self.row >= TOP_ROWS)

    def walk(self, blk, next_block):
        def body(state):
            j, _ = state
            self._single_block(blk, blk.walk_tiles(j), None)
            return j - 1, self._least_used(blk)

        lax.while_loop(lambda s: (s[0] >= 0) & (s[1] < USED_STICK_CUTOFF), body,
                       (next_block, self._least_used(blk)))


def _sb_prompt_kernel(q_ref, kt_ref, vt_ref, o_ref, carry_ref):
    def tiles(j):
        return ([kt_ref[j, rows, :] for rows in _PAIR_SLICES],
                [vt_ref[j, rows, :] for rows in _PAIR_SLICES])

    step = pl.program_id(1)
    sb = _StickBreaking(KEY_BLOCK, 0)
    blks = []
    for r in range(Q_BLOCKS_PER_STEP):
        i = step * Q_BLOCKS_PER_STEP + r
        rows = pl.ds(r * KEY_BLOCK, KEY_BLOCK)
        blks.append(_QueryBlock(q_ref.at[rows, :], o_ref.at[rows, :], carry_ref.at[r], i,
                                functools.partial(tiles, i), tiles, tiles))

    assert Q_BLOCKS_PER_STEP >= JOINT_PAST_BLOCKS > 1

    @pl.when(step > 0)
    def _():
        for first in range(0, Q_BLOCKS_PER_STEP, Q_BLOCKS_PER_PASS):
            sb.first_step(blks[first:first + Q_BLOCKS_PER_PASS], JOINT_PAST_BLOCKS)
        for blk in blks:
            sb.finish_partial_block(blk)

    @pl.when(step == 0)
    def _():
        for r, blk in enumerate(blks):
            sb.first_step([blk], min(r, JOINT_PAST_BLOCKS))
            if r >= JOINT_PAST_BLOCKS:
                sb.finish_partial_block(blk)

    for r, blk in enumerate(blks):
        sb.walk(blk, jnp.where(step > 0, blk.n_past - JOINT_PAST_BLOCKS,
                               r - min(r, JOINT_PAST_BLOCKS)) - 1)


def _sb_prompt(q, kt3, vt3, batch, seq):
    nq = seq // (KEY_BLOCK * Q_BLOCKS_PER_STEP)
    blk = pl.BlockSpec((KEY_BLOCK * Q_BLOCKS_PER_STEP, SB_WIDTH), lambda b, i: (b * nq + i, 0))
    keys = pl.BlockSpec((seq // KEY_BLOCK, SB_WIDTH, KEY_BLOCK), lambda b, i: (b, 0, 0))
    return pl.pallas_call(
        _sb_prompt_kernel,
        grid=(batch, nq),
        in_specs=[blk, keys, keys],
        out_specs=blk,
        out_shape=jax.ShapeDtypeStruct((batch * seq, SB_WIDTH), F32),
        scratch_shapes=[pltpu.VMEM((Q_BLOCKS_PER_STEP, N_PAIRS, 2 * KEY_BLOCK, LANES), F32)],
        compiler_params=_params(2),
        name="sb_prompt",
    )(q, kt3, vt3)


def _sb_sample_kernel(q_ref, kd_ref, vd_ref, kr_ref, vr_ref, kc_hbm, vc_hbm, o_ref,
                      carry_ref, kbuf_ref, vbuf_ref, sem_ref, *, past_len):
    tq = q_ref.shape[0]
    n_past = past_len // KEY_BLOCK
    n_recent = kr_ref.shape[2] // KEY_BLOCK
    b = pl.program_id(0)

    def diag_tiles():
        return ([kd_ref[0, rows, :] for rows in _PAIR_SLICES],
                [vd_ref[0, rows, :] for rows in _PAIR_SLICES])

    def recent_tiles(j):
        first = (j - (n_past - n_recent)) * KEY_BLOCK
        keys = slice(first, first + KEY_BLOCK)
        return ([kr_ref[0, rows, keys].astype(BF16) for rows in _PAIR_SLICES],
                [vr_ref[0, rows, keys].astype(BF16) for rows in _PAIR_SLICES])

    def cache_tiles(j):
        keys = pl.ds(pl.multiple_of(j * KEY_BLOCK, KEY_BLOCK), KEY_BLOCK)
        copies = [pltpu.make_async_copy(hbm.at[b, :, keys], buf, sem_ref.at[s])
                  for s, (hbm, buf) in enumerate(((kc_hbm, kbuf_ref), (vc_hbm, vbuf_ref)))]
        for c in copies:
            c.start()
        for c in copies:
            c.wait()
        return ([kbuf_ref[rows, :].astype(BF16) for rows in _PAIR_SLICES],
                [vbuf_ref[rows, :].astype(BF16) for rows in _PAIR_SLICES])

    assert n_recent == JOINT_PAST_BLOCKS > 1
    sb = _StickBreaking(tq, (b % (KEY_BLOCK // tq)) * tq)
    blk = _QueryBlock(q_ref, o_ref, carry_ref, n_past, diag_tiles, recent_tiles, cache_tiles)
    sb.first_step([blk], JOINT_PAST_BLOCKS)
    sb.finish_partial_block(blk)
    sb.walk(blk, n_past - JOINT_PAST_BLOCKS - 1)


def _sb_sample(q, kt3, vt3, cache_kt, cache_vt, t_new, first_row):
    batch, _, past_len = cache_kt.shape
    n_recent = JOINT_PAST_BLOCKS * KEY_BLOCK
    seqs_per_block = KEY_BLOCK // t_new
    new = pl.BlockSpec((1, SB_WIDTH, KEY_BLOCK),
                       lambda b: (first_row // KEY_BLOCK + b // seqs_per_block, 0, 0))
    recent = pl.BlockSpec((1, SB_WIDTH, n_recent), lambda b: (b, 0, past_len // n_recent - 1))
    hbm = pl.BlockSpec(memory_space=pl.ANY)
    return pl.pallas_call(
        functools.partial(_sb_sample_kernel, past_len=past_len),
        grid=(batch,),
        in_specs=[pl.BlockSpec((t_new, SB_WIDTH), lambda b: (first_row // t_new + b, 0)),
                  new, new, recent, recent, hbm, hbm],
        out_specs=pl.BlockSpec((t_new, SB_WIDTH), lambda b: (b, 0)),
        out_shape=jax.ShapeDtypeStruct((batch * t_new, SB_WIDTH), F32),
        scratch_shapes=[pltpu.VMEM((N_PAIRS, 2 * t_new, LANES), F32),
                        pltpu.VMEM((SB_WIDTH, KEY_BLOCK), F32),
                        pltpu.VMEM((SB_WIDTH, KEY_BLOCK), F32),
                        pltpu.SemaphoreType.DMA((2,))],
        compiler_params=_params(1),
        name="sb_sample",
    )(q, kt3, vt3, cache_kt, cache_vt, cache_kt, cache_vt)


def _mix_rows(x, o_sb, u_ref, gn_ref, ws_ref, bias_ref, gsb_ref, gsgu_ref, wout_ref, post_ref,
              sgu_ref, rows, scratch_rows):
    lane_lo = lax.broadcasted_iota(jnp.int32, (1, LANES), 1) < HEAD_DIM
    for p in range(N_PAIRS):
        cols = slice(p * LANES, (p + 1) * LANES)
        for c in range((rows.stop - rows.start) // SGU_CHUNK):
            chunk = slice(rows.start + c * SGU_CHUNK, rows.start + (c + 1) * SGU_CHUNK)
            out = slice(scratch_rows.start + c * SGU_CHUNK, scratch_rows.start + (c + 1) * SGU_CHUNK)
            gg = _pair_rows(gn_ref[chunk, cols].astype(BF16), lane_lo)
            mixed = jnp.dot(ws_ref[0, p], gg, preferred_element_type=F32) + bias_ref[0, :, cols]
            sgu_ref[out, cols] = u_ref[chunk, cols] * mixed
    merged = jnp.concatenate([_rms(o_sb, gsb_ref[...]).astype(BF16),
                              _rms(sgu_ref[scratch_rows, :], gsgu_ref[...]).astype(BF16)], axis=1)
    y = jnp.dot(merged, wout_ref[...], preferred_element_type=F32)
    return x + _rms(y, post_ref[...])


def _mix_ffn_kernel(x_ref, op_ref, os_ref, u_ref, gn_ref, ws_ref, bias_ref,
                    xn_ref, opn_ref, un_ref, gnn_ref, wsp_ref, biasp_ref,
                    gsb_ref, gsgu_ref, wout_ref, mix_post_ref,
                    pre_ref, post_ref, wg_hbm, wu_hbm, wd_hbm,
                    yp_ref, ys_ref, sgu_ref,
                    wg_ref, wu_ref, wd_ref, stage_g_ref, stage_u_ref, stage_d_ref, sem_ref,
                    act_ref, h_ref, xa_ref):
    _load_ffn_weights(wg_hbm, wu_hbm, wd_hbm, wg_ref, wu_ref, wd_ref,
                      (stage_g_ref, stage_u_ref, stage_d_ref), sem_ref)
    ffn = _RotatedFfn(pre_ref, post_ref, wg_ref, wu_ref, wd_ref, act_ref, h_ref)
    shared = (gsb_ref, gsgu_ref, wout_ref, mix_post_ref, sgu_ref)

    def mix(rows):
        o_sb = jnp.where(_is_sample_step(), os_ref[rows, :], op_ref[rows, :])
        return _mix_rows(x_ref[rows, :], o_sb, u_ref, gn_ref, ws_ref, bias_ref, *shared, rows, rows)

    def mix_next_first():
        local = slice(0, ffn.first.stop - ffn.first.start)
        return _mix_rows(xn_ref[...], opn_ref[...], un_ref, gnn_ref, wsp_ref, biasp_ref,
                         *shared, local, ffn.first)

    def prepare_first(x):
        xa_ref[...] = x
        ffn.prepare_first(x)

    pl.when(pl.program_id(0) == 0)(lambda: prepare_first(mix(ffn.first)))
    yp_ref[ffn.first, :], yp_ref[ffn.second, :] = ffn.run(
        lambda: xa_ref[...], lambda: mix(ffn.second), mix_next_first, prepare_first)

    @pl.when(_is_sample_step())
    def _():
        ys_ref[...] = yp_ref[...]


def _mix_ffn(x, o_prompt, o_sample, u, gn, ws_pairs, bias, g_sb, g_sgu, w_out, mix_post_g,
             pre_g, post_g, wg, wu, wd, n_prompt_tiles):
    row = _merged_tile_spec(D_MODEL, n_prompt_tiles)
    half = _merged_tile_spec(SB_WIDTH, n_prompt_tiles)
    which = lambda s: jnp.where(s == 0, 1, 0)
    sds = lambda rows: jax.ShapeDtypeStruct((rows, D_MODEL), F32)
    next_row = _next_first_group_spec(D_MODEL, n_prompt_tiles)
    next_half = _next_first_group_spec(SB_WIDTH, n_prompt_tiles)
    return pl.pallas_call(
        _mix_ffn_kernel,
        grid=(n_prompt_tiles + 1,),
        in_specs=[row, _prompt_tile_spec(SB_WIDTH),
                  pl.BlockSpec((ROW_TILE, SB_WIDTH), lambda s: (0, 0), pipeline_mode=pl.Buffered(1)),
                  half, half,
                  pl.BlockSpec((1,) + ws_pairs.shape[1:], lambda s: (which(s), 0, 0, 0)),
                  pl.BlockSpec((1,) + bias.shape[1:], lambda s: (which(s), 0, 0)),
                  next_row, next_half, next_half, next_half,
                  pl.BlockSpec((1,) + ws_pairs.shape[1:], lambda s: (0, 0, 0, 0),
                               pipeline_mode=pl.Buffered(1)),
                  pl.BlockSpec((1,) + bias.shape[1:], lambda s: (0, 0, 0), pipeline_mode=pl.Buffered(1)),
                  _const_spec((1, SB_WIDTH)), _const_spec((1, SGU_WIDTH)),
                  _const_spec((D_MODEL, D_MODEL)), _const_spec((1, D_MODEL))] + _FFN_WEIGHT_SPECS,
        out_specs=[_prompt_tile_spec(D_MODEL), _sample_tile_spec(D_MODEL)],
        out_shape=[sds(n_prompt_tiles * ROW_TILE), sds(ROW_TILE)],
        scratch_shapes=[pltpu.VMEM((ROW_TILE, SGU_WIDTH), F32)] + _FFN_SCRATCH + [
            pltpu.VMEM((FFN_ROW_GROUPS[0].stop, D_MODEL), F32)],
        compiler_params=_params(1),
        name="mix_ffn",
    )(x, o_prompt, o_sample, u, gn, ws_pairs, bias, x, o_prompt, u, gn, ws_pairs, bias,
      g_sb, g_sgu, w_out, mix_post_g, pre_g, post_g, wg, wu, wd)


def _sgu_operands(w_s, b_s, t_new):
    i = jnp.arange(SGU_CHUNK)
    mask = (i[None, :] // SGU_CAUSAL_CHUNK) <= (i[:, None] // SGU_CAUSAL_CHUNK)
    w_prompt = w_s * mask[None].astype(w_s.dtype)
    reps = SGU_CHUNK // t_new
    corner = w_prompt[:, :t_new, :t_new]
    eye = jnp.eye(reps, dtype=w_s.dtype)
    w_sample = jnp.einsum('ab,gij->gaibj', eye, corner).reshape(w_s.shape)
    groups = w_s.shape[0]

    def pairs(w):
        w = w.reshape(groups // 2, 2, SGU_CHUNK, SGU_CHUNK).transpose(0, 2, 1, 3)
        return w.reshape(groups // 2, SGU_CHUNK, 2 * SGU_CHUNK)

    def bias(b):
        return jnp.repeat(b.T, SGU_WIDTH // groups, axis=1)

    ws = jnp.stack([pairs(w_prompt), pairs(w_sample)]).astype(BF16)
    return ws, jnp.stack([bias(b_s), bias(jnp.tile(b_s[:, :t_new], (1, reps)))])


def kernel(x_prompt, x_sample, cache_k_sb, cache_v_sb, ffn1_pre_g, ffn1_post_g, ffn1_w_gate, ffn1_w_up, ffn1_w_down, mix_pre_g, mix_post_g, w_in, sgu_ln_g, sgu_ln_b, sgu_w_s, sgu_b_s, g_out_sb, g_out_sgu, w_out, ffn2_pre_g, ffn2_post_g, ffn2_w_gate, ffn2_w_up, ffn2_w_down):
    depth, batch, seq = w_in.shape[0], x_prompt.shape[0], x_prompt.shape[1]
    dec_batch, t_new, past_len = x_sample.shape[0], x_sample.shape[1], cache_k_sb.shape[2]
    assert depth == 1 and dec_batch * t_new == ROW_TILE and seq % ROW_TILE == 0
    assert SGU_CHUNK % t_new == 0 and KEY_BLOCK % t_new == 0
    assert past_len % (JOINT_PAST_BLOCKS * KEY_BLOCK) == 0
    n_prompt_tiles = batch * seq // ROW_TILE

    mat = lambda w: w[0].astype(BF16)
    vec = lambda g: g[0][None, :]

    x = _ffn(x_prompt.reshape(batch * seq, D_MODEL), x_sample.reshape(ROW_TILE, D_MODEL),
             vec(ffn1_pre_g), vec(ffn1_post_g), ffn1_w_gate[0], ffn1_w_up[0], ffn1_w_down[0],
             n_prompt_tiles)
    q, u, gn, kt3, vt3, kt_p, vt_p, kt_s, vt_s = _inproj(
        x, vec(mix_pre_g), w_in[0], vec(sgu_ln_g), vec(sgu_ln_b), n_prompt_tiles, seq // ROW_TILE)

    o_prompt = _sb_prompt(q, kt3, vt3, batch, seq)
    cache_t = lambda c: c[0].transpose(0, 2, 3, 1).reshape(dec_batch, SB_WIDTH, past_len)
    o_sample = _sb_sample(q, kt3, vt3, cache_t(cache_k_sb), cache_t(cache_v_sb), t_new, batch * seq)

    ws_pairs, bias = _sgu_operands(sgu_w_s[0], sgu_b_s[0], t_new)
    y_p, y_s = _mix_ffn(x, o_prompt, o_sample, u, gn, ws_pairs, bias, vec(g_out_sb), vec(g_out_sgu),
                        mat(w_out), vec(mix_post_g), vec(ffn2_pre_g), vec(ffn2_post_g),
                        ffn2_w_gate[0], ffn2_w_up[0], ffn2_w_down[0], n_prompt_tiles)

    def prompt_heads(t):
        return t.reshape(batch, N_HEADS, HEAD_DIM, seq).transpose(0, 3, 1, 2)[None]

    def sample_heads(t):
        return t.reshape(N_HEADS, HEAD_DIM, dec_batch, t_new).transpose(2, 3, 0, 1)[None]

    g_s = gn[batch * seq:].reshape(1, dec_batch, t_new, N_HEADS, HEAD_DIM)
    return (y_p.reshape(batch, seq, D_MODEL), y_s.reshape(dec_batch, t_new, D_MODEL),
            prompt_heads(kt_p), prompt_heads(vt_p), sample_heads(kt_s), sample_heads(vt_s), g_s)
```

```python
import functools

import jax
import jax.numpy as jnp
from jax import lax
from jax.experimental import pallas as pl
from jax.experimental.pallas import tpu as pltpu

D_MODEL = 1024
D_FF = 2816
SB_WIDTH = 512
SGU_WIDTH = 512
HEAD_DIM = 64
N_HEADS = SB_WIDTH // HEAD_DIM
SGU_CHUNK = 128
SGU_CAUSAL_CHUNK = 64
FFN_RES = 0.5
EPS = 1e-6

LANES = 128
N_PAIRS = SB_WIDTH // LANES
KEY_BLOCK = 128
JOINT_PAST_BLOCKS = 2
TOP_ROWS = 32
Q_BLOCKS_PER_STEP = 4
Q_BLOCKS_PER_PASS = 2
FF_CHUNKS = (1536, 1280)
ROW_TILE = 512
WEIGHT_CHUNK_ROWS = 128
FFN_ROW_GROUPS = (slice(0, 256), slice(256, 512))
VMEM_LIMIT = 62 * 1024 * 1024

LOG2_E = 1.4426950408889634
USED_STICK_CUTOFF = 105.0 * LOG2_E
HIDDEN_LOGIT = -1e30

F32 = jnp.float32
BF16 = jnp.bfloat16


def _rms(x, g):
    return x * lax.rsqrt(jnp.mean(x * x, axis=-1, keepdims=True) + EPS) * g


def _gelu(x):
    return 0.5 * x * (1.0 + lax.erf(x * (0.5 ** 0.5)))


def _const_spec(shape):
    return pl.BlockSpec(shape, lambda *_: (0,) * len(shape), pipeline_mode=pl.Buffered(1))


def _params(n_axes):
    return pltpu.CompilerParams(dimension_semantics=("arbitrary",) * n_axes,
                                vmem_limit_bytes=VMEM_LIMIT)


def _is_sample_step():
    return pl.program_id(0) == 0


def _merged_tile_spec(width, n_prompt_tiles):
    return pl.BlockSpec((ROW_TILE, width), lambda s: ((s + n_prompt_tiles) % (n_prompt_tiles + 1), 0))


def _prompt_tile_spec(width, rows=ROW_TILE):
    return pl.BlockSpec((rows, width), lambda s: (jnp.maximum(s - 1, 0), 0))


def _sample_tile_spec(width, rows=ROW_TILE):
    return pl.BlockSpec((rows, width), lambda s: (0, 0))


def _ffn_hidden(x, pre_ref):
    return _rms(x, pre_ref[...]).astype(BF16)


def _ffn_chunk(h, wg_ref, wu_ref, act_ref, rows, c):
    cols = slice(sum(FF_CHUNKS[:c]), sum(FF_CHUNKS[:c + 1]))
    gate = jnp.dot(h, wg_ref[:, cols], preferred_element_type=F32)
    up = jnp.dot(h, wu_ref[:, cols], preferred_element_type=F32)
    act_ref[rows, cols] = (gate * jax.nn.sigmoid(gate) * up).astype(BF16)


def _ffn_down(act_ref, wd_ref, rows):
    return jnp.dot(act_ref[rows, :], wd_ref[...], preferred_element_type=F32)


def _ffn_out(x, f, post_ref):
    return x + FFN_RES * _rms(f, post_ref[...])


class _RotatedFfn:
    def __init__(self, pre_ref, post_ref, wg_ref, wu_ref, wd_ref, act_ref, h_ref):
        self.pre_ref, self.post_ref = pre_ref, post_ref
        self.wg_ref, self.wu_ref, self.wd_ref = wg_ref, wu_ref, wd_ref
        self.act_ref, self.h_ref = act_ref, h_ref
        self.first, self.second = FFN_ROW_GROUPS

    def _chunk(self, h, rows, c):
        _ffn_chunk(h, self.wg_ref, self.wu_ref, self.act_ref, rows, c)

    def prepare_first(self, x):
        self.h_ref[...] = _ffn_hidden(x, self.pre_ref)
        self._chunk(self.h_ref[...], self.first, 0)

    def run(self, first_x, second_x, next_first_x, prepare_first=None):
        prepare_first = prepare_first or self.prepare_first
        x_second = second_x()
        for c in range(1, len(FF_CHUNKS)):
            self._chunk(self.h_ref[...], self.first, c)
        h_second = _ffn_hidden(x_second, self.pre_ref)
        for c in range(len(FF_CHUNKS)):
            self._chunk(h_second, self.second, c)
        f_first = _ffn_down(self.act_ref, self.wd_ref, self.first)
        x_next = next_first_x()
        f_second = _ffn_down(self.act_ref, self.wd_ref, self.second)
        y_first = _ffn_out(first_x(), f_first, self.post_ref)
        prepare_first(x_next)
        return y_first, _ffn_out(x_second, f_second, self.post_ref)


_HBM_SPEC = pl.BlockSpec(memory_space=pl.ANY)
_FFN_WEIGHT_SPECS = [_const_spec((1, D_MODEL)), _const_spec((1, D_MODEL)), _HBM_SPEC, _HBM_SPEC, _HBM_SPEC]
_FFN_SCRATCH = [pltpu.VMEM((D_MODEL, D_FF), BF16), pltpu.VMEM((D_MODEL, D_FF), BF16),
                pltpu.VMEM((D_FF, D_MODEL), BF16),
                pltpu.VMEM((2, WEIGHT_CHUNK_ROWS, D_FF), F32),
                pltpu.VMEM((2, WEIGHT_CHUNK_ROWS, D_FF), F32),
                pltpu.VMEM((2, WEIGHT_CHUNK_ROWS, D_MODEL), F32),
                pltpu.SemaphoreType.DMA((6,)),
                pltpu.VMEM((ROW_TILE, D_FF), BF16),
                pltpu.VMEM((FFN_ROW_GROUPS[0].stop, D_MODEL), BF16)]


def _load_bf16(streams):
    def n_chunks(stream):
        return stream[0].shape[0] // stream[2].shape[1]

    def copy(stream, i):
        hbm_ref, _, stage_ref, sem_ref, first_sem = stream
        chunk = stage_ref.shape[1]
        return pltpu.make_async_copy(hbm_ref.at[pl.ds(i * chunk, chunk), :], stage_ref.at[i % 2],
                                     sem_ref.at[first_sem + i % 2])

    for stream in streams:
        copy(stream, 0).start()
    order = sorted(((i + 1) / n_chunks(stream), k, i)
                   for k, stream in enumerate(streams) for i in range(n_chunks(stream)))
    for _, k, i in order:
        stream = streams[k]
        if i + 1 < n_chunks(stream):
            copy(stream, i + 1).start()
        copy(stream, i).wait()
        chunk = stream[2].shape[1]
        stream[1][pl.ds(i * chunk, chunk), :] = stream[2][i % 2].astype(BF16)


def _load_ffn_weights(wg_hbm, wu_hbm, wd_hbm, wg_ref, wu_ref, wd_ref, stage_refs, sem_ref):
    @pl.when(pl.program_id(0) == 0)
    def _():
        _load_bf16([(hbm, dst, stage, sem_ref, 2 * k) for k, (hbm, dst, stage) in enumerate(
            zip((wg_hbm, wu_hbm, wd_hbm), (wg_ref, wu_ref, wd_ref), stage_refs))])


def _ffn_kernel(xp_ref, xs_ref, xnext_ref, pre_ref, post_ref, wg_hbm, wu_hbm, wd_hbm, o_ref,
                wg_ref, wu_ref, wd_ref, stage_g_ref, stage_u_ref, stage_d_ref, sem_ref, act_ref, h_ref):
    _load_ffn_weights(wg_hbm, wu_hbm, wd_hbm, wg_ref, wu_ref, wd_ref,
                      (stage_g_ref, stage_u_ref, stage_d_ref), sem_ref)
    ffn = _RotatedFfn(pre_ref, post_ref, wg_ref, wu_ref, wd_ref, act_ref, h_ref)

    def x_rows(rows):
        return jnp.where(_is_sample_step(), xs_ref[rows, :], xp_ref[rows, :])

    @pl.when(pl.program_id(0) == 0)
    def _():
        ffn.prepare_first(x_rows(ffn.first))

    o_ref[ffn.first, :], o_ref[ffn.second, :] = ffn.run(
        lambda: x_rows(ffn.first), lambda: x_rows(ffn.second), lambda: xnext_ref[...])


def _next_first_group_spec(width, n_tiles):
    rows = FFN_ROW_GROUPS[0].stop
    return pl.BlockSpec((rows, width), lambda s: (jnp.minimum(s, n_tiles - 1) * (ROW_TILE // rows), 0))


def _ffn(x_prompt, x_sample, pre_g, post_g, wg, wu, wd, n_prompt_tiles):
    n_tiles = n_prompt_tiles + 1
    return pl.pallas_call(
        _ffn_kernel,
        grid=(n_tiles,),
        in_specs=[_prompt_tile_spec(D_MODEL), _sample_tile_spec(D_MODEL),
                  _next_first_group_spec(D_MODEL, n_prompt_tiles)] + _FFN_WEIGHT_SPECS,
        out_specs=_merged_tile_spec(D_MODEL, n_prompt_tiles),
        out_shape=jax.ShapeDtypeStruct((n_tiles * ROW_TILE, D_MODEL), F32),
        scratch_shapes=_FFN_SCRATCH,
        compiler_params=_params(1),
        name="ffn",
    )(x_prompt, x_sample, x_prompt, pre_g, post_g, wg, wu, wd)


def _inproj_kernel(x_ref, g_ref, win_hbm, lng_ref, lnb_ref,
                   q_ref, u_ref, gn_ref, kt3_ref, vt3_ref, ktp_ref, vtp_ref, kts_ref, vts_ref,
                   wqug_ref, wkvt_ref, stage_ref, sem_ref):
    w = SB_WIDTH

    @pl.when(pl.program_id(0) == 0)
    def _():
        chunk = stage_ref.shape[1]
        n_chunks = win_hbm.shape[0] // chunk

        def copy(i):
            return pltpu.make_async_copy(win_hbm.at[pl.ds(i * chunk, chunk), :], stage_ref.at[i % 2],
                                         sem_ref.at[i % 2])

        copy(0).start()
        for i in range(n_chunks):
            if i + 1 < n_chunks:
                copy(i + 1).start()
            copy(i).wait()
            rows = pl.ds(i * chunk, chunk)
            part = stage_ref[i % 2]
            wqug_ref[rows, 0:w] = part[:, 0:w].astype(BF16)
            wqug_ref[rows, w:] = part[:, 3 * w:].astype(BF16)
            wkvt_ref[:, rows] = part[:, w:3 * w].T.astype(BF16)

    h = _rms(x_ref[...], g_ref[...]).astype(BF16)
    z = jnp.dot(h, wqug_ref[...], preferred_element_type=F32)
    kvt = lax.dot_general(wkvt_ref[...], h, (((1,), (1,)), ((), ())), preferred_element_type=F32)
    q_ref[...] = (z[:, 0:w] * (HEAD_DIM ** -0.5 * LOG2_E)).astype(BF16)
    u_ref[...] = _gelu(z[:, w:w + SGU_WIDTH])
    ge = _gelu(z[:, w + SGU_WIDTH:])
    xc = ge - jnp.mean(ge, axis=-1, keepdims=True)
    y = xc * lax.rsqrt(jnp.mean(xc * xc, axis=-1, keepdims=True) + EPS)
    gn_ref[...] = y * lng_ref[...] + lnb_ref[...]
    for t_ref, t3_ref, rows in ((ktp_ref, kt3_ref, slice(0, w)), (vtp_ref, vt3_ref, slice(w, 2 * w))):
        t = kvt[rows]
        t_ref[...] = t
        for c in range(ROW_TILE // KEY_BLOCK):
            t3_ref[c] = t[:, c * KEY_BLOCK:(c + 1) * KEY_BLOCK].astype(BF16)

    @pl.when(_is_sample_step())
    def _():
        kts_ref[...] = ktp_ref[...]
        vts_ref[...] = vtp_ref[...]


def _inproj(x, g, w_in, ln_g, ln_b, n_prompt_tiles, prompt_tiles_per_row):
    n = x.shape[0]
    half = _merged_tile_spec(SB_WIDTH, n_prompt_tiles)
    blocks_per_tile = ROW_TILE // KEY_BLOCK
    key_blocks = pl.BlockSpec((blocks_per_tile, SB_WIDTH, KEY_BLOCK),
                              lambda s: ((s + n_prompt_tiles) % (n_prompt_tiles + 1), 0, 0))

    def prompt_t_index(s):
        tile = jnp.maximum(s - 1, 0)
        return tile // prompt_tiles_per_row, tile % prompt_tiles_per_row

    prompt_t = pl.BlockSpec((SB_WIDTH, ROW_TILE), prompt_t_index)
    prompt_t_sds = jax.ShapeDtypeStruct(
        (n_prompt_tiles // prompt_tiles_per_row * SB_WIDTH, prompt_tiles_per_row * ROW_TILE), F32)
    sample_t_sds = jax.ShapeDtypeStruct((SB_WIDTH, ROW_TILE), F32)
    f32_half = jax.ShapeDtypeStruct((n, SB_WIDTH), F32)
    key_blocks_sds = jax.ShapeDtypeStruct((n // KEY_BLOCK, SB_WIDTH, KEY_BLOCK), BF16)
    return pl.pallas_call(
        _inproj_kernel,
        grid=(n // ROW_TILE,),
        in_specs=[_merged_tile_spec(D_MODEL, n_prompt_tiles), _const_spec((1, D_MODEL)),
                  _HBM_SPEC, _const_spec((1, SGU_WIDTH)), _const_spec((1, SGU_WIDTH))],
        out_specs=[half] * 3 + [key_blocks] * 2 + [prompt_t] * 2 + [_sample_tile_spec(ROW_TILE, SB_WIDTH)] * 2,
        out_shape=[jax.ShapeDtypeStruct((n, SB_WIDTH), BF16), f32_half, f32_half,
                   key_blocks_sds, key_blocks_sds, prompt_t_sds, prompt_t_sds,
                   sample_t_sds, sample_t_sds],
        scratch_shapes=[pltpu.VMEM((D_MODEL, SB_WIDTH + 2 * SGU_WIDTH), BF16),
                        pltpu.VMEM((2 * SB_WIDTH, D_MODEL), BF16),
                        pltpu.VMEM((2, WEIGHT_CHUNK_ROWS, w_in.shape[1]), F32),
                        pltpu.SemaphoreType.DMA((2,))],
        compiler_params=_params(1),
        name="inproj",
    )(x, g, w_in, ln_g, ln_b)


def _pair_rows(x, lane_lo):
    zero = jnp.zeros_like(x)
    return jnp.concatenate([jnp.where(lane_lo, x, zero), jnp.where(lane_lo, zero, x)], axis=0)


def _sb_blocks(qqs, kts, vts, carries, suffix_ones, lane_lo, masks, tops=None):
    n_blocks = len(masks)
    tops = tops or [None] * n_blocks
    t = qqs[0].shape[0] // 2
    nt_dims = (((1,), (1,)), ((), ()))

    def head_rows(x, top):
        return x if top is None else jnp.concatenate([x[:top], x[t:t + top]], axis=0)

    def weighted_values(a, vt):
        both = lax.dot_general(a, vt, nt_dims, preferred_element_type=F32)
        rows = a.shape[0] // 2
        return jnp.where(lane_lo, both[:rows], both[rows:])

    zs = []
    for qq, kts_p in zip(qqs, kts):
        zs_p, b = [], 0
        while b < n_blocks:
            if b + 1 < n_blocks and tops[b] is None and tops[b + 1] is None:
                z2 = jnp.dot(qq, jnp.concatenate([kts_p[b], kts_p[b + 1]], axis=1),
                             preferred_element_type=F32)
                zs_p += [z2[:, :KEY_BLOCK], z2[:, KEY_BLOCK:]]
                b += 2
            else:
                zs_p.append(jnp.dot(head_rows(qq, tops[b]), kts_p[b], preferred_element_type=F32))
                b += 1
        zs.append(zs_p)
    useds, splits = [], []
    for zs_p in zs:
        useds.append([])
        splits.append([])
        for i, mask in enumerate(masks):
            z = zs_p[i]
            if mask is not None:
                z = zs_p[i] = jnp.where(mask, z, HIDDEN_LOGIT)
            used = jnp.maximum(z, 0.0) + jnp.log(1.0 + jnp.exp2(-jnp.abs(z))) * LOG2_E
            hi = used.astype(BF16)
            lo = (used - hi.astype(F32)).astype(BF16)
            useds[-1].append(used)
            splits[-1].append(jnp.concatenate([hi, lo], axis=1))
    sums = [[jnp.dot(s, suffix_ones, preferred_element_type=F32) for s in splits_p]
            for splits_p in splits]
    outs, new_carries = [], []
    for zs_p, useds_p, sums_p, vts_p, carry in zip(zs, useds, sums, vts, carries):
        weights, values_t, top_outs = [], [], []
        for z, used, s, vt, top in zip(zs_p, useds_p, sums_p, vts_p, tops):
            a = jnp.exp2(z - used - (s[:, :KEY_BLOCK] + head_rows(carry, top))).astype(BF16)
            row_sum = s[:, KEY_BLOCK:]
            if top is None:
                weights.append(a)
                values_t.append(vt)
                carry = carry + row_sum
            else:
                top_outs.append(weighted_values(a, vt))
                carry = jnp.concatenate([carry[:top] + row_sum[:top], carry[top:t],
                                         carry[t:t + top] + row_sum[top:], carry[t + top:]], axis=0)
        out = weighted_values(jnp.concatenate(weights, axis=1), jnp.concatenate(values_t, axis=1))
        for o_top in top_outs:
            top = o_top.shape[0]
            out = jnp.concatenate([out[:top] + o_top, out[top:]], axis=0)
        outs.append(out)
        new_carries.append(carry)
    return outs, new_carries


def _suffix_ones():
    r = lax.broadcasted_iota(jnp.int32, (2 * KEY_BLOCK, 2 * KEY_BLOCK), 0) % KEY_BLOCK
    c = lax.broadcasted_iota(jnp.int32, (2 * KEY_BLOCK, 2 * KEY_BLOCK), 1)
    return jnp.where((c >= KEY_BLOCK) | (r > c), 1.0, 0.0).astype(BF16)


_PAIR_SLICES = [slice(p * LANES, (p + 1) * LANES) for p in range(N_PAIRS)]


class _QueryBlock:
    def __init__(self, q_ref, o_ref, carry_ref, n_past, diag_tiles, joint_tiles, walk_tiles):
        self.q_ref, self.o_ref, self.carry_ref, self.n_past = q_ref, o_ref, carry_ref, n_past
        self.diag_tiles, self.joint_tiles, self.walk_tiles = diag_tiles, joint_tiles, walk_tiles


class _StickBreaking:
    def __init__(self, tq, diag_key_offset):
        self.tq = tq
        self.lane_lo = lax.broadcasted_iota(jnp.int32, (1, LANES), 1) < HEAD_DIM
        self.suffix_ones = _suffix_ones()
        self.row = lax.broadcasted_iota(jnp.int32, (2 * tq, KEY_BLOCK), 0) % tq
        col = lax.broadcasted_iota(jnp.int32, (2 * tq, KEY_BLOCK), 1) - diag_key_offset
        self.causal = (col >= 0) & (col < self.row)

    def _qq_tiles(self, blk):
        return [_pair_rows(blk.q_ref[:, cols], self.lane_lo) for cols in _PAIR_SLICES]

    def _store(self, blk, outs, carries, first):
        for p, cols in enumerate(_PAIR_SLICES):
            if first:
                blk.o_ref[:, cols] = outs[p]
            else:
                blk.o_ref[:, cols] += outs[p]
            blk.carry_ref[p] = carries[p]

    def first_step(self, blks, n_joint):
        qqs, kts, vts = [], [], []
        for blk in blks:
            kd, vd = blk.diag_tiles()
            kts_b, vts_b = [[t] for t in kd], [[t] for t in vd]
            for b in range(n_joint):
                kj, vj = blk.joint_tiles(blk.n_past - 1 - b)
                for p in range(N_PAIRS):
                    kts_b[p].append(kj[p])
                    vts_b[p].append(vj[p])
            qqs += self._qq_tiles(blk)
            kts += kts_b
            vts += vts_b
        tops = [None] * n_joint + ([TOP_ROWS] if n_joint > 1 else [None])
        outs, carries = _sb_blocks(qqs, kts, vts, [jnp.zeros((2 * self.tq, LANES), F32)] * len(qqs),
                                   self.suffix_ones, self.lane_lo, [self.causal] + [None] * n_joint, tops)
        for i, blk in enumerate(blks):
            pairs = slice(i * N_PAIRS, (i + 1) * N_PAIRS)
            self._store(blk, outs[pairs], carries[pairs], True)

    def _single_block(self, blk, tiles, mask):
        kts, vts = tiles
        outs, carries = _sb_blocks(self._qq_tiles(blk), [[t] for t in kts], [[t] for t in vts],
                                   [blk.carry_ref[p] for p in range(N_PAIRS)],
                                   self.suffix_ones, self.lane_lo, [mask])
        self._store(blk, outs, carries, False)

    def _least_used(self, blk, rows=None):
        tq = self.tq

        def pick(x):
            return x if rows is None else jnp.concatenate([x[rows], x[tq + rows.start:tq + rows.stop]], axis=0)
        return jnp.min(functools.reduce(jnp.minimum, [pick(blk.carry_ref[p]) for p in range(N_PAIRS)]))

    def finish_partial_block(self, blk):
        @pl.when(self._least_used(blk, slice(TOP_ROWS, self.tq)) < USED_STICK_CUTOFF)
        def _():
            self._single_block(blk, blk.joint_tiles(blk.n_past - JOINT_PAST_BLOCKS), self.row >= TOP_ROWS)

    def walk(self, blk, next_block):
        def body(state):
            j, _ = state
            self._single_block(blk, blk.walk_tiles(j), None)
            return j - 1, self._least_used(blk)

        lax.while_loop(lambda s: (s[0] >= 0) & (s[1] < USED_STICK_CUTOFF), body,
                       (next_block, self._least_used(blk)))


def _sb_prompt_kernel(q_ref, kt_ref, vt_ref, o_ref, carry_ref):
    def tiles(j):
        return ([kt_ref[j, rows, :] for rows in _PAIR_SLICES],
                [vt_ref[j, rows, :] for rows in _PAIR_SLICES])

    step = pl.program_id(1)
    sb = _StickBreaking(KEY_BLOCK, 0)
    blks = []
    for r in range(Q_BLOCKS_PER_STEP):
        i = step * Q_BLOCKS_PER_STEP + r
        rows = pl.ds(r * KEY_BLOCK, KEY_BLOCK)
        blks.append(_QueryBlock(q_ref.at[rows, :], o_ref.at[rows, :], carry_ref.at[r], i,
                                functools.partial(tiles, i), tiles, tiles))

    assert Q_BLOCKS_PER_STEP >= JOINT_PAST_BLOCKS > 1

    @pl.when(step > 0)
    def _():
        for first in range(0, Q_BLOCKS_PER_STEP, Q_BLOCKS_PER_PASS):
            sb.first_step(blks[first:first + Q_BLOCKS_PER_PASS], JOINT_PAST_BLOCKS)
        for blk in blks:
            sb.finish_partial_block(blk)

    @pl.when(step == 0)
    def _():
        for r, blk in enumerate(blks):
            sb.first_step([blk], min(r, JOINT_PAST_BLOCKS))
            if r >= JOINT_PAST_BLOCKS:
                sb.finish_partial_block(blk)

    for r, blk in enumerate(blks):
        sb.walk(blk, jnp.where(step > 0, blk.n_past - JOINT_PAST_BLOCKS,
                               r - min(r, JOINT_PAST_BLOCKS)) - 1)


def _sb_prompt(q, kt3, vt3, batch, seq):
    nq = seq // (KEY_BLOCK * Q_BLOCKS_PER_STEP)
    blk = pl.BlockSpec((KEY_BLOCK * Q_BLOCKS_PER_STEP, SB_WIDTH), lambda b, i: (b * nq + i, 0))
    keys = pl.BlockSpec((seq // KEY_BLOCK, SB_WIDTH, KEY_BLOCK), lambda b, i: (b, 0, 0))
    return pl.pallas_call(
        _sb_prompt_kernel,
        grid=(batch, nq),
        in_specs=[blk, keys, keys],
        out_specs=blk,
        out_shape=jax.ShapeDtypeStruct((batch * seq, SB_WIDTH), F32),
        scratch_shapes=[pltpu.VMEM((Q_BLOCKS_PER_STEP, N_PAIRS, 2 * KEY_BLOCK, LANES), F32)],
        compiler_params=_params(2),
        name="sb_prompt",
    )(q, kt3, vt3)


def _sb_sample_kernel(q_ref, kd_ref, vd_ref, kr_ref, vr_ref, kc_hbm, vc_hbm, o_ref,
                      carry_ref, kbuf_ref, vbuf_ref, sem_ref, *, past_len):
    tq = q_ref.shape[0]
    n_past = past_len // KEY_BLOCK
    n_recent = kr_ref.shape[2] // KEY_BLOCK
    b = pl.program_id(0)

    def diag_tiles():
        return ([kd_ref[0, rows, :] for rows in _PAIR_SLICES],
                [vd_ref[0, rows, :] for rows in _PAIR_SLICES])

    def recent_tiles(j):
        first = (j - (n_past - n_recent)) * KEY_BLOCK
        keys = slice(first, first + KEY_BLOCK)
        return ([kr_ref[0, rows, keys].astype(BF16) for rows in _PAIR_SLICES],
                [vr_ref[0, rows, keys].astype(BF16) for rows in _PAIR_SLICES])

    def cache_tiles(j):
        keys = pl.ds(pl.multiple_of(j * KEY_BLOCK, KEY_BLOCK), KEY_BLOCK)
        copies = [pltpu.make_async_copy(hbm.at[b, :, keys], buf, sem_ref.at[s])
                  for s, (hbm, buf) in enumerate(((kc_hbm, kbuf_ref), (vc_hbm, vbuf_ref)))]
        for c in copies:
            c.start()
        for c in copies:
            c.wait()
        return ([kbuf_ref[rows, :].astype(BF16) for rows in _PAIR_SLICES],
                [vbuf_ref[rows, :].astype(BF16) for rows in _PAIR_SLICES])

    assert n_recent == JOINT_PAST_BLOCKS > 1
    sb = _StickBreaking(tq, (b % (KEY_BLOCK // tq)) * tq)
    blk = _QueryBlock(q_ref, o_ref, carry_ref, n_past, diag_tiles, recent_tiles, cache_tiles)
    sb.first_step([blk], JOINT_PAST_BLOCKS)
    sb.finish_partial_block(blk)
    sb.walk(blk, n_past - JOINT_PAST_BLOCKS - 1)


def _sb_sample(q, kt3, vt3, cache_kt, cache_vt, t_new, first_row):
    batch, _, past_len = cache_kt.shape
    n_recent = JOINT_PAST_BLOCKS * KEY_BLOCK
    seqs_per_block = KEY_BLOCK // t_new
    new = pl.BlockSpec((1, SB_WIDTH, KEY_BLOCK),
                       lambda b: (first_row // KEY_BLOCK + b // seqs_per_block, 0, 0))
    recent = pl.BlockSpec((1, SB_WIDTH, n_recent), lambda b: (b, 0, past_len // n_recent - 1))
    hbm = pl.BlockSpec(memory_space=pl.ANY)
    return pl.pallas_call(
        functools.partial(_sb_sample_kernel, past_len=past_len),
        grid=(batch,),
        in_specs=[pl.BlockSpec((t_new, SB_WIDTH), lambda b: (first_row // t_new + b, 0)),
                  new, new, recent, recent, hbm, hbm],
        out_specs=pl.BlockSpec((t_new, SB_WIDTH), lambda b: (b, 0)),
        out_shape=jax.ShapeDtypeStruct((batch * t_new, SB_WIDTH), F32),
        scratch_shapes=[pltpu.VMEM((N_PAIRS, 2 * t_new, LANES), F32),
                        pltpu.VMEM((SB_WIDTH, KEY_BLOCK), F32),
                        pltpu.VMEM((SB_WIDTH, KEY_BLOCK), F32),
                        pltpu.SemaphoreType.DMA((2,))],
        compiler_params=_params(1),
        name="sb_sample",
    )(q, kt3, vt3, cache_kt, cache_vt, cache_kt, cache_vt)


def _mix_rows(x, o_sb, u_ref, gn_ref, ws_ref, bias_ref, gsb_ref, gsgu_ref, wout_ref, post_ref,
              sgu_ref, rows, scratch_rows):
    lane_lo = lax.broadcasted_iota(jnp.int32, (1, LANES), 1) < HEAD_DIM
    for p in range(N_PAIRS):
        cols = slice(p * LANES, (p + 1) * LANES)
        for c in range((rows.stop - rows.start) // SGU_CHUNK):
            chunk = slice(rows.start + c * SGU_CHUNK, rows.start + (c + 1) * SGU_CHUNK)
            out = slice(scratch_rows.start + c * SGU_CHUNK, scratch_rows.start + (c + 1) * SGU_CHUNK)
            gg = _pair_rows(gn_ref[chunk, cols].astype(BF16), lane_lo)
            mixed = jnp.dot(ws_ref[0, p], gg, preferred_element_type=F32) + bias_ref[0, :, cols]
            sgu_ref[out, cols] = u_ref[chunk, cols] * mixed
    merged = jnp.concatenate([_rms(o_sb, gsb_ref[...]).astype(BF16),
                              _rms(sgu_ref[scratch_rows, :], gsgu_ref[...]).astype(BF16)], axis=1)
    y = jnp.dot(merged, wout_ref[...], preferred_element_type=F32)
    return x + _rms(y, post_ref[...])


def _mix_ffn_kernel(x_ref, op_ref, os_ref, u_ref, gn_ref, ws_ref, bias_ref,
                    xn_ref, opn_ref, un_ref, gnn_ref, wsp_ref, biasp_ref,
                    gsb_ref, gsgu_ref, wout_ref, mix_post_ref,
                    pre_ref, post_ref, wg_hbm, wu_hbm, wd_hbm,
                    yp_ref, ys_ref, sgu_ref,
                    wg_ref, wu_ref, wd_ref, stage_g_ref, stage_u_ref, stage_d_ref, sem_ref,
                    act_ref, h_ref, xa_ref):
    _load_ffn_weights(wg_hbm, wu_hbm, wd_hbm, wg_ref, wu_ref, wd_ref,
                      (stage_g_ref, stage_u_ref, stage_d_ref), sem_ref)
    ffn = _RotatedFfn(pre_ref, post_ref, wg_ref, wu_ref, wd_ref, act_ref, h_ref)
    shared = (gsb_ref, gsgu_ref, wout_ref, mix_post_ref, sgu_ref)

    def mix(rows):
        o_sb = jnp.where(_is_sample_step(), os_ref[rows, :], op_ref[rows, :])
        return _mix_rows(x_ref[rows, :], o_sb, u_ref, gn_ref, ws_ref, bias_ref, *shared, rows, rows)

    def mix_next_first():
        local = slice(0, ffn.first.stop - ffn.first.start)
        return _mix_rows(xn_ref[...], opn_ref[...], un_ref, gnn_ref, wsp_ref, biasp_ref,
                         *shared, local, ffn.first)

    def prepare_first(x):
        xa_ref[...] = x
        ffn.prepare_first(x)

    pl.when(pl.program_id(0) == 0)(lambda: prepare_first(mix(ffn.first)))
    yp_ref[ffn.first, :], yp_ref[ffn.second, :] = ffn.run(
        lambda: xa_ref[...], lambda: mix(ffn.second), mix_next_first, prepare_first)

    @pl.when(_is_sample_step())
    def _():
        ys_ref[...] = yp_ref[...]


def _mix_ffn(x, o_prompt, o_sample, u, gn, ws_pairs, bias, g_sb, g_sgu, w_out, mix_post_g,
             pre_g, post_g, wg, wu, wd, n_prompt_tiles):
    row = _merged_tile_spec(D_MODEL, n_prompt_tiles)
    half = _merged_tile_spec(SB_WIDTH, n_prompt_tiles)
    which = lambda s: jnp.where(s == 0, 1, 0)
    sds = lambda rows: jax.ShapeDtypeStruct((rows, D_MODEL), F32)
    next_row = _next_first_group_spec(D_MODEL, n_prompt_tiles)
    next_half = _next_first_group_spec(SB_WIDTH, n_prompt_tiles)
    return pl.pallas_call(
        _mix_ffn_kernel,
        grid=(n_prompt_tiles + 1,),
        in_specs=[row, _prompt_tile_spec(SB_WIDTH),
                  pl.BlockSpec((ROW_TILE, SB_WIDTH), lambda s: (0, 0), pipeline_mode=pl.Buffered(1)),
                  half, half,
                  pl.BlockSpec((1,) + ws_pairs.shape[1:], lambda s: (which(s), 0, 0, 0)),
                  pl.BlockSpec((1,) + bias.shape[1:], lambda s: (which(s), 0, 0)),
                  next_row, next_half, next_half, next_half,
                  pl.BlockSpec((1,) + ws_pairs.shape[1:], lambda s: (0, 0, 0, 0),
                               pipeline_mode=pl.Buffered(1)),
                  pl.BlockSpec((1,) + bias.shape[1:], lambda s: (0, 0, 0), pipeline_mode=pl.Buffered(1)),
                  _const_spec((1, SB_WIDTH)), _const_spec((1, SGU_WIDTH)),
                  _const_spec((D_MODEL, D_MODEL)), _const_spec((1, D_MODEL))] + _FFN_WEIGHT_SPECS,
        out_specs=[_prompt_tile_spec(D_MODEL), _sample_tile_spec(D_MODEL)],
        out_shape=[sds(n_prompt_tiles * ROW_TILE), sds(ROW_TILE)],
        scratch_shapes=[pltpu.VMEM((ROW_TILE, SGU_WIDTH), F32)] + _FFN_SCRATCH + [
            pltpu.VMEM((FFN_ROW_GROUPS[0].stop, D_MODEL), F32)],
        compiler_params=_params(1),
        name="mix_ffn",
    )(x, o_prompt, o_sample, u, gn, ws_pairs, bias, x, o_prompt, u, gn, ws_pairs, bias,
      g_sb, g_sgu, w_out, mix_post_g, pre_g, post_g, wg, wu, wd)


def _sgu_operands(w_s, b_s, t_new):
    i = jnp.arange(SGU_CHUNK)
    mask = (i[None, :] // SGU_CAUSAL_CHUNK) <= (i[:, None] // SGU_CAUSAL_CHUNK)
    w_prompt = w_s * mask[None].astype(w_s.dtype)
    reps = SGU_CHUNK // t_new
    corner = w_prompt[:, :t_new, :t_new]
    eye = jnp.eye(reps, dtype=w_s.dtype)
    w_sample = jnp.einsum('ab,gij->gaibj', eye, corner).reshape(w_s.shape)
    groups = w_s.shape[0]

    def pairs(w):
        w = w.reshape(groups // 2, 2, SGU_CHUNK, SGU_CHUNK).transpose(0, 2, 1, 3)
        return w.reshape(groups // 2, SGU_CHUNK, 2 * SGU_CHUNK)

    def bias(b):
        return jnp.repeat(b.T, SGU_WIDTH // groups, axis=1)

    ws = jnp.stack([pairs(w_prompt), pairs(w_sample)]).astype(BF16)
    return ws, jnp.stack([bias(b_s), bias(jnp.tile(b_s[:, :t_new], (1, reps)))])


def kernel(x_prompt, x_sample, cache_k_sb, cache_v_sb, ffn1_pre_g, ffn1_post_g, ffn1_w_gate, ffn1_w_up, ffn1_w_down, mix_pre_g, mix_post_g, w_in, sgu_ln_g, sgu_ln_b, sgu_w_s, sgu_b_s, g_out_sb, g_out_sgu, w_out, ffn2_pre_g, ffn2_post_g, ffn2_w_gate, ffn2_w_up, ffn2_w_down):
    depth, batch, seq = w_in.shape[0], x_prompt.shape[0], x_prompt.shape[1]
    dec_batch, t_new, past_len = x_sample.shape[0], x_sample.shape[1], cache_k_sb.shape[2]
    assert depth == 1 and dec_batch * t_new == ROW_TILE and seq % ROW_TILE == 0
    assert SGU_CHUNK % t_new == 0 and KEY_BLOCK % t_new == 0
    assert past_len % (JOINT_PAST_BLOCKS * KEY_BLOCK) == 0
    n_prompt_tiles = batch * seq // ROW_TILE

    mat = lambda w: w[0].astype(BF16)
    vec = lambda g: g[0][None, :]

    x = _ffn(x_prompt.reshape(batch * seq, D_MODEL), x_sample.reshape(ROW_TILE, D_MODEL),
             vec(ffn1_pre_g), vec(ffn1_post_g), ffn1_w_gate[0], ffn1_w_up[0], ffn1_w_down[0],
             n_prompt_tiles)
    q, u, gn, kt3, vt3, kt_p, vt_p, kt_s, vt_s = _inproj(
        x, vec(mix_pre_g), w_in[0], vec(sgu_ln_g), vec(sgu_ln_b), n_prompt_tiles, seq // ROW_TILE)

    o_prompt = _sb_prompt(q, kt3, vt3, batch, seq)
    cache_t = lambda c: c[0].transpose(0, 2, 3, 1).reshape(dec_batch, SB_WIDTH, past_len)
    o_sample = _sb_sample(q, kt3, vt3, cache_t(cache_k_sb), cache_t(cache_v_sb), t_new, batch * seq)

    ws_pairs, bias = _sgu_operands(sgu_w_s[0], sgu_b_s[0], t_new)
    y_p, y_s = _mix_ffn(x, o_prompt, o_sample, u, gn, ws_pairs, bias, vec(g_out_sb), vec(g_out_sgu),
                        mat(w_out), vec(mix_post_g), vec(ffn2_pre_g), vec(ffn2_post_g),
                        ffn2_w_gate[0], ffn2_w_up[0], ffn2_w_down[0], n_prompt_tiles)

    def prompt_heads(t):
        return t.reshape(batch, N_HEADS, HEAD_DIM, seq).transpose(0, 3, 1, 2)[None]

    def sample_heads(t):
        return t.reshape(N_HEADS, HEAD_DIM, dec_batch, t_new).transpose(2, 3, 0, 1)[None]

    g_s = gn[batch * seq:].reshape(1, dec_batch, t_new, N_HEADS, HEAD_DIM)
    return (y_p.reshape(batch, seq, D_MODEL), y_s.reshape(dec_batch, t_new, D_MODEL),
            prompt_heads(kt_p), prompt_heads(vt_p), sample_heads(kt_s), sample_heads(vt_s), g_s)
```

```python
import functools

import jax
import jax.numpy as jnp
from jax import lax
from jax.experimental import pallas as pl
from jax.experimental.pallas import tpu as pltpu

D_MODEL = 1024
D_FF = 2816
SB_WIDTH = 512
SGU_WIDTH = 512
HEAD_DIM = 64
N_HEADS = SB_WIDTH // HEAD_DIM
SGU_CHUNK = 128
SGU_CAUSAL_CHUNK = 64
FFN_RES = 0.5
EPS = 1e-6

LANES = 128
N_PAIRS = SB_WIDTH // LANES
KEY_BLOCK = 128
JOINT_PAST_BLOCKS = 2
TOP_ROWS = 32
Q_BLOCKS_PER_STEP = 8
Q_BLOCKS_PER_PASS = 2
FF_CHUNKS = (1536, 1280)
ROW_TILE = 512
WEIGHT_CHUNK_ROWS = 128
FFN_ROW_GROUPS = (slice(0, 256), slice(256, 512))
VMEM_LIMIT = 56 * 1024 * 1024

LOG2_E = 1.4426950408889634
USED_STICK_CUTOFF = 105.0 * LOG2_E
HIDDEN_LOGIT = -1e30

F32 = jnp.float32
BF16 = jnp.bfloat16


def _rms(x, g):
    return x * lax.rsqrt(jnp.mean(x * x, axis=-1, keepdims=True) + EPS) * g


def _gelu(x):
    return 0.5 * x * (1.0 + lax.erf(x * (0.5 ** 0.5)))


def _const_spec(shape):
    return pl.BlockSpec(shape, lambda *_: (0,) * len(shape), pipeline_mode=pl.Buffered(1))


def _params(n_axes):
    return pltpu.CompilerParams(dimension_semantics=("arbitrary",) * n_axes,
                                vmem_limit_bytes=VMEM_LIMIT)


def _is_sample_step():
    return pl.program_id(0) == 0


def _merged_tile_spec(width, n_prompt_tiles):
    return pl.BlockSpec((ROW_TILE, width), lambda s: ((s + n_prompt_tiles) % (n_prompt_tiles + 1), 0))


def _prompt_tile_spec(width, rows=ROW_TILE):
    return pl.BlockSpec((rows, width), lambda s: (jnp.maximum(s - 1, 0), 0))


def _sample_tile_spec(width, rows=ROW_TILE):
    return pl.BlockSpec((rows, width), lambda s: (0, 0))


def _ffn_stages(x, pre_ref, post_ref, wg_ref, wu_ref, wd_ref, act_ref, rows):
    state = {}

    def hidden():
        state['h'] = _rms(x, pre_ref[...]).astype(BF16)

    def chunk(cols):
        gate = jnp.dot(state['h'], wg_ref[:, cols], preferred_element_type=F32)
        up = jnp.dot(state['h'], wu_ref[:, cols], preferred_element_type=F32)
        act_ref[rows, cols] = (gate * jax.nn.sigmoid(gate) * up).astype(BF16)

    def finish():
        f = jnp.dot(act_ref[rows, :], wd_ref[...], preferred_element_type=F32)
        return x + FFN_RES * _rms(f, post_ref[...])

    stages, start = [hidden], 0
    for width in FF_CHUNKS:
        stages.append(functools.partial(chunk, slice(start, start + width)))
        start += width
    return stages + [finish]


def _ffn_rows(x, pre_ref, post_ref, wg_ref, wu_ref, wd_ref, act_ref, rows):
    for stage in _ffn_stages(x, pre_ref, post_ref, wg_ref, wu_ref, wd_ref, act_ref, rows):
        y = stage()
    return y


_HBM_SPEC = pl.BlockSpec(memory_space=pl.ANY)
_FFN_WEIGHT_SPECS = [_const_spec((1, D_MODEL)), _const_spec((1, D_MODEL)), _HBM_SPEC, _HBM_SPEC, _HBM_SPEC]
_FFN_SCRATCH = [pltpu.VMEM((D_MODEL, D_FF), BF16), pltpu.VMEM((D_MODEL, D_FF), BF16),
                pltpu.VMEM((D_FF, D_MODEL), BF16),
                pltpu.VMEM((2, WEIGHT_CHUNK_ROWS, D_FF), F32),
                pltpu.VMEM((2, WEIGHT_CHUNK_ROWS, D_FF), F32),
                pltpu.VMEM((2, WEIGHT_CHUNK_ROWS, D_MODEL), F32),
                pltpu.SemaphoreType.DMA((6,)),
                pltpu.VMEM((ROW_TILE, D_FF), BF16)]


def _load_bf16(streams):
    def n_chunks(stream):
        return stream[0].shape[0] // stream[2].shape[1]

    def copy(stream, i):
        hbm_ref, _, stage_ref, sem_ref, first_sem = stream
        chunk = stage_ref.shape[1]
        return pltpu.make_async_copy(hbm_ref.at[pl.ds(i * chunk, chunk), :], stage_ref.at[i % 2],
                                     sem_ref.at[first_sem + i % 2])

    for stream in streams:
        copy(stream, 0).start()
    order = sorted(((i + 1) / n_chunks(stream), k, i)
                   for k, stream in enumerate(streams) for i in range(n_chunks(stream)))
    for _, k, i in order:
        stream = streams[k]
        if i + 1 < n_chunks(stream):
            copy(stream, i + 1).start()
        copy(stream, i).wait()
        chunk = stream[2].shape[1]
        stream[1][pl.ds(i * chunk, chunk), :] = stream[2][i % 2].astype(BF16)


def _load_ffn_weights(wg_hbm, wu_hbm, wd_hbm, wg_ref, wu_ref, wd_ref, stage_refs, sem_ref):
    @pl.when(pl.program_id(0) == 0)
    def _():
        _load_bf16([(hbm, dst, stage, sem_ref, 2 * k) for k, (hbm, dst, stage) in enumerate(
            zip((wg_hbm, wu_hbm, wd_hbm), (wg_ref, wu_ref, wd_ref), stage_refs))])


def _ffn_kernel(xp_ref, xs_ref, pre_ref, post_ref, wg_hbm, wu_hbm, wd_hbm, o_ref,
                wg_ref, wu_ref, wd_ref, stage_g_ref, stage_u_ref, stage_d_ref, sem_ref, act_ref):
    _load_ffn_weights(wg_hbm, wu_hbm, wd_hbm, wg_ref, wu_ref, wd_ref,
                      (stage_g_ref, stage_u_ref, stage_d_ref), sem_ref)
    for rows in FFN_ROW_GROUPS:
        x = jnp.where(_is_sample_step(), xs_ref[rows, :], xp_ref[rows, :])
        o_ref[rows, :] = _ffn_rows(x, pre_ref, post_ref, wg_ref, wu_ref, wd_ref, act_ref, rows)


def _ffn(x_prompt, x_sample, pre_g, post_g, wg, wu, wd, n_prompt_tiles):
    n_tiles = n_prompt_tiles + 1
    return pl.pallas_call(
        _ffn_kernel,
        grid=(n_tiles,),
        in_specs=[_prompt_tile_spec(D_MODEL), _sample_tile_spec(D_MODEL)] + _FFN_WEIGHT_SPECS,
        out_specs=_merged_tile_spec(D_MODEL, n_prompt_tiles),
        out_shape=jax.ShapeDtypeStruct((n_tiles * ROW_TILE, D_MODEL), F32),
        scratch_shapes=_FFN_SCRATCH,
        compiler_params=_params(1),
        name="ffn",
    )(x_prompt, x_sample, pre_g, post_g, wg, wu, wd)


def _inproj_kernel(x_ref, g_ref, win_hbm, lng_ref, lnb_ref,
                   q_ref, u_ref, gn_ref, kt3_ref, vt3_ref, ktp_ref, vtp_ref, kts_ref, vts_ref,
                   wqug_ref, wkvt_ref, stage_ref, sem_ref):
    w = SB_WIDTH

    @pl.when(pl.program_id(0) == 0)
    def _():
        chunk = stage_ref.shape[1]
        n_chunks = win_hbm.shape[0] // chunk

        def copy(i):
            return pltpu.make_async_copy(win_hbm.at[pl.ds(i * chunk, chunk), :], stage_ref.at[i % 2],
                                         sem_ref.at[i % 2])

        copy(0).start()
        for i in range(n_chunks):
            if i + 1 < n_chunks:
                copy(i + 1).start()
            copy(i).wait()
            rows = pl.ds(i * chunk, chunk)
            part = stage_ref[i % 2]
            wqug_ref[rows, 0:w] = part[:, 0:w].astype(BF16)
            wqug_ref[rows, w:] = part[:, 3 * w:].astype(BF16)
            wkvt_ref[:, rows] = part[:, w:3 * w].T.astype(BF16)

    h = _rms(x_ref[...], g_ref[...]).astype(BF16)
    z = jnp.dot(h, wqug_ref[...], preferred_element_type=F32)
    kvt = lax.dot_general(wkvt_ref[...], h, (((1,), (1,)), ((), ())), preferred_element_type=F32)
    q_ref[...] = (z[:, 0:w] * (HEAD_DIM ** -0.5 * LOG2_E)).astype(BF16)
    u_ref[...] = _gelu(z[:, w:w + SGU_WIDTH])
    ge = _gelu(z[:, w + SGU_WIDTH:])
    xc = ge - jnp.mean(ge, axis=-1, keepdims=True)
    y = xc * lax.rsqrt(jnp.mean(xc * xc, axis=-1, keepdims=True) + EPS)
    gn_ref[...] = y * lng_ref[...] + lnb_ref[...]
    for t_ref, t3_ref, rows in ((ktp_ref, kt3_ref, slice(0, w)), (vtp_ref, vt3_ref, slice(w, 2 * w))):
        t = kvt[rows]
        t_ref[...] = t
        for c in range(ROW_TILE // KEY_BLOCK):
            t3_ref[c] = t[:, c * KEY_BLOCK:(c + 1) * KEY_BLOCK].astype(BF16)

    @pl.when(_is_sample_step())
    def _():
        kts_ref[...] = ktp_ref[...]
        vts_ref[...] = vtp_ref[...]


def _inproj(x, g, w_in, ln_g, ln_b, n_prompt_tiles, prompt_tiles_per_row):
    n = x.shape[0]
    half = _merged_tile_spec(SB_WIDTH, n_prompt_tiles)
    blocks_per_tile = ROW_TILE // KEY_BLOCK
    key_blocks = pl.BlockSpec((blocks_per_tile, SB_WIDTH, KEY_BLOCK),
                              lambda s: ((s + n_prompt_tiles) % (n_prompt_tiles + 1), 0, 0))

    def prompt_t_index(s):
        tile = jnp.maximum(s - 1, 0)
        return tile // prompt_tiles_per_row, tile % prompt_tiles_per_row

    prompt_t = pl.BlockSpec((SB_WIDTH, ROW_TILE), prompt_t_index)
    prompt_t_sds = jax.ShapeDtypeStruct(
        (n_prompt_tiles // prompt_tiles_per_row * SB_WIDTH, prompt_tiles_per_row * ROW_TILE), F32)
    sample_t_sds = jax.ShapeDtypeStruct((SB_WIDTH, ROW_TILE), F32)
    f32_half = jax.ShapeDtypeStruct((n, SB_WIDTH), F32)
    key_blocks_sds = jax.ShapeDtypeStruct((n // KEY_BLOCK, SB_WIDTH, KEY_BLOCK), BF16)
    return pl.pallas_call(
        _inproj_kernel,
        grid=(n // ROW_TILE,),
        in_specs=[_merged_tile_spec(D_MODEL, n_prompt_tiles), _const_spec((1, D_MODEL)),
                  _HBM_SPEC, _const_spec((1, SGU_WIDTH)), _const_spec((1, SGU_WIDTH))],
        out_specs=[half] * 3 + [key_blocks] * 2 + [prompt_t] * 2 + [_sample_tile_spec(ROW_TILE, SB_WIDTH)] * 2,
        out_shape=[jax.ShapeDtypeStruct((n, SB_WIDTH), BF16), f32_half, f32_half,
                   key_blocks_sds, key_blocks_sds, prompt_t_sds, prompt_t_sds,
                   sample_t_sds, sample_t_sds],
        scratch_shapes=[pltpu.VMEM((D_MODEL, SB_WIDTH + 2 * SGU_WIDTH), BF16),
                        pltpu.VMEM((2 * SB_WIDTH, D_MODEL), BF16),
                        pltpu.VMEM((2, WEIGHT_CHUNK_ROWS, w_in.shape[1]), F32),
                        pltpu.SemaphoreType.DMA((2,))],
        compiler_params=_params(1),
        name="inproj",
    )(x, g, w_in, ln_g, ln_b)


def _pair_rows(x, lane_lo):
    zero = jnp.zeros_like(x)
    return jnp.concatenate([jnp.where(lane_lo, x, zero), jnp.where(lane_lo, zero, x)], axis=0)


def _sb_blocks(qqs, kts, vts, carries, suffix_ones, lane_lo, masks, tops=None):
    n_blocks = len(masks)
    tops = tops or [None] * n_blocks
    t = qqs[0].shape[0] // 2
    nt_dims = (((1,), (1,)), ((), ()))

    def head_rows(x, top):
        return x if top is None else jnp.concatenate([x[:top], x[t:t + top]], axis=0)

    def weighted_values(a, vt):
        both = lax.dot_general(a, vt, nt_dims, preferred_element_type=F32)
        rows = a.shape[0] // 2
        return jnp.where(lane_lo, both[:rows], both[rows:])

    zs = []
    for qq, kts_p in zip(qqs, kts):
        zs_p, b = [], 0
        while b < n_blocks:
            if b + 1 < n_blocks and tops[b] is None and tops[b + 1] is None:
                z2 = jnp.dot(qq, jnp.concatenate([kts_p[b], kts_p[b + 1]], axis=1),
                             preferred_element_type=F32)
                zs_p += [z2[:, :KEY_BLOCK], z2[:, KEY_BLOCK:]]
                b += 2
            else:
                zs_p.append(jnp.dot(head_rows(qq, tops[b]), kts_p[b], preferred_element_type=F32))
                b += 1
        zs.append(zs_p)
    useds, splits = [], []
    for zs_p in zs:
        useds.append([])
        splits.append([])
        for i, mask in enumerate(masks):
            z = zs_p[i]
            if mask is not None:
                z = zs_p[i] = jnp.where(mask, z, HIDDEN_LOGIT)
            used = jnp.maximum(z, 0.0) + jnp.log(1.0 + jnp.exp2(-jnp.abs(z))) * LOG2_E
            hi = used.astype(BF16)
            lo = (used - hi.astype(F32)).astype(BF16)
            useds[-1].append(used)
            splits[-1].append(jnp.concatenate([hi, lo], axis=1))
    sums = [[jnp.dot(s, suffix_ones, preferred_element_type=F32) for s in splits_p]
            for splits_p in splits]
    outs, new_carries = [], []
    for zs_p, useds_p, sums_p, vts_p, carry in zip(zs, useds, sums, vts, carries):
        weights, values_t, top_outs = [], [], []
        for z, used, s, vt, top in zip(zs_p, useds_p, sums_p, vts_p, tops):
            a = jnp.exp2(z - used - (s[:, :KEY_BLOCK] + head_rows(carry, top))).astype(BF16)
            row_sum = s[:, KEY_BLOCK:]
            if top is None:
                weights.append(a)
                values_t.append(vt)
                carry = carry + row_sum
            else:
                top_outs.append(weighted_values(a, vt))
                carry = jnp.concatenate([carry[:top] + row_sum[:top], carry[top:t],
                                         carry[t:t + top] + row_sum[top:], carry[t + top:]], axis=0)
        out = weighted_values(jnp.concatenate(weights, axis=1), jnp.concatenate(values_t, axis=1))
        for o_top in top_outs:
            top = o_top.shape[0]
            out = jnp.concatenate([out[:top] + o_top, out[top:]], axis=0)
        outs.append(out)
        new_carries.append(carry)
    return outs, new_carries


def _suffix_ones():
    r = lax.broadcasted_iota(jnp.int32, (2 * KEY_BLOCK, 2 * KEY_BLOCK), 0) % KEY_BLOCK
    c = lax.broadcasted_iota(jnp.int32, (2 * KEY_BLOCK, 2 * KEY_BLOCK), 1)
    return jnp.where((c >= KEY_BLOCK) | (r > c), 1.0, 0.0).astype(BF16)


_PAIR_SLICES = [slice(p * LANES, (p + 1) * LANES) for p in range(N_PAIRS)]


class _QueryBlock:
    def __init__(self, q_ref, o_ref, carry_ref, n_past, diag_tiles, joint_tiles, walk_tiles):
        self.q_ref, self.o_ref, self.carry_ref, self.n_past = q_ref, o_ref, carry_ref, n_past
        self.diag_tiles, self.joint_tiles, self.walk_tiles = diag_tiles, joint_tiles, walk_tiles


class _StickBreaking:
    def __init__(self, tq, diag_key_offset):
        self.tq = tq
        self.lane_lo = lax.broadcasted_iota(jnp.int32, (1, LANES), 1) < HEAD_DIM
        self.suffix_ones = _suffix_ones()
        self.row = lax.broadcasted_iota(jnp.int32, (2 * tq, KEY_BLOCK), 0) % tq
        col = lax.broadcasted_iota(jnp.int32, (2 * tq, KEY_BLOCK), 1) - diag_key_offset
        self.causal = (col >= 0) & (col < self.row)

    def _qq_tiles(self, blk):
        return [_pair_rows(blk.q_ref[:, cols], self.lane_lo) for cols in _PAIR_SLICES]

    def _store(self, blk, outs, carries, first):
        for p, cols in enumerate(_PAIR_SLICES):
            if first:
                blk.o_ref[:, cols] = outs[p]
            else:
                blk.o_ref[:, cols] += outs[p]
            blk.carry_ref[p] = carries[p]

    def first_step(self, blks, n_joint):
        qqs, kts, vts = [], [], []
        for blk in blks:
            kd, vd = blk.diag_tiles()
            kts_b, vts_b = [[t] for t in kd], [[t] for t in vd]
            for b in range(n_joint):
                kj, vj = blk.joint_tiles(blk.n_past - 1 - b)
                for p in range(N_PAIRS):
                    kts_b[p].append(kj[p])
                    vts_b[p].append(vj[p])
            qqs += self._qq_tiles(blk)
            kts += kts_b
            vts += vts_b
        tops = [None] * n_joint + ([TOP_ROWS] if n_joint > 1 else [None])
        outs, carries = _sb_blocks(qqs, kts, vts, [jnp.zeros((2 * self.tq, LANES), F32)] * len(qqs),
                                   self.suffix_ones, self.lane_lo, [self.causal] + [None] * n_joint, tops)
        for i, blk in enumerate(blks):
            pairs = slice(i * N_PAIRS, (i + 1) * N_PAIRS)
            self._store(blk, outs[pairs], carries[pairs], True)

    def _single_block(self, blk, tiles, mask):
        kts, vts = tiles
        outs, carries = _sb_blocks(self._qq_tiles(blk), [[t] for t in kts], [[t] for t in vts],
                                   [blk.carry_ref[p] for p in range(N_PAIRS)],
                                   self.suffix_ones, self.lane_lo, [mask])
        self._store(blk, outs, carries, False)

    def _least_used(self, blk, rows=None):
        tq = self.tq

        def pick(x):
            return x if rows is None else jnp.concatenate([x[rows], x[tq + rows.start:tq + rows.stop]], axis=0)
        return jnp.min(functools.reduce(jnp.minimum, [pick(blk.carry_ref[p]) for p in range(N_PAIRS)]))

    def finish_partial_block(self, blk):
        @pl.when(self._least_used(blk, slice(TOP_ROWS, self.tq)) < USED_STICK_CUTOFF)
        def _():
            self._single_block(blk, blk.joint_tiles(blk.n_past - JOINT_PAST_BLOCKS), self.row >= TOP_ROWS)

    def walk(self, blk, next_block):
        def body(state):
            j, _ = state
            self._single_block(blk, blk.walk_tiles(j), None)
            return j - 1, self._least_used(blk)

        lax.while_loop(lambda s: (s[0] >= 0) & (s[1] < USED_STICK_CUTOFF), body,
                       (next_block, self._least_used(blk)))


def _sb_prompt_kernel(q_ref, kt_ref, vt_ref, o_ref, carry_ref):
    def tiles(j):
        return ([kt_ref[j, rows, :] for rows in _PAIR_SLICES],
                [vt_ref[j, rows, :] for rows in _PAIR_SLICES])

    step = pl.program_id(1)
    sb = _StickBreaking(KEY_BLOCK, 0)
    blks = []
    for r in range(Q_BLOCKS_PER_STEP):
        i = step * Q_BLOCKS_PER_STEP + r
        rows = pl.ds(r * KEY_BLOCK, KEY_BLOCK)
        blks.append(_QueryBlock(q_ref.at[rows, :], o_ref.at[rows, :], carry_ref.at[r], i,
                                functools.partial(tiles, i), tiles, tiles))

    assert Q_BLOCKS_PER_STEP >= JOINT_PAST_BLOCKS > 1

    @pl.when(step > 0)
    def _():
        for first in range(0, Q_BLOCKS_PER_STEP, Q_BLOCKS_PER_PASS):
            sb.first_step(blks[first:first + Q_BLOCKS_PER_PASS], JOINT_PAST_BLOCKS)
        for blk in blks:
            sb.finish_partial_block(blk)

    @pl.when(step == 0)
    def _():
        for r, blk in enumerate(blks):
            sb.first_step([blk], min(r, JOINT_PAST_BLOCKS))
            if r >= JOINT_PAST_BLOCKS:
                sb.finish_partial_block(blk)

    for r, blk in enumerate(blks):
        sb.walk(blk, jnp.where(step > 0, blk.n_past - JOINT_PAST_BLOCKS,
                               r - min(r, JOINT_PAST_BLOCKS)) - 1)


def _sb_prompt(q, kt3, vt3, batch, seq):
    nq = seq // (KEY_BLOCK * Q_BLOCKS_PER_STEP)
    blk = pl.BlockSpec((KEY_BLOCK * Q_BLOCKS_PER_STEP, SB_WIDTH), lambda b, i: (b * nq + i, 0))
    keys = pl.BlockSpec((seq // KEY_BLOCK, SB_WIDTH, KEY_BLOCK), lambda b, i: (b, 0, 0))
    return pl.pallas_call(
        _sb_prompt_kernel,
        grid=(batch, nq),
        in_specs=[blk, keys, keys],
        out_specs=blk,
        out_shape=jax.ShapeDtypeStruct((batch * seq, SB_WIDTH), F32),
        scratch_shapes=[pltpu.VMEM((Q_BLOCKS_PER_STEP, N_PAIRS, 2 * KEY_BLOCK, LANES), F32)],
        compiler_params=_params(2),
        name="sb_prompt",
    )(q, kt3, vt3)


def _sb_sample_kernel(q_ref, kd_ref, vd_ref, kr_ref, vr_ref, kc_hbm, vc_hbm, o_ref,
                      carry_ref, kbuf_ref, vbuf_ref, sem_ref, *, past_len):
    tq = q_ref.shape[0]
    n_past = past_len // KEY_BLOCK
    n_recent = kr_ref.shape[2] // KEY_BLOCK
    b = pl.program_id(0)

    def diag_tiles():
        return ([kd_ref[0, rows, :] for rows in _PAIR_SLICES],
                [vd_ref[0, rows, :] for rows in _PAIR_SLICES])

    def recent_tiles(j):
        first = (j - (n_past - n_recent)) * KEY_BLOCK
        keys = slice(first, first + KEY_BLOCK)
        return ([kr_ref[0, rows, keys].astype(BF16) for rows in _PAIR_SLICES],
                [vr_ref[0, rows, keys].astype(BF16) for rows in _PAIR_SLICES])

    def cache_tiles(j):
        keys = pl.ds(pl.multiple_of(j * KEY_BLOCK, KEY_BLOCK), KEY_BLOCK)
        copies = [pltpu.make_async_copy(hbm.at[b, :, keys], buf, sem_ref.at[s])
                  for s, (hbm, buf) in enumerate(((kc_hbm, kbuf_ref), (vc_hbm, vbuf_ref)))]
        for c in copies:
            c.start()
        for c in copies:
            c.wait()
        return ([kbuf_ref[rows, :].astype(BF16) for rows in _PAIR_SLICES],
                [vbuf_ref[rows, :].astype(BF16) for rows in _PAIR_SLICES])

    assert n_recent == JOINT_PAST_BLOCKS > 1
    sb = _StickBreaking(tq, (b % (KEY_BLOCK // tq)) * tq)
    blk = _QueryBlock(q_ref, o_ref, carry_ref, n_past, diag_tiles, recent_tiles, cache_tiles)
    sb.first_step([blk], JOINT_PAST_BLOCKS)
    sb.finish_partial_block(blk)
    sb.walk(blk, n_past - JOINT_PAST_BLOCKS - 1)


def _sb_sample(q, kt3, vt3, cache_kt, cache_vt, t_new, first_row):
    batch, _, past_len = cache_kt.shape
    n_recent = JOINT_PAST_BLOCKS * KEY_BLOCK
    seqs_per_block = KEY_BLOCK // t_new
    new = pl.BlockSpec((1, SB_WIDTH, KEY_BLOCK),
                       lambda b: (first_row // KEY_BLOCK + b // seqs_per_block, 0, 0))
    recent = pl.BlockSpec((1, SB_WIDTH, n_recent), lambda b: (b, 0, past_len // n_recent - 1))
    hbm = pl.BlockSpec(memory_space=pl.ANY)
    return pl.pallas_call(
        functools.partial(_sb_sample_kernel, past_len=past_len),
        grid=(batch,),
        in_specs=[pl.BlockSpec((t_new, SB_WIDTH), lambda b: (first_row // t_new + b, 0)),
                  new, new, recent, recent, hbm, hbm],
        out_specs=pl.BlockSpec((t_new, SB_WIDTH), lambda b: (b, 0)),
        out_shape=jax.ShapeDtypeStruct((batch * t_new, SB_WIDTH), F32),
        scratch_shapes=[pltpu.VMEM((N_PAIRS, 2 * t_new, LANES), F32),
                        pltpu.VMEM((SB_WIDTH, KEY_BLOCK), F32),
                        pltpu.VMEM((SB_WIDTH, KEY_BLOCK), F32),
                        pltpu.SemaphoreType.DMA((2,))],
        compiler_params=_params(1),
        name="sb_sample",
    )(q, kt3, vt3, cache_kt, cache_vt, cache_kt, cache_vt)


def _mix_rows(x_ref, op_ref, os_ref, u_ref, gn_ref, ws_ref, bias_ref, gsb_ref, gsgu_ref, wout_ref,
              post_ref, sgu_ref, rows):
    lane_lo = lax.broadcasted_iota(jnp.int32, (1, LANES), 1) < HEAD_DIM
    for p in range(N_PAIRS):
        cols = slice(p * LANES, (p + 1) * LANES)
        for first in range(rows.start, rows.stop, SGU_CHUNK):
            chunk = slice(first, first + SGU_CHUNK)
            gg = _pair_rows(gn_ref[chunk, cols].astype(BF16), lane_lo)
            mixed = jnp.dot(ws_ref[0, p], gg, preferred_element_type=F32) + bias_ref[0, :, cols]
            sgu_ref[chunk, cols] = u_ref[chunk, cols] * mixed
    o_sb = jnp.where(_is_sample_step(), os_ref[rows, :], op_ref[rows, :])
    merged = jnp.concatenate([_rms(o_sb, gsb_ref[...]).astype(BF16),
                              _rms(sgu_ref[rows, :], gsgu_ref[...]).astype(BF16)], axis=1)
    y = jnp.dot(merged, wout_ref[...], preferred_element_type=F32)
    return x_ref[rows, :] + _rms(y, post_ref[...])


def _mix_ffn_kernel(x_ref, op_ref, os_ref, u_ref, gn_ref, ws_ref, bias_ref, gsb_ref, gsgu_ref,
                    wout_ref, mix_post_ref, pre_ref, post_ref, wg_hbm, wu_hbm, wd_hbm,
                    yp_ref, ys_ref, sgu_ref,
                    wg_ref, wu_ref, wd_ref, stage_g_ref, stage_u_ref, stage_d_ref, sem_ref, act_ref):
    _load_ffn_weights(wg_hbm, wu_hbm, wd_hbm, wg_ref, wu_ref, wd_ref,
                      (stage_g_ref, stage_u_ref, stage_d_ref), sem_ref)

    def mix(rows):
        return _mix_rows(x_ref, op_ref, os_ref, u_ref, gn_ref, ws_ref, bias_ref, gsb_ref, gsgu_ref,
                         wout_ref, mix_post_ref, sgu_ref, rows)

    groups = FFN_ROW_GROUPS
    x = mix(groups[0])
    for g, rows in enumerate(groups):
        stages = _ffn_stages(x, pre_ref, post_ref, wg_ref, wu_ref, wd_ref, act_ref, rows)
        stages[0]()
        stages[1]()
        if g + 1 < len(groups):
            x = mix(groups[g + 1])
        for stage in stages[2:]:
            y = stage()
        yp_ref[rows, :] = y

    @pl.when(_is_sample_step())
    def _():
        ys_ref[...] = yp_ref[...]


def _mix_ffn(x, o_prompt, o_sample, u, gn, ws_pairs, bias, g_sb, g_sgu, w_out, mix_post_g,
             pre_g, post_g, wg, wu, wd, n_prompt_tiles):
    row = _merged_tile_spec(D_MODEL, n_prompt_tiles)
    half = _merged_tile_spec(SB_WIDTH, n_prompt_tiles)
    which = lambda s: jnp.where(s == 0, 1, 0)
    sds = lambda rows: jax.ShapeDtypeStruct((rows, D_MODEL), F32)
    return pl.pallas_call(
        _mix_ffn_kernel,
        grid=(n_prompt_tiles + 1,),
        in_specs=[row, _prompt_tile_spec(SB_WIDTH), _sample_tile_spec(SB_WIDTH), half, half,
                  pl.BlockSpec((1,) + ws_pairs.shape[1:], lambda s: (which(s), 0, 0, 0)),
                  pl.BlockSpec((1,) + bias.shape[1:], lambda s: (which(s), 0, 0)),
                  _const_spec((1, SB_WIDTH)), _const_spec((1, SGU_WIDTH)),
                  _const_spec((D_MODEL, D_MODEL)), _const_spec((1, D_MODEL))] + _FFN_WEIGHT_SPECS,
        out_specs=[_prompt_tile_spec(D_MODEL), _sample_tile_spec(D_MODEL)],
        out_shape=[sds(n_prompt_tiles * ROW_TILE), sds(ROW_TILE)],
        scratch_shapes=[pltpu.VMEM((ROW_TILE, SGU_WIDTH), F32)] + _FFN_SCRATCH,
        compiler_params=_params(1),
        name="mix_ffn",
    )(x, o_prompt, o_sample, u, gn, ws_pairs, bias, g_sb, g_sgu, w_out, mix_post_g,
      pre_g, post_g, wg, wu, wd)


def _sgu_operands(w_s, b_s, t_new):
    i = jnp.arange(SGU_CHUNK)
    mask = (i[None, :] // SGU_CAUSAL_CHUNK) <= (i[:, None] // SGU_CAUSAL_CHUNK)
    w_prompt = w_s * mask[None].astype(w_s.dtype)
    reps = SGU_CHUNK // t_new
    corner = w_prompt[:, :t_new, :t_new]
    eye = jnp.eye(reps, dtype=w_s.dtype)
    w_sample = jnp.einsum('ab,gij->gaibj', eye, corner).reshape(w_s.shape)
    groups = w_s.shape[0]

    def pairs(w):
        w = w.reshape(groups // 2, 2, SGU_CHUNK, SGU_CHUNK).transpose(0, 2, 1, 3)
        return w.reshape(groups // 2, SGU_CHUNK, 2 * SGU_CHUNK)

    def bias(b):
        return jnp.repeat(b.T, SGU_WIDTH // groups, axis=1)

    ws = jnp.stack([pairs(w_prompt), pairs(w_sample)]).astype(BF16)
    return ws, jnp.stack([bias(b_s), bias(jnp.tile(b_s[:, :t_new], (1, reps)))])


def kernel(x_prompt, x_sample, cache_k_sb, cache_v_sb, ffn1_pre_g, ffn1_post_g, ffn1_w_gate, ffn1_w_up, ffn1_w_down, mix_pre_g, mix_post_g, w_in, sgu_ln_g, sgu_ln_b, sgu_w_s, sgu_b_s, g_out_sb, g_out_sgu, w_out, ffn2_pre_g, ffn2_post_g, ffn2_w_gate, ffn2_w_up, ffn2_w_down):
    depth, batch, seq = w_in.shape[0], x_prompt.shape[0], x_prompt.shape[1]
    dec_batch, t_new, past_len = x_sample.shape[0], x_sample.shape[1], cache_k_sb.shape[2]
    assert depth == 1 and dec_batch * t_new == ROW_TILE and seq % ROW_TILE == 0
    assert SGU_CHUNK % t_new == 0 and KEY_BLOCK % t_new == 0
    assert past_len % (JOINT_PAST_BLOCKS * KEY_BLOCK) == 0
    n_prompt_tiles = batch * seq // ROW_TILE

    mat = lambda w: w[0].astype(BF16)
    vec = lambda g: g[0][None, :]

    x = _ffn(x_prompt.reshape(batch * seq, D_MODEL), x_sample.reshape(ROW_TILE, D_MODEL),
             vec(ffn1_pre_g), vec(ffn1_post_g), ffn1_w_gate[0], ffn1_w_up[0], ffn1_w_down[0],
             n_prompt_tiles)
    q, u, gn, kt3, vt3, kt_p, vt_p, kt_s, vt_s = _inproj(
        x, vec(mix_pre_g), w_in[0], vec(sgu_ln_g), vec(sgu_ln_b), n_prompt_tiles, seq // ROW_TILE)

    o_prompt = _sb_prompt(q, kt3, vt3, batch, seq)
    cache_t = lambda c: c[0].transpose(0, 2, 3, 1).reshape(dec_batch, SB_WIDTH, past_len)
    o_sample = _sb_sample(q, kt3, vt3, cache_t(cache_k_sb), cache_t(cache_v_sb), t_new, batch * seq)

    ws_pairs, bias = _sgu_operands(sgu_w_s[0], sgu_b_s[0], t_new)
    y_p, y_s = _mix_ffn(x, o_prompt, o_sample, u, gn, ws_pairs, bias, vec(g_out_sb), vec(g_out_sgu),
                        mat(w_out), vec(mix_post_g), vec(ffn2_pre_g), vec(ffn2_post_g),
                        ffn2_w_gate[0], ffn2_w_up[0], ffn2_w_down[0], n_prompt_tiles)

    def prompt_heads(t):
        return t.reshape(batch, N_HEADS, HEAD_DIM, seq).transpose(0, 3, 1, 2)[None]

    def sample_heads(t):
        return t.reshape(N_HEADS, HEAD_DIM, dec_batch, t_new).transpose(2, 3, 0, 1)[None]

    g_s = gn[batch * seq:].reshape(1, dec_batch, t_new, N_HEADS, HEAD_DIM)
    return (y_p.reshape(batch, seq, D_MODEL), y_s.reshape(dec_batch, t_new, D_MODEL),
            prompt_heads(kt_p), prompt_heads(vt_p), sample_heads(kt_s), sample_heads(vt_s), g_s)
```

```python
import functools

import jax
import jax.numpy as jnp
from jax import lax
from jax.experimental import pallas as pl
from jax.experimental.pallas import tpu as pltpu

D_MODEL = 1024
D_FF = 2816
SB_WIDTH = 512
SGU_WIDTH = 512
HEAD_DIM = 64
N_HEADS = SB_WIDTH // HEAD_DIM
SGU_CHUNK = 128
SGU_CAUSAL_CHUNK = 64
FFN_RES = 0.5
EPS = 1e-6

LANES = 128
N_PAIRS = SB_WIDTH // LANES
KEY_BLOCK = 128
JOINT_PAST_BLOCKS = 2
TOP_ROWS = 32
Q_BLOCKS_PER_STEP = 4
Q_BLOCKS_PER_PASS = 2
FF_CHUNKS = (1536, 1280)
ROW_TILE = 512
WEIGHT_CHUNK_ROWS = 128
FFN_ROW_GROUPS = (slice(0, 256), slice(256, 512))
VMEM_LIMIT = 56 * 1024 * 1024

LOG2_E = 1.4426950408889634
USED_STICK_CUTOFF = 105.0 * LOG2_E
HIDDEN_LOGIT = -1e30

F32 = jnp.float32
BF16 = jnp.bfloat16


def _rms(x, g):
    return x * lax.rsqrt(jnp.mean(x * x, axis=-1, keepdims=True) + EPS) * g


def _gelu(x):
    return 0.5 * x * (1.0 + lax.erf(x * (0.5 ** 0.5)))


def _const_spec(shape):
    return pl.BlockSpec(shape, lambda *_: (0,) * len(shape), pipeline_mode=pl.Buffered(1))


def _params(n_axes):
    return pltpu.CompilerParams(dimension_semantics=("arbitrary",) * n_axes,
                                vmem_limit_bytes=VMEM_LIMIT)


def _is_sample_step():
    return pl.program_id(0) == 0


def _merged_tile_spec(width, n_prompt_tiles):
    return pl.BlockSpec((ROW_TILE, width), lambda s: ((s + n_prompt_tiles) % (n_prompt_tiles + 1), 0))


def _prompt_tile_spec(width, rows=ROW_TILE):
    return pl.BlockSpec((rows, width), lambda s: (jnp.maximum(s - 1, 0), 0))


def _sample_tile_spec(width, rows=ROW_TILE):
    return pl.BlockSpec((rows, width), lambda s: (0, 0))


def _ffn_stages(x, pre_ref, post_ref, wg_ref, wu_ref, wd_ref, act_ref, rows):
    state = {}

    def hidden():
        state['h'] = _rms(x, pre_ref[...]).astype(BF16)

    def chunk(cols):
        gate = jnp.dot(state['h'], wg_ref[:, cols], preferred_element_type=F32)
        up = jnp.dot(state['h'], wu_ref[:, cols], preferred_element_type=F32)
        act_ref[rows, cols] = (gate * jax.nn.sigmoid(gate) * up).astype(BF16)

    def finish():
        f = jnp.dot(act_ref[rows, :], wd_ref[...], preferred_element_type=F32)
        return x + FFN_RES * _rms(f, post_ref[...])

    stages, start = [hidden], 0
    for width in FF_CHUNKS:
        stages.append(functools.partial(chunk, slice(start, start + width)))
        start += width
    return stages + [finish]


def _ffn_rows(x, pre_ref, post_ref, wg_ref, wu_ref, wd_ref, act_ref, rows):
    for stage in _ffn_stages(x, pre_ref, post_ref, wg_ref, wu_ref, wd_ref, act_ref, rows):
        y = stage()
    return y


_HBM_SPEC = pl.BlockSpec(memory_space=pl.ANY)
_FFN_WEIGHT_SPECS = [_const_spec((1, D_MODEL)), _const_spec((1, D_MODEL)), _HBM_SPEC, _HBM_SPEC, _HBM_SPEC]
_FFN_SCRATCH = [pltpu.VMEM((D_MODEL, D_FF), BF16), pltpu.VMEM((D_MODEL, D_FF), BF16),
                pltpu.VMEM((D_FF, D_MODEL), BF16),
                pltpu.VMEM((2, WEIGHT_CHUNK_ROWS, D_FF), F32),
                pltpu.VMEM((2, WEIGHT_CHUNK_ROWS, D_FF), F32),
                pltpu.VMEM((2, WEIGHT_CHUNK_ROWS, D_MODEL), F32),
                pltpu.SemaphoreType.DMA((6,)),
                pltpu.VMEM((ROW_TILE, D_FF), BF16)]


def _load_bf16(streams):
    def n_chunks(stream):
        return stream[0].shape[0] // stream[2].shape[1]

    def copy(stream, i):
        hbm_ref, _, stage_ref, sem_ref, first_sem = stream
        chunk = stage_ref.shape[1]
        return pltpu.make_async_copy(hbm_ref.at[pl.ds(i * chunk, chunk), :], stage_ref.at[i % 2],
                                     sem_ref.at[first_sem + i % 2])

    for stream in streams:
        copy(stream, 0).start()
    order = sorted(((i + 1) / n_chunks(stream), k, i)
                   for k, stream in enumerate(streams) for i in range(n_chunks(stream)))
    for _, k, i in order:
        stream = streams[k]
        if i + 1 < n_chunks(stream):
            copy(stream, i + 1).start()
        copy(stream, i).wait()
        chunk = stream[2].shape[1]
        stream[1][pl.ds(i * chunk, chunk), :] = stream[2][i % 2].astype(BF16)


def _load_ffn_weights(wg_hbm, wu_hbm, wd_hbm, wg_ref, wu_ref, wd_ref, stage_refs, sem_ref):
    @pl.when(pl.program_id(0) == 0)
    def _():
        _load_bf16([(hbm, dst, stage, sem_ref, 2 * k) for k, (hbm, dst, stage) in enumerate(
            zip((wg_hbm, wu_hbm, wd_hbm), (wg_ref, wu_ref, wd_ref), stage_refs))])


def _ffn_kernel(xp_ref, xs_ref, pre_ref, post_ref, wg_hbm, wu_hbm, wd_hbm, o_ref,
                wg_ref, wu_ref, wd_ref, stage_g_ref, stage_u_ref, stage_d_ref, sem_ref, act_ref):
    _load_ffn_weights(wg_hbm, wu_hbm, wd_hbm, wg_ref, wu_ref, wd_ref,
                      (stage_g_ref, stage_u_ref, stage_d_ref), sem_ref)
    for rows in FFN_ROW_GROUPS:
        x = jnp.where(_is_sample_step(), xs_ref[rows, :], xp_ref[rows, :])
        o_ref[rows, :] = _ffn_rows(x, pre_ref, post_ref, wg_ref, wu_ref, wd_ref, act_ref, rows)


def _ffn(x_prompt, x_sample, pre_g, post_g, wg, wu, wd, n_prompt_tiles):
    n_tiles = n_prompt_tiles + 1
    return pl.pallas_call(
        _ffn_kernel,
        grid=(n_tiles,),
        in_specs=[_prompt_tile_spec(D_MODEL), _sample_tile_spec(D_MODEL)] + _FFN_WEIGHT_SPECS,
        out_specs=_merged_tile_spec(D_MODEL, n_prompt_tiles),
        out_shape=jax.ShapeDtypeStruct((n_tiles * ROW_TILE, D_MODEL), F32),
        scratch_shapes=_FFN_SCRATCH,
        compiler_params=_params(1),
        name="ffn",
    )(x_prompt, x_sample, pre_g, post_g, wg, wu, wd)


def _inproj_kernel(x_ref, g_ref, win_hbm, lng_ref, lnb_ref,
                   q_ref, u_ref, gn_ref, kt3_ref, vt3_ref, ktp_ref, vtp_ref, kts_ref, vts_ref,
                   wqug_ref, wkvt_ref, stage_ref, sem_ref):
    w = SB_WIDTH

    @pl.when(pl.program_id(0) == 0)
    def _():
        chunk = stage_ref.shape[1]
        n_chunks = win_hbm.shape[0] // chunk

        def copy(i):
            return pltpu.make_async_copy(win_hbm.at[pl.ds(i * chunk, chunk), :], stage_ref.at[i % 2],
                                         sem_ref.at[i % 2])

        copy(0).start()
        for i in range(n_chunks):
            if i + 1 < n_chunks:
                copy(i + 1).start()
            copy(i).wait()
            rows = pl.ds(i * chunk, chunk)
            part = stage_ref[i % 2]
            wqug_ref[rows, 0:w] = part[:, 0:w].astype(BF16)
            wqug_ref[rows, w:] = part[:, 3 * w:].astype(BF16)
            wkvt_ref[:, rows] = part[:, w:3 * w].T.astype(BF16)

    h = _rms(x_ref[...], g_ref[...]).astype(BF16)
    z = jnp.dot(h, wqug_ref[:, w:], preferred_element_type=F32)
    u_ref[...] = _gelu(z[:, :SGU_WIDTH])
    ge = _gelu(z[:, SGU_WIDTH:])
    xc = ge - jnp.mean(ge, axis=-1, keepdims=True)
    y = xc * lax.rsqrt(jnp.mean(xc * xc, axis=-1, keepdims=True) + EPS)
    gn_ref[...] = y * lng_ref[...] + lnb_ref[...]
    kvt = lax.dot_general(wkvt_ref[...], h, (((1,), (1,)), ((), ())), preferred_element_type=F32)
    for t_ref, t3_ref, rows in ((ktp_ref, kt3_ref, slice(0, w)), (vtp_ref, vt3_ref, slice(w, 2 * w))):
        t = kvt[rows]
        t_ref[...] = t
        for c in range(ROW_TILE // KEY_BLOCK):
            t3_ref[c] = t[:, c * KEY_BLOCK:(c + 1) * KEY_BLOCK].astype(BF16)
    q = jnp.dot(h, wqug_ref[:, :w], preferred_element_type=F32)
    q_ref[...] = (q * (HEAD_DIM ** -0.5 * LOG2_E)).astype(BF16)

    @pl.when(_is_sample_step())
    def _():
        kts_ref[...] = ktp_ref[...]
        vts_ref[...] = vtp_ref[...]


def _inproj(x, g, w_in, ln_g, ln_b, n_prompt_tiles, prompt_tiles_per_row):
    n = x.shape[0]
    half = _merged_tile_spec(SB_WIDTH, n_prompt_tiles)
    blocks_per_tile = ROW_TILE // KEY_BLOCK
    key_blocks = pl.BlockSpec((blocks_per_tile, SB_WIDTH, KEY_BLOCK),
                              lambda s: ((s + n_prompt_tiles) % (n_prompt_tiles + 1), 0, 0))

    def prompt_t_index(s):
        tile = jnp.maximum(s - 1, 0)
        return tile // prompt_tiles_per_row, tile % prompt_tiles_per_row

    prompt_t = pl.BlockSpec((SB_WIDTH, ROW_TILE), prompt_t_index)
    prompt_t_sds = jax.ShapeDtypeStruct(
        (n_prompt_tiles // prompt_tiles_per_row * SB_WIDTH, prompt_tiles_per_row * ROW_TILE), F32)
    sample_t_sds = jax.ShapeDtypeStruct((SB_WIDTH, ROW_TILE), F32)
    f32_half = jax.ShapeDtypeStruct((n, SB_WIDTH), F32)
    key_blocks_sds = jax.ShapeDtypeStruct((n // KEY_BLOCK, SB_WIDTH, KEY_BLOCK), BF16)
    return pl.pallas_call(
        _inproj_kernel,
        grid=(n // ROW_TILE,),
        in_specs=[_merged_tile_spec(D_MODEL, n_prompt_tiles), _const_spec((1, D_MODEL)),
                  _HBM_SPEC, _const_spec((1, SGU_WIDTH)), _const_spec((1, SGU_WIDTH))],
        out_specs=[half] * 3 + [key_blocks] * 2 + [prompt_t] * 2 + [_sample_tile_spec(ROW_TILE, SB_WIDTH)] * 2,
        out_shape=[jax.ShapeDtypeStruct((n, SB_WIDTH), BF16), f32_half, f32_half,
                   key_blocks_sds, key_blocks_sds, prompt_t_sds, prompt_t_sds,
                   sample_t_sds, sample_t_sds],
        scratch_shapes=[pltpu.VMEM((D_MODEL, SB_WIDTH + 2 * SGU_WIDTH), BF16),
                        pltpu.VMEM((2 * SB_WIDTH, D_MODEL), BF16),
                        pltpu.VMEM((2, WEIGHT_CHUNK_ROWS, w_in.shape[1]), F32),
                        pltpu.SemaphoreType.DMA((2,))],
        compiler_params=_params(1),
        name="inproj",
    )(x, g, w_in, ln_g, ln_b)


def _pair_rows(x, lane_lo):
    zero = jnp.zeros_like(x)
    return jnp.concatenate([jnp.where(lane_lo, x, zero), jnp.where(lane_lo, zero, x)], axis=0)


def _sb_blocks(qqs, kts, vts, carries, suffix_ones, lane_lo, masks, tops=None):
    n_blocks = len(masks)
    tops = tops or [None] * n_blocks
    t = qqs[0].shape[0] // 2
    nt_dims = (((1,), (1,)), ((), ()))

    def head_rows(x, top):
        return x if top is None else jnp.concatenate([x[:top], x[t:t + top]], axis=0)

    def weighted_values(a, vt):
        both = lax.dot_general(a, vt, nt_dims, preferred_element_type=F32)
        rows = a.shape[0] // 2
        return jnp.where(lane_lo, both[:rows], both[rows:])

    zs = []
    for qq, kts_p in zip(qqs, kts):
        zs_p, b = [], 0
        while b < n_blocks:
            if b + 1 < n_blocks and tops[b] is None and tops[b + 1] is None:
                z2 = jnp.dot(qq, jnp.concatenate([kts_p[b], kts_p[b + 1]], axis=1),
                             preferred_element_type=F32)
                zs_p += [z2[:, :KEY_BLOCK], z2[:, KEY_BLOCK:]]
                b += 2
            else:
                zs_p.append(jnp.dot(head_rows(qq, tops[b]), kts_p[b], preferred_element_type=F32))
                b += 1
        zs.append(zs_p)
    useds, splits = [], []
    for zs_p in zs:
        useds.append([])
        splits.append([])
        for i, mask in enumerate(masks):
            z = zs_p[i]
            if mask is not None:
                z = zs_p[i] = jnp.where(mask, z, HIDDEN_LOGIT)
            used = jnp.maximum(z, 0.0) + jnp.log(1.0 + jnp.exp2(-jnp.abs(z))) * LOG2_E
            hi = used.astype(BF16)
            lo = (used - hi.astype(F32)).astype(BF16)
            useds[-1].append(used)
            splits[-1].append(jnp.concatenate([hi, lo], axis=1))
    sums = [[jnp.dot(s, suffix_ones, preferred_element_type=F32) for s in splits_p]
            for splits_p in splits]
    outs, new_carries = [], []
    for zs_p, useds_p, sums_p, vts_p, carry in zip(zs, useds, sums, vts, carries):
        weights, values_t, top_outs = [], [], []
        for z, used, s, vt, top in zip(zs_p, useds_p, sums_p, vts_p, tops):
            later = s[:, :KEY_BLOCK] if carry is None else s[:, :KEY_BLOCK] + head_rows(carry, top)
            a = jnp.exp2(z - used - later).astype(BF16)
            row_sum = s[:, KEY_BLOCK:]
            if top is None:
                weights.append(a)
                values_t.append(vt)
                carry = row_sum if carry is None else carry + row_sum
            else:
                top_outs.append(weighted_values(a, vt))
                carry = jnp.concatenate([carry[:top] + row_sum[:top], carry[top:t],
                                         carry[t:t + top] + row_sum[top:], carry[t + top:]], axis=0)
        out = weighted_values(jnp.concatenate(weights, axis=1), jnp.concatenate(values_t, axis=1))
        for o_top in top_outs:
            top = o_top.shape[0]
            out = jnp.concatenate([out[:top] + o_top, out[top:]], axis=0)
        outs.append(out)
        new_carries.append(carry)
    return outs, new_carries


def _suffix_ones():
    r = lax.broadcasted_iota(jnp.int32, (2 * KEY_BLOCK, 2 * KEY_BLOCK), 0) % KEY_BLOCK
    c = lax.broadcasted_iota(jnp.int32, (2 * KEY_BLOCK, 2 * KEY_BLOCK), 1)
    return jnp.where((c >= KEY_BLOCK) | (r > c), 1.0, 0.0).astype(BF16)


_PAIR_SLICES = [slice(p * LANES, (p + 1) * LANES) for p in range(N_PAIRS)]


class _QueryBlock:
    def __init__(self, q_ref, o_ref, carry_ref, n_past, diag_tiles, joint_tiles, walk_tiles):
        self.q_ref, self.o_ref, self.carry_ref, self.n_past = q_ref, o_ref, carry_ref, n_past
        self.least_ref = carry_ref.at[N_PAIRS]
        self.diag_tiles, self.joint_tiles, self.walk_tiles = diag_tiles, joint_tiles, walk_tiles


class _StickBreaking:
    def __init__(self, tq, diag_key_offset):
        self.tq = tq
        self.lane_lo = lax.broadcasted_iota(jnp.int32, (1, LANES), 1) < HEAD_DIM
        self.suffix_ones = _suffix_ones()
        self.row = lax.broadcasted_iota(jnp.int32, (2 * tq, KEY_BLOCK), 0) % tq
        col = lax.broadcasted_iota(jnp.int32, (2 * tq, KEY_BLOCK), 1) - diag_key_offset
        self.causal = (col >= 0) & (col < self.row)

    def _qq_tiles(self, blk):
        return [_pair_rows(blk.q_ref[:, cols], self.lane_lo) for cols in _PAIR_SLICES]

    def _store(self, blk, outs, carries, first):
        for p, cols in enumerate(_PAIR_SLICES):
            if first:
                blk.o_ref[:, cols] = outs[p]
            else:
                blk.o_ref[:, cols] += outs[p]
            blk.carry_ref[p] = carries[p]

    def first_step(self, blks, n_joint):
        qqs, kts, vts = [], [], []
        for blk in blks:
            kd, vd = blk.diag_tiles()
            kts_b, vts_b = [[t] for t in kd], [[t] for t in vd]
            for b in range(n_joint):
                kj, vj = blk.joint_tiles(blk.n_past - 1 - b)
                for p in range(N_PAIRS):
                    kts_b[p].append(kj[p])
                    vts_b[p].append(vj[p])
            qqs += self._qq_tiles(blk)
            kts += kts_b
            vts += vts_b
        tops = [None] * n_joint + ([TOP_ROWS] if n_joint > 1 else [None])
        outs, carries = _sb_blocks(qqs, kts, vts, [None] * len(qqs),
                                   self.suffix_ones, self.lane_lo, [self.causal] + [None] * n_joint, tops)
        for i, blk in enumerate(blks):
            pairs = slice(i * N_PAIRS, (i + 1) * N_PAIRS)
            self._store(blk, outs[pairs], carries[pairs], True)
        least = functools.reduce(jnp.minimum, carries)
        for blk in blks:
            blk.least_ref[...] = least

    def _single_block(self, blk, tiles, mask):
        kts, vts = tiles
        outs, carries = _sb_blocks(self._qq_tiles(blk), [[t] for t in kts], [[t] for t in vts],
                                   [blk.carry_ref[p] for p in range(N_PAIRS)],
                                   self.suffix_ones, self.lane_lo, [mask])
        self._store(blk, outs, carries, False)

    def _least_used(self, blk, rows=None):
        tq = self.tq

        def pick(x):
            return x if rows is None else jnp.concatenate([x[rows], x[tq + rows.start:tq + rows.stop]], axis=0)
        return jnp.min(functools.reduce(jnp.minimum, [pick(blk.carry_ref[p]) for p in range(N_PAIRS)]))

    def finish_partial_block(self, blk):
        @pl.when(self._least_used(blk, slice(TOP_ROWS, self.tq)) < USED_STICK_CUTOFF)
        def _():
            self._single_block(blk, blk.joint_tiles(blk.n_past - JOINT_PAST_BLOCKS), self.row >= TOP_ROWS)

    def walk(self, blk, next_block):
        def body(state):
            j, _ = state
            self._single_block(blk, blk.walk_tiles(j), None)
            return j - 1, self._least_used(blk)

        lax.while_loop(lambda s: (s[0] >= 0) & (s[1] < USED_STICK_CUTOFF), body,
                       (next_block, self._least_used(blk)))

    def finish(self, blks, saw_partial_block, next_blocks):
        least_used = jnp.min(functools.reduce(jnp.minimum, [blk.least_ref[...] for blk in blks]))

        @pl.when(least_used < USED_STICK_CUTOFF)
        def _():
            for blk, partial in zip(blks, saw_partial_block):
                if partial is True:
                    self.finish_partial_block(blk)
                elif partial is not False:
                    pl.when(partial)(functools.partial(self.finish_partial_block, blk))
            for blk, next_block in zip(blks, next_blocks):
                self.walk(blk, next_block)


def _sb_prompt_kernel(q_ref, kt_ref, vt_ref, o_ref, carry_ref):
    def tiles(j):
        return ([kt_ref[j, rows, :] for rows in _PAIR_SLICES],
                [vt_ref[j, rows, :] for rows in _PAIR_SLICES])

    step = pl.program_id(1)
    sb = _StickBreaking(KEY_BLOCK, 0)
    blks = []
    for r in range(Q_BLOCKS_PER_STEP):
        i = step * Q_BLOCKS_PER_STEP + r
        rows = pl.ds(r * KEY_BLOCK, KEY_BLOCK)
        blks.append(_QueryBlock(q_ref.at[rows, :], o_ref.at[rows, :], carry_ref.at[r], i,
                                functools.partial(tiles, i), tiles, tiles))

    assert Q_BLOCKS_PER_STEP >= JOINT_PAST_BLOCKS > 1

    @pl.when(step > 0)
    def _():
        for first in range(0, Q_BLOCKS_PER_STEP, Q_BLOCKS_PER_PASS):
            sb.first_step(blks[first:first + Q_BLOCKS_PER_PASS], JOINT_PAST_BLOCKS)

    @pl.when(step == 0)
    def _():
        for r, blk in enumerate(blks):
            sb.first_step([blk], min(r, JOINT_PAST_BLOCKS))

    sb.finish(blks,
              [True if r >= JOINT_PAST_BLOCKS else step > 0 for r in range(Q_BLOCKS_PER_STEP)],
              [jnp.where(step > 0, blk.n_past - JOINT_PAST_BLOCKS, r - min(r, JOINT_PAST_BLOCKS)) - 1
               for r, blk in enumerate(blks)])


def _sb_prompt(q, kt3, vt3, batch, seq):
    nq = seq // (KEY_BLOCK * Q_BLOCKS_PER_STEP)
    blk = pl.BlockSpec((KEY_BLOCK * Q_BLOCKS_PER_STEP, SB_WIDTH), lambda b, i: (b * nq + i, 0))
    keys = pl.BlockSpec((seq // KEY_BLOCK, SB_WIDTH, KEY_BLOCK), lambda b, i: (b, 0, 0))
    return pl.pallas_call(
        _sb_prompt_kernel,
        grid=(batch, nq),
        in_specs=[blk, keys, keys],
        out_specs=blk,
        out_shape=jax.ShapeDtypeStruct((batch * seq, SB_WIDTH), F32),
        scratch_shapes=[pltpu.VMEM((Q_BLOCKS_PER_STEP, N_PAIRS + 1, 2 * KEY_BLOCK, LANES), F32)],
        compiler_params=_params(2),
        name="sb_prompt",
    )(q, kt3, vt3)


def _sb_sample_kernel(q_ref, kd_ref, vd_ref, kr_ref, vr_ref, kc_hbm, vc_hbm, o_ref,
                      carry_ref, kbuf_ref, vbuf_ref, sem_ref, *, past_len):
    tq = q_ref.shape[0]
    n_past = past_len // KEY_BLOCK
    n_recent = kr_ref.shape[2] // KEY_BLOCK
    b = pl.program_id(0)

    def diag_tiles():
        return ([kd_ref[0, rows, :] for rows in _PAIR_SLICES],
                [vd_ref[0, rows, :] for rows in _PAIR_SLICES])

    def recent_tiles(j):
        first = (j - (n_past - n_recent)) * KEY_BLOCK
        keys = slice(first, first + KEY_BLOCK)
        return ([kr_ref[0, rows, keys].astype(BF16) for rows in _PAIR_SLICES],
                [vr_ref[0, rows, keys].astype(BF16) for rows in _PAIR_SLICES])

    def cache_tiles(j):
        keys = pl.ds(pl.multiple_of(j * KEY_BLOCK, KEY_BLOCK), KEY_BLOCK)
        copies = [pltpu.make_async_copy(hbm.at[b, :, keys], buf, sem_ref.at[s])
                  for s, (hbm, buf) in enumerate(((kc_hbm, kbuf_ref), (vc_hbm, vbuf_ref)))]
        for c in copies:
            c.start()
        for c in copies:
            c.wait()
        return ([kbuf_ref[rows, :].astype(BF16) for rows in _PAIR_SLICES],
                [vbuf_ref[rows, :].astype(BF16) for rows in _PAIR_SLICES])

    assert n_recent == JOINT_PAST_BLOCKS > 1
    sb = _StickBreaking(tq, (b % (KEY_BLOCK // tq)) * tq)
    blk = _QueryBlock(q_ref, o_ref, carry_ref, n_past, diag_tiles, recent_tiles, cache_tiles)
    sb.first_step([blk], JOINT_PAST_BLOCKS)
    sb.finish([blk], [True], [n_past - JOINT_PAST_BLOCKS - 1])


def _sb_sample(q, kt3, vt3, cache_kt, cache_vt, t_new, first_row):
    batch, _, past_len = cache_kt.shape
    n_recent = JOINT_PAST_BLOCKS * KEY_BLOCK
    seqs_per_block = KEY_BLOCK // t_new
    new = pl.BlockSpec((1, SB_WIDTH, KEY_BLOCK),
                       lambda b: (first_row // KEY_BLOCK + b // seqs_per_block, 0, 0))
    recent = pl.BlockSpec((1, SB_WIDTH, n_recent), lambda b: (b, 0, past_len // n_recent - 1))
    hbm = pl.BlockSpec(memory_space=pl.ANY)
    return pl.pallas_call(
        functools.partial(_sb_sample_kernel, past_len=past_len),
        grid=(batch,),
        in_specs=[pl.BlockSpec((t_new, SB_WIDTH), lambda b: (first_row // t_new + b, 0)),
                  new, new, recent, recent, hbm, hbm],
        out_specs=pl.BlockSpec((t_new, SB_WIDTH), lambda b: (b, 0)),
        out_shape=jax.ShapeDtypeStruct((batch * t_new, SB_WIDTH), F32),
        scratch_shapes=[pltpu.VMEM((N_PAIRS + 1, 2 * t_new, LANES), F32),
                        pltpu.VMEM((SB_WIDTH, KEY_BLOCK), F32),
                        pltpu.VMEM((SB_WIDTH, KEY_BLOCK), F32),
                        pltpu.SemaphoreType.DMA((2,))],
        compiler_params=_params(1),
        name="sb_sample",
    )(q, kt3, vt3, cache_kt, cache_vt, cache_kt, cache_vt)


def _mix_rows(x_ref, op_ref, os_ref, u_ref, gn_ref, ws_ref, bias_ref, gsb_ref, gsgu_ref, wout_ref,
              post_ref, sgu_ref, rows):
    lane_lo = lax.broadcasted_iota(jnp.int32, (1, LANES), 1) < HEAD_DIM
    for p in range(N_PAIRS):
        cols = slice(p * LANES, (p + 1) * LANES)
        for first in range(rows.start, rows.stop, SGU_CHUNK):
            chunk = slice(first, first + SGU_CHUNK)
            gg = _pair_rows(gn_ref[chunk, cols].astype(BF16), lane_lo)
            mixed = jnp.dot(ws_ref[0, p], gg, preferred_element_type=F32) + bias_ref[0, :, cols]
            sgu_ref[chunk, cols] = u_ref[chunk, cols] * mixed
    o_sb = jnp.where(_is_sample_step(), os_ref[rows, :], op_ref[rows, :])
    merged = jnp.concatenate([_rms(o_sb, gsb_ref[...]).astype(BF16),
                              _rms(sgu_ref[rows, :], gsgu_ref[...]).astype(BF16)], axis=1)
    y = jnp.dot(merged, wout_ref[...], preferred_element_type=F32)
    return x_ref[rows, :] + _rms(y, post_ref[...])


def _mix_ffn_kernel(x_ref, op_ref, os_ref, u_ref, gn_ref, ws_ref, bias_ref, gsb_ref, gsgu_ref,
                    wout_ref, mix_post_ref, pre_ref, post_ref, wg_hbm, wu_hbm, wd_hbm,
                    yp_ref, ys_ref, sgu_ref,
                    wg_ref, wu_ref, wd_ref, stage_g_ref, stage_u_ref, stage_d_ref, sem_ref, act_ref):
    _load_ffn_weights(wg_hbm, wu_hbm, wd_hbm, wg_ref, wu_ref, wd_ref,
                      (stage_g_ref, stage_u_ref, stage_d_ref), sem_ref)

    def mix(rows):
        return _mix_rows(x_ref, op_ref, os_ref, u_ref, gn_ref, ws_ref, bias_ref, gsb_ref, gsgu_ref,
                         wout_ref, mix_post_ref, sgu_ref, rows)

    groups = FFN_ROW_GROUPS
    x = mix(groups[0])
    for g, rows in enumerate(groups):
        stages = _ffn_stages(x, pre_ref, post_ref, wg_ref, wu_ref, wd_ref, act_ref, rows)
        stages[0]()
        stages[1]()
        if g + 1 < len(groups):
            x = mix(groups[g + 1])
        for stage in stages[2:]:
            y = stage()
        yp_ref[rows, :] = y

    @pl.when(_is_sample_step())
    def _():
        ys_ref[...] = yp_ref[...]


def _mix_ffn(x, o_prompt, o_sample, u, gn, ws_pairs, bias, g_sb, g_sgu, w_out, mix_post_g,
             pre_g, post_g, wg, wu, wd, n_prompt_tiles):
    row = _merged_tile_spec(D_MODEL, n_prompt_tiles)
    half = _merged_tile_spec(SB_WIDTH, n_prompt_tiles)
    which = lambda s: jnp.where(s == 0, 1, 0)
    sds = lambda rows: jax.ShapeDtypeStruct((rows, D_MODEL), F32)
    return pl.pallas_call(
        _mix_ffn_kernel,
        grid=(n_prompt_tiles + 1,),
        in_specs=[row, _prompt_tile_spec(SB_WIDTH), _sample_tile_spec(SB_WIDTH), half, half,
                  pl.BlockSpec((1,) + ws_pairs.shape[1:], lambda s: (which(s), 0, 0, 0)),
                  pl.BlockSpec((1,) + bias.shape[1:], lambda s: (which(s), 0, 0)),
                  _const_spec((1, SB_WIDTH)), _const_spec((1, SGU_WIDTH)),
                  _const_spec((D_MODEL, D_MODEL)), _const_spec((1, D_MODEL))] + _FFN_WEIGHT_SPECS,
        out_specs=[_prompt_tile_spec(D_MODEL), _sample_tile_spec(D_MODEL)],
        out_shape=[sds(n_prompt_tiles * ROW_TILE), sds(ROW_TILE)],
        scratch_shapes=[pltpu.VMEM((ROW_TILE, SGU_WIDTH), F32)] + _FFN_SCRATCH,
        compiler_params=_params(1),
        name="mix_ffn",
    )(x, o_prompt, o_sample, u, gn, ws_pairs, bias, g_sb, g_sgu, w_out, mix_post_g,
      pre_g, post_g, wg, wu, wd)


def _sgu_operands(w_s, b_s, t_new):
    i = jnp.arange(SGU_CHUNK)
    mask = (i[None, :] // SGU_CAUSAL_CHUNK) <= (i[:, None] // SGU_CAUSAL_CHUNK)
    w_prompt = w_s * mask[None].astype(w_s.dtype)
    reps = SGU_CHUNK // t_new
    corner = w_prompt[:, :t_new, :t_new]
    eye = jnp.eye(reps, dtype=w_s.dtype)
    w_sample = jnp.einsum('ab,gij->gaibj', eye, corner).reshape(w_s.shape)
    groups = w_s.shape[0]

    def pairs(w):
        w = w.reshape(groups // 2, 2, SGU_CHUNK, SGU_CHUNK).transpose(0, 2, 1, 3)
        return w.reshape(groups // 2, SGU_CHUNK, 2 * SGU_CHUNK)

    def bias(b):
        return jnp.repeat(b.T, SGU_WIDTH // groups, axis=1)

    ws = jnp.stack([pairs(w_prompt), pairs(w_sample)]).astype(BF16)
    return ws, jnp.stack([bias(b_s), bias(jnp.tile(b_s[:, :t_new], (1, reps)))])


def kernel(x_prompt, x_sample, cache_k_sb, cache_v_sb, ffn1_pre_g, ffn1_post_g, ffn1_w_gate, ffn1_w_up, ffn1_w_down, mix_pre_g, mix_post_g, w_in, sgu_ln_g, sgu_ln_b, sgu_w_s, sgu_b_s, g_out_sb, g_out_sgu, w_out, ffn2_pre_g, ffn2_post_g, ffn2_w_gate, ffn2_w_up, ffn2_w_down):
    depth, batch, seq = w_in.shape[0], x_prompt.shape[0], x_prompt.shape[1]
    dec_batch, t_new, past_len = x_sample.shape[0], x_sample.shape[1], cache_k_sb.shape[2]
    assert depth == 1 and dec_batch * t_new == ROW_TILE and seq % ROW_TILE == 0
    assert SGU_CHUNK % t_new == 0 and KEY_BLOCK % t_new == 0
    assert past_len % (JOINT_PAST_BLOCKS * KEY_BLOCK) == 0
    n_prompt_tiles = batch * seq // ROW_TILE

    mat = lambda w: w[0].astype(BF16)
    vec = lambda g: g[0][None, :]

    x = _ffn(x_prompt.reshape(batch * seq, D_MODEL), x_sample.reshape(ROW_TILE, D_MODEL),
             vec(ffn1_pre_g), vec(ffn1_post_g), ffn1_w_gate[0], ffn1_w_up[0], ffn1_w_down[0],
             n_prompt_tiles)
    q, u, gn, kt3, vt3, kt_p, vt_p, kt_s, vt_s = _inproj(
        x, vec(mix_pre_g), w_in[0], vec(sgu_ln_g), vec(sgu_ln_b), n_prompt_tiles, seq // ROW_TILE)

    o_prompt = _sb_prompt(q, kt3, vt3, batch, seq)
    cache_t = lambda c: c[0].transpose(0, 2, 3, 1).reshape(dec_batch, SB_WIDTH, past_len)
    o_sample = _sb_sample(q, kt3, vt3, cache_t(cache_k_sb), cache_t(cache_v_sb), t_new, batch * seq)

    ws_pairs, bias = _sgu_operands(sgu_w_s[0], sgu_b_s[0], t_new)
    y_p, y_s = _mix_ffn(x, o_prompt, o_sample, u, gn, ws_pairs, bias, vec(g_out_sb), vec(g_out_sgu),
                        mat(w_out), vec(mix_post_g), vec(ffn2_pre_g), vec(ffn2_post_g),
                        ffn2_w_gate[0], ffn2_w_up[0], ffn2_w_down[0], n_prompt_tiles)

    def prompt_heads(t):
        return t.reshape(batch, N_HEADS, HEAD_DIM, seq).transpose(0, 3, 1, 2)[None]

    def sample_heads(t):
        return t.reshape(N_HEADS, HEAD_DIM, dec_batch, t_new).transpose(2, 3, 0, 1)[None]

    g_s = gn[batch * seq:].reshape(1, dec_batch, t_new, N_HEADS, HEAD_DIM)
    return (y_p.reshape(batch, seq, D_MODEL), y_s.reshape(dec_batch, t_new, D_MODEL),
            prompt_heads(kt_p), prompt_heads(vt_p), sample_heads(kt_s), sample_heads(vt_s), g_s)
```

```python
import functools

import jax
import jax.numpy as jnp
from jax import lax
from jax.experimental import pallas as pl
from jax.experimental.pallas import tpu as pltpu

D_MODEL = 1024
D_FF = 2816
SB_WIDTH = 512
SGU_WIDTH = 512
HEAD_DIM = 64
N_HEADS = SB_WIDTH // HEAD_DIM
SGU_CHUNK = 128
SGU_CAUSAL_CHUNK = 64
FFN_RES = 0.5
EPS = 1e-6

LANES = 128
N_PAIRS = SB_WIDTH // LANES
KEY_BLOCK = 128
JOINT_PAST_BLOCKS = 2
TOP_ROWS = 32
Q_BLOCKS_PER_STEP = 4
Q_BLOCKS_PER_PASS = 2
FF_CHUNKS = (1536, 1280)
ROW_TILE = 512
WEIGHT_CHUNK_ROWS = 128
FFN_ROW_GROUPS = (slice(0, 256), slice(256, 512))
VMEM_LIMIT = 56 * 1024 * 1024

LOG2_E = 1.4426950408889634
USED_STICK_CUTOFF = 105.0 * LOG2_E
HIDDEN_LOGIT = -1e30

F32 = jnp.float32
BF16 = jnp.bfloat16


def _rms(x, g):
    return x * lax.rsqrt(jnp.mean(x * x, axis=-1, keepdims=True) + EPS) * g


def _gelu(x):
    return 0.5 * x * (1.0 + lax.erf(x * (0.5 ** 0.5)))


def _const_spec(shape):
    return pl.BlockSpec(shape, lambda *_: (0,) * len(shape), pipeline_mode=pl.Buffered(1))


def _params(n_axes):
    return pltpu.CompilerParams(dimension_semantics=("arbitrary",) * n_axes,
                                vmem_limit_bytes=VMEM_LIMIT)


def _is_sample_step():
    return pl.program_id(0) == 0


def _merged_tile_spec(width, n_prompt_tiles):
    return pl.BlockSpec((ROW_TILE, width), lambda s: ((s + n_prompt_tiles) % (n_prompt_tiles + 1), 0))


def _prompt_tile_spec(width, rows=ROW_TILE):
    return pl.BlockSpec((rows, width), lambda s: (jnp.maximum(s - 1, 0), 0))


def _sample_tile_spec(width, rows=ROW_TILE):
    return pl.BlockSpec((rows, width), lambda s: (0, 0))


def _ffn_stages(x, pre_ref, post_ref, wg_ref, wu_ref, wd_ref, act_ref, rows):
    state = {}

    def hidden():
        state['h'] = _rms(x, pre_ref[...]).astype(BF16)

    def chunk(cols):
        gate = jnp.dot(state['h'], wg_ref[:, cols], preferred_element_type=F32)
        up = jnp.dot(state['h'], wu_ref[:, cols], preferred_element_type=F32)
        act_ref[rows, cols] = (gate * jax.nn.sigmoid(gate) * up).astype(BF16)

    def finish():
        f = jnp.dot(act_ref[rows, :], wd_ref[...], preferred_element_type=F32)
        return x + FFN_RES * _rms(f, post_ref[...])

    stages, start = [hidden], 0
    for width in FF_CHUNKS:
        stages.append(functools.partial(chunk, slice(start, start + width)))
        start += width
    return stages + [finish]


def _ffn_rows(x, pre_ref, post_ref, wg_ref, wu_ref, wd_ref, act_ref, rows):
    for stage in _ffn_stages(x, pre_ref, post_ref, wg_ref, wu_ref, wd_ref, act_ref, rows):
        y = stage()
    return y


_HBM_SPEC = pl.BlockSpec(memory_space=pl.ANY)
_FFN_WEIGHT_SPECS = [_const_spec((1, D_MODEL)), _const_spec((1, D_MODEL)), _HBM_SPEC, _HBM_SPEC, _HBM_SPEC]
_FFN_SCRATCH = [pltpu.VMEM((D_MODEL, D_FF), BF16), pltpu.VMEM((D_MODEL, D_FF), BF16),
                pltpu.VMEM((D_FF, D_MODEL), BF16),
                pltpu.VMEM((2, WEIGHT_CHUNK_ROWS, D_FF), F32),
                pltpu.VMEM((2, WEIGHT_CHUNK_ROWS, D_FF), F32),
                pltpu.VMEM((2, WEIGHT_CHUNK_ROWS, D_MODEL), F32),
                pltpu.SemaphoreType.DMA((6,)),
                pltpu.VMEM((ROW_TILE, D_FF), BF16)]


def _load_bf16(streams):
    def n_chunks(stream):
        return stream[0].shape[0] // stream[2].shape[1]

    def copy(stream, i):
        hbm_ref, _, stage_ref, sem_ref, first_sem = stream
        chunk = stage_ref.shape[1]
        return pltpu.make_async_copy(hbm_ref.at[pl.ds(i * chunk, chunk), :], stage_ref.at[i % 2],
                                     sem_ref.at[first_sem + i % 2])

    for stream in streams:
        copy(stream, 0).start()
    order = sorted(((i + 1) / n_chunks(stream), k, i)
                   for k, stream in enumerate(streams) for i in range(n_chunks(stream)))
    for _, k, i in order:
        stream = streams[k]
        if i + 1 < n_chunks(stream):
            copy(stream, i + 1).start()
        copy(stream, i).wait()
        chunk = stream[2].shape[1]
        stream[1][pl.ds(i * chunk, chunk), :] = stream[2][i % 2].astype(BF16)


def _load_ffn_weights(wg_hbm, wu_hbm, wd_hbm, wg_ref, wu_ref, wd_ref, stage_refs, sem_ref):
    @pl.when(pl.program_id(0) == 0)
    def _():
        _load_bf16([(hbm, dst, stage, sem_ref, 2 * k) for k, (hbm, dst, stage) in enumerate(
            zip((wg_hbm, wu_hbm, wd_hbm), (wg_ref, wu_ref, wd_ref), stage_refs))])


def _ffn_kernel(xp_ref, xs_ref, pre_ref, post_ref, wg_hbm, wu_hbm, wd_hbm, o_ref,
                wg_ref, wu_ref, wd_ref, stage_g_ref, stage_u_ref, stage_d_ref, sem_ref, act_ref):
    _load_ffn_weights(wg_hbm, wu_hbm, wd_hbm, wg_ref, wu_ref, wd_ref,
                      (stage_g_ref, stage_u_ref, stage_d_ref), sem_ref)
    for rows in FFN_ROW_GROUPS:
        x = jnp.where(_is_sample_step(), xs_ref[rows, :], xp_ref[rows, :])
        o_ref[rows, :] = _ffn_rows(x, pre_ref, post_ref, wg_ref, wu_ref, wd_ref, act_ref, rows)


def _ffn(x_prompt, x_sample, pre_g, post_g, wg, wu, wd, n_prompt_tiles):
    n_tiles = n_prompt_tiles + 1
    return pl.pallas_call(
        _ffn_kernel,
        grid=(n_tiles,),
        in_specs=[_prompt_tile_spec(D_MODEL), _sample_tile_spec(D_MODEL)] + _FFN_WEIGHT_SPECS,
        out_specs=_merged_tile_spec(D_MODEL, n_prompt_tiles),
        out_shape=jax.ShapeDtypeStruct((n_tiles * ROW_TILE, D_MODEL), F32),
        scratch_shapes=_FFN_SCRATCH,
        compiler_params=_params(1),
        name="ffn",
    )(x_prompt, x_sample, pre_g, post_g, wg, wu, wd)


def _inproj_kernel(x_ref, g_ref, win_hbm, lng_ref, lnb_ref,
                   q_ref, u_ref, gn_ref, kt3_ref, vt3_ref, ktp_ref, vtp_ref, kts_ref, vts_ref,
                   wqug_ref, wkvt_ref, stage_ref, sem_ref):
    w = SB_WIDTH

    @pl.when(pl.program_id(0) == 0)
    def _():
        chunk = stage_ref.shape[1]
        n_chunks = win_hbm.shape[0] // chunk

        def copy(i):
            return pltpu.make_async_copy(win_hbm.at[pl.ds(i * chunk, chunk), :], stage_ref.at[i % 2],
                                         sem_ref.at[i % 2])

        copy(0).start()
        for i in range(n_chunks):
            if i + 1 < n_chunks:
                copy(i + 1).start()
            copy(i).wait()
            rows = pl.ds(i * chunk, chunk)
            part = stage_ref[i % 2]
            wqug_ref[rows, 0:w] = part[:, 0:w].astype(BF16)
            wqug_ref[rows, w:] = part[:, 3 * w:].astype(BF16)
            wkvt_ref[rows, :] = part[:, w:3 * w].astype(BF16)

    h = _rms(x_ref[...], g_ref[...]).astype(BF16)
    z = jnp.dot(h, wqug_ref[:, w:], preferred_element_type=F32)
    u_ref[...] = _gelu(z[:, :SGU_WIDTH])
    ge = _gelu(z[:, SGU_WIDTH:])
    xc = ge - jnp.mean(ge, axis=-1, keepdims=True)
    y = xc * lax.rsqrt(jnp.mean(xc * xc, axis=-1, keepdims=True) + EPS)
    gn_ref[...] = y * lng_ref[...] + lnb_ref[...]
    kvt = lax.dot_general(wkvt_ref[...], h, (((0,), (1,)), ((), ())), preferred_element_type=F32)
    for t_ref, t3_ref, rows in ((ktp_ref, kt3_ref, slice(0, w)), (vtp_ref, vt3_ref, slice(w, 2 * w))):
        t = kvt[rows]
        t_ref[...] = t
        for c in range(ROW_TILE // KEY_BLOCK):
            t3_ref[c] = t[:, c * KEY_BLOCK:(c + 1) * KEY_BLOCK].astype(BF16)
    q = jnp.dot(h, wqug_ref[:, :w], preferred_element_type=F32)
    q_ref[...] = (q * (HEAD_DIM ** -0.5 * LOG2_E)).astype(BF16)

    @pl.when(_is_sample_step())
    def _():
        kts_ref[...] = ktp_ref[...]
        vts_ref[...] = vtp_ref[...]


def _inproj(x, g, w_in, ln_g, ln_b, n_prompt_tiles, prompt_tiles_per_row):
    n = x.shape[0]
    half = _merged_tile_spec(SB_WIDTH, n_prompt_tiles)
    blocks_per_tile = ROW_TILE // KEY_BLOCK
    key_blocks = pl.BlockSpec((blocks_per_tile, SB_WIDTH, KEY_BLOCK),
                              lambda s: ((s + n_prompt_tiles) % (n_prompt_tiles + 1), 0, 0))

    def prompt_t_index(s):
        tile = jnp.maximum(s - 1, 0)
        return tile // prompt_tiles_per_row, tile % prompt_tiles_per_row

    prompt_t = pl.BlockSpec((SB_WIDTH, ROW_TILE), prompt_t_index)
    prompt_t_sds = jax.ShapeDtypeStruct(
        (n_prompt_tiles // prompt_tiles_per_row * SB_WIDTH, prompt_tiles_per_row * ROW_TILE), F32)
    sample_t_sds = jax.ShapeDtypeStruct((SB_WIDTH, ROW_TILE), F32)
    f32_half = jax.ShapeDtypeStruct((n, SB_WIDTH), F32)
    key_blocks_sds = jax.ShapeDtypeStruct((n // KEY_BLOCK, SB_WIDTH, KEY_BLOCK), BF16)
    return pl.pallas_call(
        _inproj_kernel,
        grid=(n // ROW_TILE,),
        in_specs=[_merged_tile_spec(D_MODEL, n_prompt_tiles), _const_spec((1, D_MODEL)),
                  _HBM_SPEC, _const_spec((1, SGU_WIDTH)), _const_spec((1, SGU_WIDTH))],
        out_specs=[half] * 3 + [key_blocks] * 2 + [prompt_t] * 2 + [_sample_tile_spec(ROW_TILE, SB_WIDTH)] * 2,
        out_shape=[jax.ShapeDtypeStruct((n, SB_WIDTH), BF16), f32_half, f32_half,
                   key_blocks_sds, key_blocks_sds, prompt_t_sds, prompt_t_sds,
                   sample_t_sds, sample_t_sds],
        scratch_shapes=[pltpu.VMEM((D_MODEL, SB_WIDTH + 2 * SGU_WIDTH), BF16),
                        pltpu.VMEM((2 * SB_WIDTH, D_MODEL), BF16),
                        pltpu.VMEM((2, WEIGHT_CHUNK_ROWS, w_in.shape[1]), F32),
                        pltpu.SemaphoreType.DMA((2,))],
        compiler_params=_params(1),
        name="inproj",
    )(x, g, w_in, ln_g, ln_b)


def _pair_rows(x, lane_lo):
    zero = jnp.zeros_like(x)
    return jnp.concatenate([jnp.where(lane_lo, x, zero), jnp.where(lane_lo, zero, x)], axis=0)


def _sb_blocks(qqs, kts, vts, carries, suffix_ones, lane_lo, masks, tops=None):
    n_blocks = len(masks)
    tops = tops or [None] * n_blocks
    t = qqs[0].shape[0] // 2
    nt_dims = (((1,), (1,)), ((), ()))

    def head_rows(x, top):
        return x if top is None else jnp.concatenate([x[:top], x[t:t + top]], axis=0)

    def weighted_values(a, vt):
        both = lax.dot_general(a, vt, nt_dims, preferred_element_type=F32)
        rows = a.shape[0] // 2
        return jnp.where(lane_lo, both[:rows], both[rows:])

    zs = []
    for qq, kts_p in zip(qqs, kts):
        zs_p, b = [], 0
        while b < n_blocks:
            if b + 1 < n_blocks and tops[b] is None and tops[b + 1] is None:
                z2 = jnp.dot(qq, jnp.concatenate([kts_p[b], kts_p[b + 1]], axis=1),
                             preferred_element_type=F32)
                zs_p += [z2[:, :KEY_BLOCK], z2[:, KEY_BLOCK:]]
                b += 2
            else:
                zs_p.append(jnp.dot(head_rows(qq, tops[b]), kts_p[b], preferred_element_type=F32))
                b += 1
        zs.append(zs_p)
    useds, splits = [], []
    for zs_p in zs:
        useds.append([])
        splits.append([])
        for i, mask in enumerate(masks):
            z = zs_p[i]
            if mask is not None:
                z = zs_p[i] = jnp.where(mask, z, HIDDEN_LOGIT)
            used = jnp.maximum(z, 0.0) + jnp.log(1.0 + jnp.exp2(-jnp.abs(z))) * LOG2_E
            hi = used.astype(BF16)
            lo = (used - hi.astype(F32)).astype(BF16)
            useds[-1].append(used)
            splits[-1].append(jnp.concatenate([hi, lo], axis=1))
    sums = [[jnp.dot(s, suffix_ones, preferred_element_type=F32) for s in splits_p]
            for splits_p in splits]
    outs, new_carries = [], []
    for zs_p, useds_p, sums_p, vts_p, carry in zip(zs, useds, sums, vts, carries):
        weights, values_t, top_outs = [], [], []
        for z, used, s, vt, top in zip(zs_p, useds_p, sums_p, vts_p, tops):
            later = s[:, :KEY_BLOCK] if carry is None else s[:, :KEY_BLOCK] + head_rows(carry, top)
            a = jnp.exp2(z - used - later).astype(BF16)
            row_sum = s[:, KEY_BLOCK:]
            if top is None:
                weights.append(a)
                values_t.append(vt)
                carry = row_sum if carry is None else carry + row_sum
            else:
                top_outs.append(weighted_values(a, vt))
                carry = jnp.concatenate([carry[:top] + row_sum[:top], carry[top:t],
                                         carry[t:t + top] + row_sum[top:], carry[t + top:]], axis=0)
        out = weighted_values(jnp.concatenate(weights, axis=1), jnp.concatenate(values_t, axis=1))
        for o_top in top_outs:
            top = o_top.shape[0]
            out = jnp.concatenate([out[:top] + o_top, out[top:]], axis=0)
        outs.append(out)
        new_carries.append(carry)
    return outs, new_carries


def _suffix_ones():
    r = lax.broadcasted_iota(jnp.int32, (2 * KEY_BLOCK, 2 * KEY_BLOCK), 0) % KEY_BLOCK
    c = lax.broadcasted_iota(jnp.int32, (2 * KEY_BLOCK, 2 * KEY_BLOCK), 1)
    return jnp.where((c >= KEY_BLOCK) | (r > c), 1.0, 0.0).astype(BF16)


_PAIR_SLICES = [slice(p * LANES, (p + 1) * LANES) for p in range(N_PAIRS)]


class _QueryBlock:
    def __init__(self, q_ref, o_ref, carry_ref, n_past, diag_tiles, joint_tiles, walk_tiles):
        self.q_ref, self.o_ref, self.carry_ref, self.n_past = q_ref, o_ref, carry_ref, n_past
        self.least_ref = carry_ref.at[N_PAIRS]
        self.diag_tiles, self.joint_tiles, self.walk_tiles = diag_tiles, joint_tiles, walk_tiles


class _StickBreaking:
    def __init__(self, tq, diag_key_offset):
        self.tq = tq
        self.lane_lo = lax.broadcasted_iota(jnp.int32, (1, LANES), 1) < HEAD_DIM
        self.suffix_ones = _suffix_ones()
        self.row = lax.broadcasted_iota(jnp.int32, (2 * tq, KEY_BLOCK), 0) % tq
        col = lax.broadcasted_iota(jnp.int32, (2 * tq, KEY_BLOCK), 1) - diag_key_offset
        self.causal = (col >= 0) & (col < self.row)

    def _qq_tiles(self, blk):
        return [_pair_rows(blk.q_ref[:, cols], self.lane_lo) for cols in _PAIR_SLICES]

    def _store(self, blk, outs, carries, first):
        for p, cols in enumerate(_PAIR_SLICES):
            if first:
                blk.o_ref[:, cols] = outs[p]
            else:
                blk.o_ref[:, cols] += outs[p]
            blk.carry_ref[p] = carries[p]

    def first_step(self, blks, n_joint):
        qqs, kts, vts = [], [], []
        for blk in blks:
            kd, vd = blk.diag_tiles()
            kts_b, vts_b = [[t] for t in kd], [[t] for t in vd]
            for b in range(n_joint):
                kj, vj = blk.joint_tiles(blk.n_past - 1 - b)
                for p in range(N_PAIRS):
                    kts_b[p].append(kj[p])
                    vts_b[p].append(vj[p])
            qqs += self._qq_tiles(blk)
            kts += kts_b
            vts += vts_b
        tops = [None] * n_joint + ([TOP_ROWS] if n_joint > 1 else [None])
        outs, carries = _sb_blocks(qqs, kts, vts, [None] * len(qqs),
                                   self.suffix_ones, self.lane_lo, [self.causal] + [None] * n_joint, tops)
        for i, blk in enumerate(blks):
            pairs = slice(i * N_PAIRS, (i + 1) * N_PAIRS)
            self._store(blk, outs[pairs], carries[pairs], True)
        least = functools.reduce(jnp.minimum, carries)
        for blk in blks:
            blk.least_ref[...] = least

    def _single_block(self, blk, tiles, mask):
        kts, vts = tiles
        outs, carries = _sb_blocks(self._qq_tiles(blk), [[t] for t in kts], [[t] for t in vts],
                                   [blk.carry_ref[p] for p in range(N_PAIRS)],
                                   self.suffix_ones, self.lane_lo, [mask])
        self._store(blk, outs, carries, False)

    def _least_used(self, blk, rows=None):
        tq = self.tq

        def pick(x):
            return x if rows is None else jnp.concatenate([x[rows], x[tq + rows.start:tq + rows.stop]], axis=0)
        return jnp.min(functools.reduce(jnp.minimum, [pick(blk.carry_ref[p]) for p in range(N_PAIRS)]))

    def finish_partial_block(self, blk):
        @pl.when(self._least_used(blk, slice(TOP_ROWS, self.tq)) < USED_STICK_CUTOFF)
        def _():
            self._single_block(blk, blk.joint_tiles(blk.n_past - JOINT_PAST_BLOCKS), self.row >= TOP_ROWS)

    def walk(self, blk, next_block):
        def body(state):
            j, _ = state
            self._single_block(blk, blk.walk_tiles(j), None)
            return j - 1, self._least_used(blk)

        lax.while_loop(lambda s: (s[0] >= 0) & (s[1] < USED_STICK_CUTOFF), body,
                       (next_block, self._least_used(blk)))

    def finish(self, blks, saw_partial_block, next_blocks):
        least_used = jnp.min(functools.reduce(jnp.minimum, [blk.least_ref[...] for blk in blks]))

        @pl.when(least_used < USED_STICK_CUTOFF)
        def _():
            for blk, partial in zip(blks, saw_partial_block):
                if partial is True:
                    self.finish_partial_block(blk)
                elif partial is not False:
                    pl.when(partial)(functools.partial(self.finish_partial_block, blk))
            for blk, next_block in zip(blks, next_blocks):
                self.walk(blk, next_block)


def _sb_prompt_kernel(q_ref, kt_ref, vt_ref, o_ref, carry_ref):
    def tiles(j):
        return ([kt_ref[j, rows, :] for rows in _PAIR_SLICES],
                [vt_ref[j, rows, :] for rows in _PAIR_SLICES])

    step = pl.program_id(1)
    sb = _StickBreaking(KEY_BLOCK, 0)
    blks = []
    for r in range(Q_BLOCKS_PER_STEP):
        i = step * Q_BLOCKS_PER_STEP + r
        rows = pl.ds(r * KEY_BLOCK, KEY_BLOCK)
        blks.append(_QueryBlock(q_ref.at[rows, :], o_ref.at[rows, :], carry_ref.at[r], i,
                                functools.partial(tiles, i), tiles, tiles))

    assert Q_BLOCKS_PER_STEP >= JOINT_PAST_BLOCKS > 1

    @pl.when(step > 0)
    def _():
        for first in range(0, Q_BLOCKS_PER_STEP, Q_BLOCKS_PER_PASS):
            sb.first_step(blks[first:first + Q_BLOCKS_PER_PASS], JOINT_PAST_BLOCKS)

    @pl.when(step == 0)
    def _():
        for r, blk in enumerate(blks):
            sb.first_step([blk], min(r, JOINT_PAST_BLOCKS))

    sb.finish(blks,
              [True if r >= JOINT_PAST_BLOCKS else step > 0 for r in range(Q_BLOCKS_PER_STEP)],
              [jnp.where(step > 0, blk.n_past - JOINT_PAST_BLOCKS, r - min(r, JOINT_PAST_BLOCKS)) - 1
               for r, blk in enumerate(blks)])


def _sb_prompt(q, kt3, vt3, batch, seq):
    nq = seq // (KEY_BLOCK * Q_BLOCKS_PER_STEP)
    blk = pl.BlockSpec((KEY_BLOCK * Q_BLOCKS_PER_STEP, SB_WIDTH), lambda b, i: (b * nq + i, 0))
    keys = pl.BlockSpec((seq // KEY_BLOCK, SB_WIDTH, KEY_BLOCK), lambda b, i: (b, 0, 0))
    return pl.pallas_call(
        _sb_prompt_kernel,
        grid=(batch, nq),
        in_specs=[blk, keys, keys],
        out_specs=blk,
        out_shape=jax.ShapeDtypeStruct((batch * seq, SB_WIDTH), F32),
        scratch_shapes=[pltpu.VMEM((Q_BLOCKS_PER_STEP, N_PAIRS + 1, 2 * KEY_BLOCK, LANES), F32)],
        compiler_params=_params(2),
        name="sb_prompt",
    )(q, kt3, vt3)


def _sb_sample_kernel(q_ref, kd_ref, vd_ref, kr_ref, vr_ref, kc_hbm, vc_hbm, o_ref,
                      carry_ref, kbuf_ref, vbuf_ref, sem_ref, *, past_len):
    tq = q_ref.shape[0]
    n_past = past_len // KEY_BLOCK
    n_recent = kr_ref.shape[2] // KEY_BLOCK
    b = pl.program_id(0)

    def diag_tiles():
        return ([kd_ref[0, rows, :] for rows in _PAIR_SLICES],
                [vd_ref[0, rows, :] for rows in _PAIR_SLICES])

    def recent_tiles(j):
        first = (j - (n_past - n_recent)) * KEY_BLOCK
        keys = slice(first, first + KEY_BLOCK)
        return ([kr_ref[0, rows, keys].astype(BF16) for rows in _PAIR_SLICES],
                [vr_ref[0, rows, keys].astype(BF16) for rows in _PAIR_SLICES])

    def cache_tiles(j):
        keys = pl.ds(pl.multiple_of(j * KEY_BLOCK, KEY_BLOCK), KEY_BLOCK)
        copies = [pltpu.make_async_copy(hbm.at[b, :, keys], buf, sem_ref.at[s])
                  for s, (hbm, buf) in enumerate(((kc_hbm, kbuf_ref), (vc_hbm, vbuf_ref)))]
        for c in copies:
            c.start()
        for c in copies:
            c.wait()
        return ([kbuf_ref[rows, :].astype(BF16) for rows in _PAIR_SLICES],
                [vbuf_ref[rows, :].astype(BF16) for rows in _PAIR_SLICES])

    assert n_recent == JOINT_PAST_BLOCKS > 1
    sb = _StickBreaking(tq, (b % (KEY_BLOCK // tq)) * tq)
    blk = _QueryBlock(q_ref, o_ref, carry_ref, n_past, diag_tiles, recent_tiles, cache_tiles)
    sb.first_step([blk], JOINT_PAST_BLOCKS)
    sb.finish([blk], [True], [n_past - JOINT_PAST_BLOCKS - 1])


def _sb_sample(q, kt3, vt3, cache_kt, cache_vt, t_new, first_row):
    batch, _, past_len = cache_kt.shape
    n_recent = JOINT_PAST_BLOCKS * KEY_BLOCK
    seqs_per_block = KEY_BLOCK // t_new
    new = pl.BlockSpec((1, SB_WIDTH, KEY_BLOCK),
                       lambda b: (first_row // KEY_BLOCK + b // seqs_per_block, 0, 0))
    recent = pl.BlockSpec((1, SB_WIDTH, n_recent), lambda b: (b, 0, past_len // n_recent - 1))
    hbm = pl.BlockSpec(memory_space=pl.ANY)
    return pl.pallas_call(
        functools.partial(_sb_sample_kernel, past_len=past_len),
        grid=(batch,),
        in_specs=[pl.BlockSpec((t_new, SB_WIDTH), lambda b: (first_row // t_new + b, 0)),
                  new, new, recent, recent, hbm, hbm],
        out_specs=pl.BlockSpec((t_new, SB_WIDTH), lambda b: (b, 0)),
        out_shape=jax.ShapeDtypeStruct((batch * t_new, SB_WIDTH), F32),
        scratch_shapes=[pltpu.VMEM((N_PAIRS + 1, 2 * t_new, LANES), F32),
                        pltpu.VMEM((SB_WIDTH, KEY_BLOCK), F32),
                        pltpu.VMEM((SB_WIDTH, KEY_BLOCK), F32),
                        pltpu.SemaphoreType.DMA((2,))],
        compiler_params=_params(1),
        name="sb_sample",
    )(q, kt3, vt3, cache_kt, cache_vt, cache_kt, cache_vt)


def _mix_rows(x_ref, op_ref, os_ref, u_ref, gn_ref, ws_ref, bias_ref, gsb_ref, gsgu_ref, wout_ref,
              post_ref, sgu_ref, rows):
    lane_lo = lax.broadcasted_iota(jnp.int32, (1, LANES), 1) < HEAD_DIM
    for p in range(N_PAIRS):
        cols = slice(p * LANES, (p + 1) * LANES)
        for first in range(rows.start, rows.stop, SGU_CHUNK):
            chunk = slice(first, first + SGU_CHUNK)
            gg = _pair_rows(gn_ref[chunk, cols].astype(BF16), lane_lo)
            mixed = jnp.dot(ws_ref[0, p], gg, preferred_element_type=F32) + bias_ref[0, :, cols]
            sgu_ref[chunk, cols] = u_ref[chunk, cols] * mixed
    o_sb = jnp.where(_is_sample_step(), os_ref[rows, :], op_ref[rows, :])
    merged = jnp.concatenate([_rms(o_sb, gsb_ref[...]).astype(BF16),
                              _rms(sgu_ref[rows, :], gsgu_ref[...]).astype(BF16)], axis=1)
    y = jnp.dot(merged, wout_ref[...], preferred_element_type=F32)
    return x_ref[rows, :] + _rms(y, post_ref[...])


def _mix_ffn_kernel(x_ref, op_ref, os_ref, u_ref, gn_ref, ws_ref, bias_ref, gsb_ref, gsgu_ref,
                    wout_ref, mix_post_ref, pre_ref, post_ref, wg_hbm, wu_hbm, wd_hbm,
                    yp_ref, ys_ref, sgu_ref,
                    wg_ref, wu_ref, wd_ref, stage_g_ref, stage_u_ref, stage_d_ref, sem_ref, act_ref):
    _load_ffn_weights(wg_hbm, wu_hbm, wd_hbm, wg_ref, wu_ref, wd_ref,
                      (stage_g_ref, stage_u_ref, stage_d_ref), sem_ref)

    def mix(rows):
        return _mix_rows(x_ref, op_ref, os_ref, u_ref, gn_ref, ws_ref, bias_ref, gsb_ref, gsgu_ref,
                         wout_ref, mix_post_ref, sgu_ref, rows)

    groups = FFN_ROW_GROUPS
    x = mix(groups[0])
    for g, rows in enumerate(groups):
        stages = _ffn_stages(x, pre_ref, post_ref, wg_ref, wu_ref, wd_ref, act_ref, rows)
        stages[0]()
        stages[1]()
        if g + 1 < len(groups):
            x = mix(groups[g + 1])
        for stage in stages[2:]:
            y = stage()
        yp_ref[rows, :] = y

    @pl.when(_is_sample_step())
    def _():
        ys_ref[...] = yp_ref[...]


def _mix_ffn(x, o_prompt, o_sample, u, gn, ws_pairs, bias, g_sb, g_sgu, w_out, mix_post_g,
             pre_g, post_g, wg, wu, wd, n_prompt_tiles):
    row = _merged_tile_spec(D_MODEL, n_prompt_tiles)
    half = _merged_tile_spec(SB_WIDTH, n_prompt_tiles)
    which = lambda s: jnp.where(s == 0, 1, 0)
    sds = lambda rows: jax.ShapeDtypeStruct((rows, D_MODEL), F32)
    return pl.pallas_call(
        _mix_ffn_kernel,
        grid=(n_prompt_tiles + 1,),
        in_specs=[row, _prompt_tile_spec(SB_WIDTH), _sample_tile_spec(SB_WIDTH), half, half,
                  pl.BlockSpec((1,) + ws_pairs.shape[1:], lambda s: (which(s), 0, 0, 0)),
                  pl.BlockSpec((1,) + bias.shape[1:], lambda s: (which(s), 0, 0)),
                  _const_spec((1, SB_WIDTH)), _const_spec((1, SGU_WIDTH)),
                  _const_spec((D_MODEL, D_MODEL)), _const_spec((1, D_MODEL))] + _FFN_WEIGHT_SPECS,
        out_specs=[_prompt_tile_spec(D_MODEL), _sample_tile_spec(D_MODEL)],
        out_shape=[sds(n_prompt_tiles * ROW_TILE), sds(ROW_TILE)],
        scratch_shapes=[pltpu.VMEM((ROW_TILE, SGU_WIDTH), F32)] + _FFN_SCRATCH,
        compiler_params=_params(1),
        name="mix_ffn",
    )(x, o_prompt, o_sample, u, gn, ws_pairs, bias, g_sb, g_sgu, w_out, mix_post_g,
      pre_g, post_g, wg, wu, wd)


def _sgu_operands(w_s, b_s, t_new):
    i = jnp.arange(SGU_CHUNK)
    mask = (i[None, :] // SGU_CAUSAL_CHUNK) <= (i[:, None] // SGU_CAUSAL_CHUNK)
    w_prompt = w_s * mask[None].astype(w_s.dtype)
    reps = SGU_CHUNK // t_new
    corner = w_prompt[:, :t_new, :t_new]
    eye = jnp.eye(reps, dtype=w_s.dtype)
    w_sample = jnp.einsum('ab,gij->gaibj', eye, corner).reshape(w_s.shape)
    groups = w_s.shape[0]

    def pairs(w):
        w = w.reshape(groups // 2, 2, SGU_CHUNK, SGU_CHUNK).transpose(0, 2, 1, 3)
        return w.reshape(groups // 2, SGU_CHUNK, 2 * SGU_CHUNK)

    def bias(b):
        return jnp.repeat(b.T, SGU_WIDTH // groups, axis=1)

    ws = jnp.stack([pairs(w_prompt), pairs(w_sample)]).astype(BF16)
    return ws, jnp.stack([bias(b_s), bias(jnp.tile(b_s[:, :t_new], (1, reps)))])


def kernel(x_prompt, x_sample, cache_k_sb, cache_v_sb, ffn1_pre_g, ffn1_post_g, ffn1_w_gate, ffn1_w_up, ffn1_w_down, mix_pre_g, mix_post_g, w_in, sgu_ln_g, sgu_ln_b, sgu_w_s, sgu_b_s, g_out_sb, g_out_sgu, w_out, ffn2_pre_g, ffn2_post_g, ffn2_w_gate, ffn2_w_up, ffn2_w_down):
    depth, batch, seq = w_in.shape[0], x_prompt.shape[0], x_prompt.shape[1]
    dec_batch, t_new, past_len = x_sample.shape[0], x_sample.shape[1], cache_k_sb.shape[2]
    assert depth == 1 and dec_batch * t_new == ROW_TILE and seq % ROW_TILE == 0
    assert SGU_CHUNK % t_new == 0 and KEY_BLOCK % t_new == 0
    assert past_len % (JOINT_PAST_BLOCKS * KEY_BLOCK) == 0
    n_prompt_tiles = batch * seq // ROW_TILE

    mat = lambda w: w[0].astype(BF16)
    vec = lambda g: g[0][None, :]

    x = _ffn(x_prompt.reshape(batch * seq, D_MODEL), x_sample.reshape(ROW_TILE, D_MODEL),
             vec(ffn1_pre_g), vec(ffn1_post_g), ffn1_w_gate[0], ffn1_w_up[0], ffn1_w_down[0],
             n_prompt_tiles)
    q, u, gn, kt3, vt3, kt_p, vt_p, kt_s, vt_s = _inproj(
        x, vec(mix_pre_g), w_in[0], vec(sgu_ln_g), vec(sgu_ln_b), n_prompt_tiles, seq // ROW_TILE)

    o_prompt = _sb_prompt(q, kt3, vt3, batch, seq)
    cache_t = lambda c: c[0].transpose(0, 2, 3, 1).reshape(dec_batch, SB_WIDTH, past_len)
    o_sample = _sb_sample(q, kt3, vt3, cache_t(cache_k_sb), cache_t(cache_v_sb), t_new, batch * seq)

    ws_pairs, bias = _sgu_operands(sgu_w_s[0], sgu_b_s[0], t_new)
    y_p, y_s = _mix_ffn(x, o_prompt, o_sample, u, gn, ws_pairs, bias, vec(g_out_sb), vec(g_out_sgu),
                        mat(w_out), vec(mix_post_g), vec(ffn2_pre_g), vec(ffn2_post_g),
                        ffn2_w_gate[0], ffn2_w_up[0], ffn2_w_down[0], n_prompt_tiles)

    def prompt_heads(t):
        return t.reshape(batch, N_HEADS, HEAD_DIM, seq).transpose(0, 3, 1, 2)[None]

    def sample_heads(t):
        return t.reshape(N_HEADS, HEAD_DIM, dec_batch, t_new).transpose(2, 3, 0, 1)[None]

    g_s = gn[batch * seq:].reshape(1, dec_batch, t_new, N_HEADS, HEAD_DIM)
    return (y_p.reshape(batch, seq, D_MODEL), y_s.reshape(dec_batch, t_new, D_MODEL),
            prompt_heads(kt_p), prompt_heads(vt_p), sample_heads(kt_s), sample_heads(vt_s), g_s)
```

```python
import functools

import jax
import jax.numpy as jnp
from jax import lax
from jax.experimental import pallas as pl
from jax.experimental.pallas import tpu as pltpu

D_MODEL = 1024
D_FF = 2816
SB_WIDTH = 512
SGU_WIDTH = 512
HEAD_DIM = 64
N_HEADS = SB_WIDTH // HEAD_DIM
SGU_CHUNK = 128
SGU_CAUSAL_CHUNK = 64
FFN_RES = 0.5
EPS = 1e-6

LANES = 128
N_PAIRS = SB_WIDTH // LANES
KEY_BLOCK = 128
JOINT_PAST_BLOCKS = 2
TOP_ROWS = 32
Q_BLOCKS_PER_STEP = 4
Q_BLOCKS_PER_PASS = 2
FF_CHUNKS = (1536, 1280)
ROW_TILE = 512
WEIGHT_CHUNK_ROWS = 128
FFN_ROW_GROUPS = (slice(0, 256), slice(256, 512))
VMEM_LIMIT = 56 * 1024 * 1024

LOG2_E = 1.4426950408889634
USED_STICK_CUTOFF = 105.0 * LOG2_E
HIDDEN_LOGIT = -1e30

F32 = jnp.float32
BF16 = jnp.bfloat16


def _rms(x, g):
    return x * lax.rsqrt(jnp.mean(x * x, axis=-1, keepdims=True) + EPS) * g


def _gelu(x):
    return 0.5 * x * (1.0 + lax.erf(x * (0.5 ** 0.5)))


def _const_spec(shape):
    return pl.BlockSpec(shape, lambda *_: (0,) * len(shape), pipeline_mode=pl.Buffered(1))


def _params(n_axes):
    return pltpu.CompilerParams(dimension_semantics=("arbitrary",) * n_axes,
                                vmem_limit_bytes=VMEM_LIMIT)


def _is_sample_step():
    return pl.program_id(0) == 0


def _merged_tile_spec(width, n_prompt_tiles):
    return pl.BlockSpec((ROW_TILE, width), lambda s: ((s + n_prompt_tiles) % (n_prompt_tiles + 1), 0))


def _prompt_tile_spec(width, rows=ROW_TILE):
    return pl.BlockSpec((rows, width), lambda s: (jnp.maximum(s - 1, 0), 0))


def _sample_tile_spec(width, rows=ROW_TILE):
    return pl.BlockSpec((rows, width), lambda s: (0, 0))


def _ffn_stages(x, pre_ref, post_ref, wg_ref, wu_ref, wd_ref, act_ref, rows):
    state = {}

    def hidden():
        state['h'] = _rms(x, pre_ref[...]).astype(BF16)

    def chunk(cols):
        gate = jnp.dot(state['h'], wg_ref[:, cols], preferred_element_type=F32)
        up = jnp.dot(state['h'], wu_ref[:, cols], preferred_element_type=F32)
        act_ref[rows, cols] = (gate * jax.nn.sigmoid(gate) * up).astype(BF16)

    def finish():
        f = jnp.dot(act_ref[rows, :], wd_ref[...], preferred_element_type=F32)
        return x + FFN_RES * _rms(f, post_ref[...])

    stages, start = [hidden], 0
    for width in FF_CHUNKS:
        stages.append(functools.partial(chunk, slice(start, start + width)))
        start += width
    return stages + [finish]


def _ffn_rows(x, pre_ref, post_ref, wg_ref, wu_ref, wd_ref, act_ref, rows):
    for stage in _ffn_stages(x, pre_ref, post_ref, wg_ref, wu_ref, wd_ref, act_ref, rows):
        y = stage()
    return y


_HBM_SPEC = pl.BlockSpec(memory_space=pl.ANY)
_FFN_WEIGHT_SPECS = [_const_spec((1, D_MODEL)), _const_spec((1, D_MODEL)), _HBM_SPEC, _HBM_SPEC, _HBM_SPEC]
_FFN_SCRATCH = [pltpu.VMEM((D_MODEL, D_FF), BF16), pltpu.VMEM((D_MODEL, D_FF), BF16),
                pltpu.VMEM((D_FF, D_MODEL), BF16),
                pltpu.VMEM((2, WEIGHT_CHUNK_ROWS, D_FF), F32),
                pltpu.VMEM((2, WEIGHT_CHUNK_ROWS, D_FF), F32),
                pltpu.VMEM((2, WEIGHT_CHUNK_ROWS, D_MODEL), F32),
                pltpu.SemaphoreType.DMA((6,)),
                pltpu.VMEM((ROW_TILE, D_FF), BF16)]


def _load_bf16(streams):
    def n_chunks(stream):
        return stream[0].shape[0] // stream[2].shape[1]

    def copy(stream, i):
        hbm_ref, _, stage_ref, sem_ref, first_sem = stream
        chunk = stage_ref.shape[1]
        return pltpu.make_async_copy(hbm_ref.at[pl.ds(i * chunk, chunk), :], stage_ref.at[i % 2],
                                     sem_ref.at[first_sem + i % 2])

    for stream in streams:
        copy(stream, 0).start()
    order = sorted(((i + 1) / n_chunks(stream), k, i)
                   for k, stream in enumerate(streams) for i in range(n_chunks(stream)))
    for _, k, i in order:
        stream = streams[k]
        if i + 1 < n_chunks(stream):
            copy(stream, i + 1).start()
        copy(stream, i).wait()
        chunk = stream[2].shape[1]
        stream[1][pl.ds(i * chunk, chunk), :] = stream[2][i % 2].astype(BF16)


def _load_ffn_weights(wg_hbm, wu_hbm, wd_hbm, wg_ref, wu_ref, wd_ref, stage_refs, sem_ref):
    @pl.when(pl.program_id(0) == 0)
    def _():
        _load_bf16([(hbm, dst, stage, sem_ref, 2 * k) for k, (hbm, dst, stage) in enumerate(
            zip((wg_hbm, wu_hbm, wd_hbm), (wg_ref, wu_ref, wd_ref), stage_refs))])


def _ffn_kernel(xp_ref, xs_ref, pre_ref, post_ref, wg_hbm, wu_hbm, wd_hbm, o_ref,
                wg_ref, wu_ref, wd_ref, stage_g_ref, stage_u_ref, stage_d_ref, sem_ref, act_ref):
    _load_ffn_weights(wg_hbm, wu_hbm, wd_hbm, wg_ref, wu_ref, wd_ref,
                      (stage_g_ref, stage_u_ref, stage_d_ref), sem_ref)
    for rows in FFN_ROW_GROUPS:
        x = jnp.where(_is_sample_step(), xs_ref[rows, :], xp_ref[rows, :])
        o_ref[rows, :] = _ffn_rows(x, pre_ref, post_ref, wg_ref, wu_ref, wd_ref, act_ref, rows)


def _ffn(x_prompt, x_sample, pre_g, post_g, wg, wu, wd, n_prompt_tiles):
    n_tiles = n_prompt_tiles + 1
    return pl.pallas_call(
        _ffn_kernel,
        grid=(n_tiles,),
        in_specs=[_prompt_tile_spec(D_MODEL), _sample_tile_spec(D_MODEL)] + _FFN_WEIGHT_SPECS,
        out_specs=_merged_tile_spec(D_MODEL, n_prompt_tiles),
        out_shape=jax.ShapeDtypeStruct((n_tiles * ROW_TILE, D_MODEL), F32),
        scratch_shapes=_FFN_SCRATCH,
        compiler_params=_params(1),
        name="ffn",
    )(x_prompt, x_sample, pre_g, post_g, wg, wu, wd)


def _inproj_kernel(x_ref, g_ref, win_hbm, lng_ref, lnb_ref,
                   q_ref, u_ref, gn_ref, kt3_ref, vt3_ref, ktp_ref, vtp_ref, kts_ref, vts_ref,
                   wqug_ref, wkv_ref, stage_ref, sem_ref):
    w = SB_WIDTH

    @pl.when(pl.program_id(0) == 0)
    def _():
        chunk = stage_ref.shape[1]
        n_chunks = win_hbm.shape[0] // chunk

        def copy(i):
            return pltpu.make_async_copy(win_hbm.at[pl.ds(i * chunk, chunk), :], stage_ref.at[i % 2],
                                         sem_ref.at[i % 2])

        copy(0).start()
        for i in range(n_chunks):
            if i + 1 < n_chunks:
                copy(i + 1).start()
            copy(i).wait()
            rows = pl.ds(i * chunk, chunk)
            part = stage_ref[i % 2]
            wqug_ref[rows, 0:w] = part[:, 0:w].astype(BF16)
            wqug_ref[rows, w:] = part[:, 3 * w:].astype(BF16)
            wkv_ref[rows, :] = part[:, w:3 * w].astype(BF16)

    h = _rms(x_ref[...], g_ref[...]).astype(BF16)
    z = jnp.dot(h, wqug_ref[:, w:], preferred_element_type=F32)
    u_ref[...] = _gelu(z[:, :SGU_WIDTH])
    ge = _gelu(z[:, SGU_WIDTH:])
    xc = ge - jnp.mean(ge, axis=-1, keepdims=True)
    y = xc * lax.rsqrt(jnp.mean(xc * xc, axis=-1, keepdims=True) + EPS)
    gn_ref[...] = y * lng_ref[...] + lnb_ref[...]
    kvt = lax.dot_general(wkv_ref[...], h, (((0,), (1,)), ((), ())), preferred_element_type=F32)
    for t_ref, t3_ref, rows in ((ktp_ref, kt3_ref, slice(0, w)), (vtp_ref, vt3_ref, slice(w, 2 * w))):
        t = kvt[rows]
        t_ref[...] = t
        for c in range(ROW_TILE // KEY_BLOCK):
            t3_ref[c] = t[:, c * KEY_BLOCK:(c + 1) * KEY_BLOCK].astype(BF16)
    q = jnp.dot(h, wqug_ref[:, :w], preferred_element_type=F32)
    q_ref[...] = (q * (HEAD_DIM ** -0.5 * LOG2_E)).astype(BF16)

    @pl.when(_is_sample_step())
    def _():
        kts_ref[...] = ktp_ref[...]
        vts_ref[...] = vtp_ref[...]


def _inproj(x, g, w_in, ln_g, ln_b, n_prompt_tiles, prompt_tiles_per_row):
    n = x.shape[0]
    half = _merged_tile_spec(SB_WIDTH, n_prompt_tiles)
    blocks_per_tile = ROW_TILE // KEY_BLOCK
    key_blocks = pl.BlockSpec((blocks_per_tile, SB_WIDTH, KEY_BLOCK),
                              lambda s: ((s + n_prompt_tiles) % (n_prompt_tiles + 1), 0, 0))

    def prompt_t_index(s):
        tile = jnp.maximum(s - 1, 0)
        return tile // prompt_tiles_per_row, tile % prompt_tiles_per_row

    prompt_t = pl.BlockSpec((SB_WIDTH, ROW_TILE), prompt_t_index)
    prompt_t_sds = jax.ShapeDtypeStruct(
        (n_prompt_tiles // prompt_tiles_per_row * SB_WIDTH, prompt_tiles_per_row * ROW_TILE), F32)
    sample_t_sds = jax.ShapeDtypeStruct((SB_WIDTH, ROW_TILE), F32)
    f32_half = jax.ShapeDtypeStruct((n, SB_WIDTH), F32)
    key_blocks_sds = jax.ShapeDtypeStruct((n // KEY_BLOCK, SB_WIDTH, KEY_BLOCK), BF16)
    return pl.pallas_call(
        _inproj_kernel,
        grid=(n // ROW_TILE,),
        in_specs=[_merged_tile_spec(D_MODEL, n_prompt_tiles), _const_spec((1, D_MODEL)),
                  _HBM_SPEC, _const_spec((1, SGU_WIDTH)), _const_spec((1, SGU_WIDTH))],
        out_specs=[half] * 3 + [key_blocks] * 2 + [prompt_t] * 2 + [_sample_tile_spec(ROW_TILE, SB_WIDTH)] * 2,
        out_shape=[jax.ShapeDtypeStruct((n, SB_WIDTH), BF16), f32_half, f32_half,
                   key_blocks_sds, key_blocks_sds, prompt_t_sds, prompt_t_sds,
                   sample_t_sds, sample_t_sds],
        scratch_shapes=[pltpu.VMEM((D_MODEL, SB_WIDTH + 2 * SGU_WIDTH), BF16),
                        pltpu.VMEM((D_MODEL, 2 * SB_WIDTH), BF16),
                        pltpu.VMEM((2, WEIGHT_CHUNK_ROWS, w_in.shape[1]), F32),
                        pltpu.SemaphoreType.DMA((2,))],
        compiler_params=_params(1),
        name="inproj",
    )(x, g, w_in, ln_g, ln_b)


def _pair_rows(x, lane_lo):
    zero = jnp.zeros_like(x)
    return jnp.concatenate([jnp.where(lane_lo, x, zero), jnp.where(lane_lo, zero, x)], axis=0)


def _sb_blocks(qqs, kts, vts, carries, suffix_ones, lane_lo, masks, tops=None):
    n_blocks = len(masks)
    tops = tops or [None] * n_blocks
    t = qqs[0].shape[0] // 2
    nt_dims = (((1,), (1,)), ((), ()))

    def head_rows(x, top):
        return x if top is None else jnp.concatenate([x[:top], x[t:t + top]], axis=0)

    def weighted_values(a, vt):
        both = lax.dot_general(a, vt, nt_dims, preferred_element_type=F32)
        rows = a.shape[0] // 2
        return jnp.where(lane_lo, both[:rows], both[rows:])

    zs = []
    for qq, kts_p in zip(qqs, kts):
        zs_p, b = [], 0
        while b < n_blocks:
            if b + 1 < n_blocks and tops[b] is None and tops[b + 1] is None:
                z2 = jnp.dot(qq, jnp.concatenate([kts_p[b], kts_p[b + 1]], axis=1),
                             preferred_element_type=F32)
                zs_p += [z2[:, :KEY_BLOCK], z2[:, KEY_BLOCK:]]
                b += 2
            else:
                zs_p.append(jnp.dot(head_rows(qq, tops[b]), kts_p[b], preferred_element_type=F32))
                b += 1
        zs.append(zs_p)
    useds, splits = [], []
    for zs_p in zs:
        useds.append([])
        splits.append([])
        for i, mask in enumerate(masks):
            z = zs_p[i]
            if mask is not None:
                z = zs_p[i] = jnp.where(mask, z, HIDDEN_LOGIT)
            used = jnp.maximum(z, 0.0) + jnp.log(1.0 + jnp.exp2(-jnp.abs(z))) * LOG2_E
            hi = used.astype(BF16)
            lo = (used - hi.astype(F32)).astype(BF16)
            useds[-1].append(used)
            splits[-1].append(jnp.concatenate([hi, lo], axis=1))
    sums = [[jnp.dot(s, suffix_ones, preferred_element_type=F32) for s in splits_p]
            for splits_p in splits]
    outs, new_carries = [], []
    for zs_p, useds_p, sums_p, vts_p, carry in zip(zs, useds, sums, vts, carries):
        weights, values_t, top_outs = [], [], []
        for z, used, s, vt, top in zip(zs_p, useds_p, sums_p, vts_p, tops):
            later = s[:, :KEY_BLOCK] if carry is None else s[:, :KEY_BLOCK] + head_rows(carry, top)
            a = jnp.exp2(z - used - later).astype(BF16)
            row_sum = s[:, KEY_BLOCK:]
            if top is None:
                weights.append(a)
                values_t.append(vt)
                carry = row_sum if carry is None else carry + row_sum
            else:
                top_outs.append(weighted_values(a, vt))
                carry = jnp.concatenate([carry[:top] + row_sum[:top], carry[top:t],
                                         carry[t:t + top] + row_sum[top:], carry[t + top:]], axis=0)
        out = weighted_values(jnp.concatenate(weights, axis=1), jnp.concatenate(values_t, axis=1))
        for o_top in top_outs:
            top = o_top.shape[0]
            out = jnp.concatenate([out[:top] + o_top, out[top:]], axis=0)
        outs.append(out)
        new_carries.append(carry)
    return outs, new_carries


def _suffix_ones():
    r = lax.broadcasted_iota(jnp.int32, (2 * KEY_BLOCK, 2 * KEY_BLOCK), 0) % KEY_BLOCK
    c = lax.broadcasted_iota(jnp.int32, (2 * KEY_BLOCK, 2 * KEY_BLOCK), 1)
    return jnp.where((c >= KEY_BLOCK) | (r > c), 1.0, 0.0).astype(BF16)


_PAIR_SLICES = [slice(p * LANES, (p + 1) * LANES) for p in range(N_PAIRS)]


class _QueryBlock:
    def __init__(self, q_ref, o_ref, carry_ref, n_past, diag_tiles, joint_tiles, walk_tiles):
        self.q_ref, self.o_ref, self.carry_ref, self.n_past = q_ref, o_ref, carry_ref, n_past
        self.least_ref = carry_ref.at[N_PAIRS]
        self.diag_tiles, self.joint_tiles, self.walk_tiles = diag_tiles, joint_tiles, walk_tiles


class _StickBreaking:
    def __init__(self, tq, diag_key_offset):
        self.tq = tq
        self.lane_lo = lax.broadcasted_iota(jnp.int32, (1, LANES), 1) < HEAD_DIM
        self.suffix_ones = _suffix_ones()
        self.row = lax.broadcasted_iota(jnp.int32, (2 * tq, KEY_BLOCK), 0) % tq
        col = lax.broadcasted_iota(jnp.int32, (2 * tq, KEY_BLOCK), 1) - diag_key_offset
        self.causal = (col >= 0) & (col < self.row)

    def _qq_tiles(self, blk):
        return [_pair_rows(blk.q_ref[:, cols], self.lane_lo) for cols in _PAIR_SLICES]

    def _store(self, blk, outs, carries, first):
        for p, cols in enumerate(_PAIR_SLICES):
            if first:
                blk.o_ref[:, cols] = outs[p]
            else:
                blk.o_ref[:, cols] += outs[p]
            blk.carry_ref[p] = carries[p]

    def first_step(self, blks, n_joint):
        qqs, kts, vts = [], [], []
        for blk in blks:
            kd, vd = blk.diag_tiles()
            kts_b, vts_b = [[t] for t in kd], [[t] for t in vd]
            for b in range(n_joint):
                kj, vj = blk.joint_tiles(blk.n_past - 1 - b)
                for p in range(N_PAIRS):
                    kts_b[p].append(kj[p])
                    vts_b[p].append(vj[p])
            qqs += self._qq_tiles(blk)
            kts += kts_b
            vts += vts_b
        tops = [None] * n_joint + ([TOP_ROWS] if n_joint > 1 else [None])
        outs, carries = _sb_blocks(qqs, kts, vts, [None] * len(qqs),
                                   self.suffix_ones, self.lane_lo, [self.causal] + [None] * n_joint, tops)
        for i, blk in enumerate(blks):
            pairs = slice(i * N_PAIRS, (i + 1) * N_PAIRS)
            self._store(blk, outs[pairs], carries[pairs], True)
        least = functools.reduce(jnp.minimum, carries)
        for blk in blks:
            blk.least_ref[...] = least

    def _single_block(self, blk, tiles, mask):
        kts, vts = tiles
        outs, carries = _sb_blocks(self._qq_tiles(blk), [[t] for t in kts], [[t] for t in vts],
                                   [blk.carry_ref[p] for p in range(N_PAIRS)],
                                   self.suffix_ones, self.lane_lo, [mask])
        self._store(blk, outs, carries, False)

    def _least_used(self, blk, rows=None):
        tq = self.tq

        def pick(x):
            return x if rows is None else jnp.concatenate([x[rows], x[tq + rows.start:tq + rows.stop]], axis=0)
        return jnp.min(functools.reduce(jnp.minimum, [pick(blk.carry_ref[p]) for p in range(N_PAIRS)]))

    def finish_partial_block(self, blk):
        @pl.when(self._least_used(blk, slice(TOP_ROWS, self.tq)) < USED_STICK_CUTOFF)
        def _():
            self._single_block(blk, blk.joint_tiles(blk.n_past - JOINT_PAST_BLOCKS), self.row >= TOP_ROWS)

    def walk(self, blk, next_block):
        def body(state):
            j, _ = state
            self._single_block(blk, blk.walk_tiles(j), None)
            return j - 1, self._least_used(blk)

        lax.while_loop(lambda s: (s[0] >= 0) & (s[1] < USED_STICK_CUTOFF), body,
                       (next_block, self._least_used(blk)))

    def finish(self, blks, saw_partial_block, next_blocks):
        least_used = jnp.min(functools.reduce(jnp.minimum, [blk.least_ref[...] for blk in blks]))

        @pl.when(least_used < USED_STICK_CUTOFF)
        def _():
            for blk, partial in zip(blks, saw_partial_block):
                if partial is True:
                    self.finish_partial_block(blk)
                elif partial is not False:
                    pl.when(partial)(functools.partial(self.finish_partial_block, blk))
            for blk, next_block in zip(blks, next_blocks):
                self.walk(blk, next_block)


def _sb_prompt_kernel(q_ref, kt_ref, vt_ref, o_ref, carry_ref):
    def tiles(j):
        return ([kt_ref[j, rows, :] for rows in _PAIR_SLICES],
                [vt_ref[j, rows, :] for rows in _PAIR_SLICES])

    step = pl.program_id(1)
    sb = _StickBreaking(KEY_BLOCK, 0)
    blks = []
    for r in range(Q_BLOCKS_PER_STEP):
        i = step * Q_BLOCKS_PER_STEP + r
        rows = pl.ds(r * KEY_BLOCK, KEY_BLOCK)
        blks.append(_QueryBlock(q_ref.at[rows, :], o_ref.at[rows, :], carry_ref.at[r], i,
                                functools.partial(tiles, i), tiles, tiles))

    assert Q_BLOCKS_PER_STEP >= JOINT_PAST_BLOCKS > 1

    @pl.when(step > 0)
    def _():
        for first in range(0, Q_BLOCKS_PER_STEP, Q_BLOCKS_PER_PASS):
            sb.first_step(blks[first:first + Q_BLOCKS_PER_PASS], JOINT_PAST_BLOCKS)

    @pl.when(step == 0)
    def _():
        for r, blk in enumerate(blks):
            sb.first_step([blk], min(r, JOINT_PAST_BLOCKS))

    sb.finish(blks,
              [True if r >= JOINT_PAST_BLOCKS else step > 0 for r in range(Q_BLOCKS_PER_STEP)],
              [jnp.where(step > 0, blk.n_past - JOINT_PAST_BLOCKS, r - min(r, JOINT_PAST_BLOCKS)) - 1
               for r, blk in enumerate(blks)])


def _sb_prompt(q, kt3, vt3, batch, seq):
    nq = seq // (KEY_BLOCK * Q_BLOCKS_PER_STEP)
    blk = pl.BlockSpec((KEY_BLOCK * Q_BLOCKS_PER_STEP, SB_WIDTH), lambda b, i: (b * nq + i, 0))
    keys = pl.BlockSpec((seq // KEY_BLOCK, SB_WIDTH, KEY_BLOCK), lambda b, i: (b, 0, 0))
    return pl.pallas_call(
        _sb_prompt_kernel,
        grid=(batch, nq),
        in_specs=[blk, keys, keys],
        out_specs=blk,
        out_shape=jax.ShapeDtypeStruct((batch * seq, SB_WIDTH), F32),
        scratch_shapes=[pltpu.VMEM((Q_BLOCKS_PER_STEP, N_PAIRS + 1, 2 * KEY_BLOCK, LANES), F32)],
        compiler_params=_params(2),
        name="sb_prompt",
    )(q, kt3, vt3)


def _sb_sample_kernel(q_ref, kd_ref, vd_ref, kr_ref, vr_ref, kc_hbm, vc_hbm, o_ref,
                      carry_ref, kbuf_ref, vbuf_ref, sem_ref, *, past_len):
    tq = q_ref.shape[0]
    n_past = past_len // KEY_BLOCK
    n_recent = kr_ref.shape[2] // KEY_BLOCK
    b = pl.program_id(0)

    def diag_tiles():
        return ([kd_ref[0, rows, :] for rows in _PAIR_SLICES],
                [vd_ref[0, rows, :] for rows in _PAIR_SLICES])

    def recent_tiles(j):
        first = (j - (n_past - n_recent)) * KEY_BLOCK
        keys = slice(first, first + KEY_BLOCK)
        return ([kr_ref[0, rows, keys].astype(BF16) for rows in _PAIR_SLICES],
                [vr_ref[0, rows, keys].astype(BF16) for rows in _PAIR_SLICES])

    def cache_tiles(j):
        keys = pl.ds(pl.multiple_of(j * KEY_BLOCK, KEY_BLOCK), KEY_BLOCK)
        copies = [pltpu.make_async_copy(hbm.at[b, :, keys], buf, sem_ref.at[s])
                  for s, (hbm, buf) in enumerate(((kc_hbm, kbuf_ref), (vc_hbm, vbuf_ref)))]
        for c in copies:
            c.start()
        for c in copies:
            c.wait()
        return ([kbuf_ref[rows, :].astype(BF16) for rows in _PAIR_SLICES],
                [vbuf_ref[rows, :].astype(BF16) for rows in _PAIR_SLICES])

    assert n_recent == JOINT_PAST_BLOCKS > 1
    sb = _StickBreaking(tq, (b % (KEY_BLOCK // tq)) * tq)
    blk = _QueryBlock(q_ref, o_ref, carry_ref, n_past, diag_tiles, recent_tiles, cache_tiles)
    sb.first_step([blk], JOINT_PAST_BLOCKS)
    sb.finish([blk], [True], [n_past - JOINT_PAST_BLOCKS - 1])


def _sb_sample(q, kt3, vt3, cache_kt, cache_vt, t_new, first_row):
    batch, _, past_len = cache_kt.shape
    n_recent = JOINT_PAST_BLOCKS * KEY_BLOCK
    seqs_per_block = KEY_BLOCK // t_new
    new = pl.BlockSpec((1, SB_WIDTH, KEY_BLOCK),
                       lambda b: (first_row // KEY_BLOCK + b // seqs_per_block, 0, 0))
    recent = pl.BlockSpec((1, SB_WIDTH, n_recent), lambda b: (b, 0, past_len // n_recent - 1))
    hbm = pl.BlockSpec(memory_space=pl.ANY)
    return pl.pallas_call(
        functools.partial(_sb_sample_kernel, past_len=past_len),
        grid=(batch,),
        in_specs=[pl.BlockSpec((t_new, SB_WIDTH), lambda b: (first_row // t_new + b, 0)),
                  new, new, recent, recent, hbm, hbm],
        out_specs=pl.BlockSpec((t_new, SB_WIDTH), lambda b: (b, 0)),
        out_shape=jax.ShapeDtypeStruct((batch * t_new, SB_WIDTH), F32),
        scratch_shapes=[pltpu.VMEM((N_PAIRS + 1, 2 * t_new, LANES), F32),
                        pltpu.VMEM((SB_WIDTH, KEY_BLOCK), F32),
                        pltpu.VMEM((SB_WIDTH, KEY_BLOCK), F32),
                        pltpu.SemaphoreType.DMA((2,))],
        compiler_params=_params(1),
        name="sb_sample",
    )(q, kt3, vt3, cache_kt, cache_vt, cache_kt, cache_vt)


def _mix_rows(x_ref, op_ref, os_ref, u_ref, gn_ref, ws_ref, bias_ref, gsb_ref, gsgu_ref, wout_ref,
              post_ref, sgu_ref, rows):
    lane_lo = lax.broadcasted_iota(jnp.int32, (1, LANES), 1) < HEAD_DIM
    chunks = [slice(first, first + SGU_CHUNK) for first in range(rows.start, rows.stop, SGU_CHUNK)]
    for p in range(N_PAIRS):
        cols = slice(p * LANES, (p + 1) * LANES)
        gg = jnp.concatenate([_pair_rows(gn_ref[chunk, cols].astype(BF16), lane_lo)
                              for chunk in chunks], axis=1)
        mixed = jnp.dot(ws_ref[0, p], gg, preferred_element_type=F32)
        for c, chunk in enumerate(chunks):
            sgu_ref[chunk, cols] = u_ref[chunk, cols] * (mixed[:, c * LANES:(c + 1) * LANES]
                                                         + bias_ref[0, :, cols])
    o_sb = jnp.where(_is_sample_step(), os_ref[rows, :], op_ref[rows, :])
    merged = jnp.concatenate([_rms(o_sb, gsb_ref[...]).astype(BF16),
                              _rms(sgu_ref[rows, :], gsgu_ref[...]).astype(BF16)], axis=1)
    y = jnp.dot(merged, wout_ref[...], preferred_element_type=F32)
    return x_ref[rows, :] + _rms(y, post_ref[...])


def _mix_ffn_kernel(x_ref, op_ref, os_ref, u_ref, gn_ref, ws_ref, bias_ref, gsb_ref, gsgu_ref,
                    wout_ref, mix_post_ref, pre_ref, post_ref, wg_hbm, wu_hbm, wd_hbm,
                    yp_ref, ys_ref, sgu_ref,
                    wg_ref, wu_ref, wd_ref, stage_g_ref, stage_u_ref, stage_d_ref, sem_ref, act_ref):
    _load_ffn_weights(wg_hbm, wu_hbm, wd_hbm, wg_ref, wu_ref, wd_ref,
                      (stage_g_ref, stage_u_ref, stage_d_ref), sem_ref)

    def mix(rows):
        return _mix_rows(x_ref, op_ref, os_ref, u_ref, gn_ref, ws_ref, bias_ref, gsb_ref, gsgu_ref,
                         wout_ref, mix_post_ref, sgu_ref, rows)

    groups = FFN_ROW_GROUPS
    x = mix(groups[0])
    for g, rows in enumerate(groups):
        stages = _ffn_stages(x, pre_ref, post_ref, wg_ref, wu_ref, wd_ref, act_ref, rows)
        stages[0]()
        stages[1]()
        if g + 1 < len(groups):
            x = mix(groups[g + 1])
        for stage in stages[2:]:
            y = stage()
        yp_ref[rows, :] = y

    @pl.when(_is_sample_step())
    def _():
        ys_ref[...] = yp_ref[...]


def _mix_ffn(x, o_prompt, o_sample, u, gn, ws_pairs, bias, g_sb, g_sgu, w_out, mix_post_g,
             pre_g, post_g, wg, wu, wd, n_prompt_tiles):
    row = _merged_tile_spec(D_MODEL, n_prompt_tiles)
    half = _merged_tile_spec(SB_WIDTH, n_prompt_tiles)
    which = lambda s: jnp.where(s == 0, 1, 0)
    sds = lambda rows: jax.ShapeDtypeStruct((rows, D_MODEL), F32)
    return pl.pallas_call(
        _mix_ffn_kernel,
        grid=(n_prompt_tiles + 1,),
        in_specs=[row, _prompt_tile_spec(SB_WIDTH), _sample_tile_spec(SB_WIDTH), half, half,
                  pl.BlockSpec((1,) + ws_pairs.shape[1:], lambda s: (which(s), 0, 0, 0)),
                  pl.BlockSpec((1,) + bias.shape[1:], lambda s: (which(s), 0, 0)),
                  _const_spec((1, SB_WIDTH)), _const_spec((1, SGU_WIDTH)),
                  _const_spec((D_MODEL, D_MODEL)), _const_spec((1, D_MODEL))] + _FFN_WEIGHT_SPECS,
        out_specs=[_prompt_tile_spec(D_MODEL), _sample_tile_spec(D_MODEL)],
        out_shape=[sds(n_prompt_tiles * ROW_TILE), sds(ROW_TILE)],
        scratch_shapes=[pltpu.VMEM((ROW_TILE, SGU_WIDTH), F32)] + _FFN_SCRATCH,
        compiler_params=_params(1),
        name="mix_ffn",
    )(x, o_prompt, o_sample, u, gn, ws_pairs, bias, g_sb, g_sgu, w_out, mix_post_g,
      pre_g, post_g, wg, wu, wd)


def _sgu_operands(w_s, b_s, t_new):
    i = jnp.arange(SGU_CHUNK)
    mask = (i[None, :] // SGU_CAUSAL_CHUNK) <= (i[:, None] // SGU_CAUSAL_CHUNK)
    w_prompt = w_s * mask[None].astype(w_s.dtype)
    reps = SGU_CHUNK // t_new
    corner = w_prompt[:, :t_new, :t_new]
    eye = jnp.eye(reps, dtype=w_s.dtype)
    w_sample = jnp.einsum('ab,gij->gaibj', eye, corner).reshape(w_s.shape)
    groups = w_s.shape[0]

    def pairs(w):
        w = w.reshape(groups // 2, 2, SGU_CHUNK, SGU_CHUNK).transpose(0, 2, 1, 3)
        return w.reshape(groups // 2, SGU_CHUNK, 2 * SGU_CHUNK)

    def bias(b):
        return jnp.repeat(b.T, SGU_WIDTH // groups, axis=1)

    ws = jnp.stack([pairs(w_prompt), pairs(w_sample)]).astype(BF16)
    return ws, jnp.stack([bias(b_s), bias(jnp.tile(b_s[:, :t_new], (1, reps)))])


def kernel(x_prompt, x_sample, cache_k_sb, cache_v_sb, ffn1_pre_g, ffn1_post_g, ffn1_w_gate, ffn1_w_up, ffn1_w_down, mix_pre_g, mix_post_g, w_in, sgu_ln_g, sgu_ln_b, sgu_w_s, sgu_b_s, g_out_sb, g_out_sgu, w_out, ffn2_pre_g, ffn2_post_g, ffn2_w_gate, ffn2_w_up, ffn2_w_down):
    depth, batch, seq = w_in.shape[0], x_prompt.shape[0], x_prompt.shape[1]
    dec_batch, t_new, past_len = x_sample.shape[0], x_sample.shape[1], cache_k_sb.shape[2]
    assert depth == 1 and dec_batch * t_new == ROW_TILE and seq % ROW_TILE == 0
    assert SGU_CHUNK % t_new == 0 and KEY_BLOCK % t_new == 0
    assert past_len % (JOINT_PAST_BLOCKS * KEY_BLOCK) == 0
    n_prompt_tiles = batch * seq // ROW_TILE

    mat = lambda w: w[0].astype(BF16)
    vec = lambda g: g[0][None, :]

    x = _ffn(x_prompt.reshape(batch * seq, D_MODEL), x_sample.reshape(ROW_TILE, D_MODEL),
             vec(ffn1_pre_g), vec(ffn1_post_g), ffn1_w_gate[0], ffn1_w_up[0], ffn1_w_down[0],
             n_prompt_tiles)
    q, u, gn, kt3, vt3, kt_p, vt_p, kt_s, vt_s = _inproj(
        x, vec(mix_pre_g), w_in[0], vec(sgu_ln_g), vec(sgu_ln_b), n_prompt_tiles, seq // ROW_TILE)

    o_prompt = _sb_prompt(q, kt3, vt3, batch, seq)
    cache_t = lambda c: c[0].transpose(0, 2, 3, 1).reshape(dec_batch, SB_WIDTH, past_len)
    o_sample = _sb_sample(q, kt3, vt3, cache_t(cache_k_sb), cache_t(cache_v_sb), t_new, batch * seq)

    ws_pairs, bias = _sgu_operands(sgu_w_s[0], sgu_b_s[0], t_new)
    y_p, y_s = _mix_ffn(x, o_prompt, o_sample, u, gn, ws_pairs, bias, vec(g_out_sb), vec(g_out_sgu),
                        mat(w_out), vec(mix_post_g), vec(ffn2_pre_g), vec(ffn2_post_g),
                        ffn2_w_gate[0], ffn2_w_up[0], ffn2_w_down[0], n_prompt_tiles)

    def prompt_heads(t):
        return t.reshape(batch, N_HEADS, HEAD_DIM, seq).transpose(0, 3, 1, 2)[None]

    def sample_heads(t):
        return t.reshape(N_HEADS, HEAD_DIM, dec_batch, t_new).transpose(2, 3, 0, 1)[None]

    g_s = gn[batch * seq:].reshape(1, dec_batch, t_new, N_HEADS, HEAD_DIM)
    return (y_p.reshape(batch, seq, D_MODEL), y_s.reshape(dec_batch, t_new, D_MODEL),
            prompt_heads(kt_p), prompt_heads(vt_p), sample_heads(kt_s), sample_heads(vt_s), g_s)
```

```python
import functools

import jax
import jax.numpy as jnp
from jax import lax
from jax.experimental import pallas as pl
from jax.experimental.pallas import tpu as pltpu

D_MODEL = 1024
D_FF = 2816
SB_WIDTH = 512
SGU_WIDTH = 512
HEAD_DIM = 64
N_HEADS = SB_WIDTH // HEAD_DIM
SGU_CHUNK = 128
SGU_CAUSAL_CHUNK = 64
FFN_RES = 0.5
EPS = 1e-6

LANES = 128
N_PAIRS = SB_WIDTH // LANES
KEY_BLOCK = 128
JOINT_PAST_BLOCKS = 2
TOP_ROWS = 32
Q_BLOCKS_PER_STEP = 4
Q_BLOCKS_PER_PASS = 4
FF_CHUNKS = (1536, 1280)
ROW_TILE = 512
WEIGHT_CHUNK_ROWS = 128
FFN_ROW_GROUPS = (slice(0, 256), slice(256, 512))
VMEM_LIMIT = 56 * 1024 * 1024

LOG2_E = 1.4426950408889634
USED_STICK_CUTOFF = 105.0 * LOG2_E
HIDDEN_LOGIT = -1e30

F32 = jnp.float32
BF16 = jnp.bfloat16


def _rms(x, g):
    return x * lax.rsqrt(jnp.mean(x * x, axis=-1, keepdims=True) + EPS) * g


def _gelu(x):
    return 0.5 * x * (1.0 + lax.erf(x * (0.5 ** 0.5)))


def _const_spec(shape):
    return pl.BlockSpec(shape, lambda *_: (0,) * len(shape), pipeline_mode=pl.Buffered(1))


def _params(n_axes):
    return pltpu.CompilerParams(dimension_semantics=("arbitrary",) * n_axes,
                                vmem_limit_bytes=VMEM_LIMIT)


def _is_sample_step():
    return pl.program_id(0) == 0


def _merged_tile_spec(width, n_prompt_tiles):
    return pl.BlockSpec((ROW_TILE, width), lambda s: ((s + n_prompt_tiles) % (n_prompt_tiles + 1), 0))


def _prompt_tile_spec(width, rows=ROW_TILE):
    return pl.BlockSpec((rows, width), lambda s: (jnp.maximum(s - 1, 0), 0))


def _sample_tile_spec(width, rows=ROW_TILE):
    return pl.BlockSpec((rows, width), lambda s: (0, 0))


def _ffn_stages(x, pre_ref, post_ref, wg_ref, wu_ref, wd_ref, act_ref, rows):
    state = {}

    def hidden():
        state['h'] = _rms(x, pre_ref[...]).astype(BF16)

    def chunk(cols):
        gate = jnp.dot(state['h'], wg_ref[:, cols], preferred_element_type=F32)
        up = jnp.dot(state['h'], wu_ref[:, cols], preferred_element_type=F32)
        act_ref[rows, cols] = (gate * jax.nn.sigmoid(gate) * up).astype(BF16)

    def finish():
        f = jnp.dot(act_ref[rows, :], wd_ref[...], preferred_element_type=F32)
        return x + FFN_RES * _rms(f, post_ref[...])

    stages, start = [hidden], 0
    for width in FF_CHUNKS:
        stages.append(functools.partial(chunk, slice(start, start + width)))
        start += width
    return stages + [finish]


def _ffn_rows(x, pre_ref, post_ref, wg_ref, wu_ref, wd_ref, act_ref, rows):
    for stage in _ffn_stages(x, pre_ref, post_ref, wg_ref, wu_ref, wd_ref, act_ref, rows):
        y = stage()
    return y


_HBM_SPEC = pl.BlockSpec(memory_space=pl.ANY)
_FFN_WEIGHT_SPECS = [_const_spec((1, D_MODEL)), _const_spec((1, D_MODEL)), _HBM_SPEC, _HBM_SPEC, _HBM_SPEC]
_FFN_SCRATCH = [pltpu.VMEM((D_MODEL, D_FF), BF16), pltpu.VMEM((D_MODEL, D_FF), BF16),
                pltpu.VMEM((D_FF, D_MODEL), BF16),
                pltpu.VMEM((2, WEIGHT_CHUNK_ROWS, D_FF), F32),
                pltpu.VMEM((2, WEIGHT_CHUNK_ROWS, D_FF), F32),
                pltpu.VMEM((2, WEIGHT_CHUNK_ROWS, D_MODEL), F32),
                pltpu.SemaphoreType.DMA((6,)),
                pltpu.VMEM((ROW_TILE, D_FF), BF16)]


def _load_bf16(streams):
    def n_chunks(stream):
        return stream[0].shape[0] // stream[2].shape[1]

    def copy(stream, i):
        hbm_ref, _, stage_ref, sem_ref, first_sem = stream
        chunk = stage_ref.shape[1]
        return pltpu.make_async_copy(hbm_ref.at[pl.ds(i * chunk, chunk), :], stage_ref.at[i % 2],
                                     sem_ref.at[first_sem + i % 2])

    for stream in streams:
        copy(stream, 0).start()
    order = sorted(((i + 1) / n_chunks(stream), k, i)
                   for k, stream in enumerate(streams) for i in range(n_chunks(stream)))
    for _, k, i in order:
        stream = streams[k]
        if i + 1 < n_chunks(stream):
            copy(stream, i + 1).start()
        copy(stream, i).wait()
        chunk = stream[2].shape[1]
        stream[1][pl.ds(i * chunk, chunk), :] = stream[2][i % 2].astype(BF16)


def _load_ffn_weights(wg_hbm, wu_hbm, wd_hbm, wg_ref, wu_ref, wd_ref, stage_refs, sem_ref):
    @pl.when(pl.program_id(0) == 0)
    def _():
        _load_bf16([(hbm, dst, stage, sem_ref, 2 * k) for k, (hbm, dst, stage) in enumerate(
            zip((wg_hbm, wu_hbm, wd_hbm), (wg_ref, wu_ref, wd_ref), stage_refs))])


def _ffn_kernel(xp_ref, xs_ref, pre_ref, post_ref, wg_hbm, wu_hbm, wd_hbm, o_ref,
                wg_ref, wu_ref, wd_ref, stage_g_ref, stage_u_ref, stage_d_ref, sem_ref, act_ref):
    _load_ffn_weights(wg_hbm, wu_hbm, wd_hbm, wg_ref, wu_ref, wd_ref,
                      (stage_g_ref, stage_u_ref, stage_d_ref), sem_ref)
    for rows in FFN_ROW_GROUPS:
        x = jnp.where(_is_sample_step(), xs_ref[rows, :], xp_ref[rows, :])
        o_ref[rows, :] = _ffn_rows(x, pre_ref, post_ref, wg_ref, wu_ref, wd_ref, act_ref, rows)


def _ffn(x_prompt, x_sample, pre_g, post_g, wg, wu, wd, n_prompt_tiles):
    n_tiles = n_prompt_tiles + 1
    return pl.pallas_call(
        _ffn_kernel,
        grid=(n_tiles,),
        in_specs=[_prompt_tile_spec(D_MODEL), _sample_tile_spec(D_MODEL)] + _FFN_WEIGHT_SPECS,
        out_specs=_merged_tile_spec(D_MODEL, n_prompt_tiles),
        out_shape=jax.ShapeDtypeStruct((n_tiles * ROW_TILE, D_MODEL), F32),
        scratch_shapes=_FFN_SCRATCH,
        compiler_params=_params(1),
        name="ffn",
    )(x_prompt, x_sample, pre_g, post_g, wg, wu, wd)


def _inproj_kernel(x_ref, g_ref, win_hbm, lng_ref, lnb_ref,
                   q_ref, u_ref, gn_ref, kt3_ref, vt3_ref, ktp_ref, vtp_ref, kts_ref, vts_ref,
                   wqug_ref, wkv_ref, stage_ref, sem_ref):
    w = SB_WIDTH

    @pl.when(pl.program_id(0) == 0)
    def _():
        chunk = stage_ref.shape[1]
        n_chunks = win_hbm.shape[0] // chunk

        def copy(i):
            return pltpu.make_async_copy(win_hbm.at[pl.ds(i * chunk, chunk), :], stage_ref.at[i % 2],
                                         sem_ref.at[i % 2])

        copy(0).start()
        for i in range(n_chunks):
            if i + 1 < n_chunks:
                copy(i + 1).start()
            copy(i).wait()
            rows = pl.ds(i * chunk, chunk)
            part = stage_ref[i % 2]
            wqug_ref[rows, 0:w] = part[:, 0:w].astype(BF16)
            wqug_ref[rows, w:] = part[:, 3 * w:].astype(BF16)
            wkv_ref[rows, :] = part[:, w:3 * w].astype(BF16)

    h = _rms(x_ref[...], g_ref[...]).astype(BF16)
    z = jnp.dot(h, wqug_ref[:, w:], preferred_element_type=F32)
    u_ref[...] = _gelu(z[:, :SGU_WIDTH])
    ge = _gelu(z[:, SGU_WIDTH:])
    xc = ge - jnp.mean(ge, axis=-1, keepdims=True)
    y = xc * lax.rsqrt(jnp.mean(xc * xc, axis=-1, keepdims=True) + EPS)
    gn_ref[...] = y * lng_ref[...] + lnb_ref[...]
    kvt = lax.dot_general(wkv_ref[...], h, (((0,), (1,)), ((), ())), preferred_element_type=F32)
    for t_ref, t3_ref, rows in ((ktp_ref, kt3_ref, slice(0, w)), (vtp_ref, vt3_ref, slice(w, 2 * w))):
        t = kvt[rows]
        t_ref[...] = t
        for c in range(ROW_TILE // KEY_BLOCK):
            t3_ref[c] = t[:, c * KEY_BLOCK:(c + 1) * KEY_BLOCK].astype(BF16)
    q = jnp.dot(h, wqug_ref[:, :w], preferred_element_type=F32)
    q_ref[...] = (q * (HEAD_DIM ** -0.5 * LOG2_E)).astype(BF16)

    @pl.when(_is_sample_step())
    def _():
        kts_ref[...] = ktp_ref[...]
        vts_ref[...] = vtp_ref[...]


def _inproj(x, g, w_in, ln_g, ln_b, n_prompt_tiles, prompt_tiles_per_row):
    n = x.shape[0]
    half = _merged_tile_spec(SB_WIDTH, n_prompt_tiles)
    blocks_per_tile = ROW_TILE // KEY_BLOCK
    key_blocks = pl.BlockSpec((blocks_per_tile, SB_WIDTH, KEY_BLOCK),
                              lambda s: ((s + n_prompt_tiles) % (n_prompt_tiles + 1), 0, 0))

    def prompt_t_index(s):
        tile = jnp.maximum(s - 1, 0)
        return tile // prompt_tiles_per_row, tile % prompt_tiles_per_row

    prompt_t = pl.BlockSpec((SB_WIDTH, ROW_TILE), prompt_t_index)
    prompt_t_sds = jax.ShapeDtypeStruct(
        (n_prompt_tiles // prompt_tiles_per_row * SB_WIDTH, prompt_tiles_per_row * ROW_TILE), F32)
    sample_t_sds = jax.ShapeDtypeStruct((SB_WIDTH, ROW_TILE), F32)
    f32_half = jax.ShapeDtypeStruct((n, SB_WIDTH), F32)
    key_blocks_sds = jax.ShapeDtypeStruct((n // KEY_BLOCK, SB_WIDTH, KEY_BLOCK), BF16)
    return pl.pallas_call(
        _inproj_kernel,
        grid=(n // ROW_TILE,),
        in_specs=[_merged_tile_spec(D_MODEL, n_prompt_tiles), _const_spec((1, D_MODEL)),
                  _HBM_SPEC, _const_spec((1, SGU_WIDTH)), _const_spec((1, SGU_WIDTH))],
        out_specs=[half] * 3 + [key_blocks] * 2 + [prompt_t] * 2 + [_sample_tile_spec(ROW_TILE, SB_WIDTH)] * 2,
        out_shape=[jax.ShapeDtypeStruct((n, SB_WIDTH), BF16), f32_half, f32_half,
                   key_blocks_sds, key_blocks_sds, prompt_t_sds, prompt_t_sds,
                   sample_t_sds, sample_t_sds],
        scratch_shapes=[pltpu.VMEM((D_MODEL, SB_WIDTH + 2 * SGU_WIDTH), BF16),
                        pltpu.VMEM((D_MODEL, 2 * SB_WIDTH), BF16),
                        pltpu.VMEM((2, WEIGHT_CHUNK_ROWS, w_in.shape[1]), F32),
                        pltpu.SemaphoreType.DMA((2,))],
        compiler_params=_params(1),
        name="inproj",
    )(x, g, w_in, ln_g, ln_b)


def _pair_rows(x, lane_lo):
    zero = jnp.zeros_like(x)
    return jnp.concatenate([jnp.where(lane_lo, x, zero), jnp.where(lane_lo, zero, x)], axis=0)


def _sb_blocks(qqs, kts, vts, carries, suffix_ones, lane_lo, masks, tops=None):
    n_blocks = len(masks)
    tops = tops or [None] * n_blocks
    t = qqs[0].shape[0] // 2
    nt_dims = (((1,), (1,)), ((), ()))

    def head_rows(x, top):
        return x if top is None else jnp.concatenate([x[:top], x[t:t + top]], axis=0)

    def weighted_values(a, vt):
        both = lax.dot_general(a, vt, nt_dims, preferred_element_type=F32)
        rows = a.shape[0] // 2
        return jnp.where(lane_lo, both[:rows], both[rows:])

    zs = []
    for qq, kts_p in zip(qqs, kts):
        zs_p, b = [], 0
        while b < n_blocks:
            if b + 1 < n_blocks and tops[b] is None and tops[b + 1] is None:
                z2 = jnp.dot(qq, jnp.concatenate([kts_p[b], kts_p[b + 1]], axis=1),
                             preferred_element_type=F32)
                zs_p += [z2[:, :KEY_BLOCK], z2[:, KEY_BLOCK:]]
                b += 2
            else:
                zs_p.append(jnp.dot(head_rows(qq, tops[b]), kts_p[b], preferred_element_type=F32))
                b += 1
        zs.append(zs_p)
    useds, splits = [], []
    for zs_p in zs:
        useds.append([])
        splits.append([])
        for i, mask in enumerate(masks):
            z = zs_p[i]
            if mask is not None:
                z = zs_p[i] = jnp.where(mask, z, HIDDEN_LOGIT)
            used = jnp.maximum(z, 0.0) + jnp.log(1.0 + jnp.exp2(-jnp.abs(z))) * LOG2_E
            hi = used.astype(BF16)
            lo = (used - hi.astype(F32)).astype(BF16)
            useds[-1].append(used)
            splits[-1].append(jnp.concatenate([hi, lo], axis=1))
    sums = [[jnp.dot(s, suffix_ones, preferred_element_type=F32) for s in splits_p]
            for splits_p in splits]
    outs, new_carries = [], []
    for zs_p, useds_p, sums_p, vts_p, carry in zip(zs, useds, sums, vts, carries):
        weights, values_t, top_outs = [], [], []
        for z, used, s, vt, top in zip(zs_p, useds_p, sums_p, vts_p, tops):
            later = s[:, :KEY_BLOCK] if carry is None else s[:, :KEY_BLOCK] + head_rows(carry, top)
            a = jnp.exp2(z - used - later).astype(BF16)
            row_sum = s[:, KEY_BLOCK:]
            if top is None:
                weights.append(a)
                values_t.append(vt)
                carry = row_sum if carry is None else carry + row_sum
            else:
                top_outs.append(weighted_values(a, vt))
                carry = jnp.concatenate([carry[:top] + row_sum[:top], carry[top:t],
                                         carry[t:t + top] + row_sum[top:], carry[t + top:]], axis=0)
        out = weighted_values(jnp.concatenate(weights, axis=1), jnp.concatenate(values_t, axis=1))
        for o_top in top_outs:
            top = o_top.shape[0]
            out = jnp.concatenate([out[:top] + o_top, out[top:]], axis=0)
        outs.append(out)
        new_carries.append(carry)
    return outs, new_carries


def _suffix_ones():
    r = lax.broadcasted_iota(jnp.int32, (2 * KEY_BLOCK, 2 * KEY_BLOCK), 0) % KEY_BLOCK
    c = lax.broadcasted_iota(jnp.int32, (2 * KEY_BLOCK, 2 * KEY_BLOCK), 1)
    return jnp.where((c >= KEY_BLOCK) | (r > c), 1.0, 0.0).astype(BF16)


_PAIR_SLICES = [slice(p * LANES, (p + 1) * LANES) for p in range(N_PAIRS)]


class _QueryBlock:
    def __init__(self, q_ref, o_ref, carry_ref, n_past, diag_tiles, joint_tiles, walk_tiles):
        self.q_ref, self.o_ref, self.carry_ref, self.n_past = q_ref, o_ref, carry_ref, n_past
        self.least_ref = carry_ref.at[N_PAIRS]
        self.diag_tiles, self.joint_tiles, self.walk_tiles = diag_tiles, joint_tiles, walk_tiles


class _StickBreaking:
    def __init__(self, tq, diag_key_offset):
        self.tq = tq
        self.lane_lo = lax.broadcasted_iota(jnp.int32, (1, LANES), 1) < HEAD_DIM
        self.suffix_ones = _suffix_ones()
        self.row = lax.broadcasted_iota(jnp.int32, (2 * tq, KEY_BLOCK), 0) % tq
        col = lax.broadcasted_iota(jnp.int32, (2 * tq, KEY_BLOCK), 1) - diag_key_offset
        self.causal = (col >= 0) & (col < self.row)

    def _qq_tiles(self, blk):
        return [_pair_rows(blk.q_ref[:, cols], self.lane_lo) for cols in _PAIR_SLICES]

    def _store(self, blk, outs, carries, first):
        for p, cols in enumerate(_PAIR_SLICES):
            if first:
                blk.o_ref[:, cols] = outs[p]
            else:
                blk.o_ref[:, cols] += outs[p]
            blk.carry_ref[p] = carries[p]

    def first_step(self, blks, n_joint):
        qqs, kts, vts = [], [], []
        for blk in blks:
            kd, vd = blk.diag_tiles()
            kts_b, vts_b = [[t] for t in kd], [[t] for t in vd]
            for b in range(n_joint):
                kj, vj = blk.joint_tiles(b + 1)
                for p in range(N_PAIRS):
                    kts_b[p].append(kj[p])
                    vts_b[p].append(vj[p])
            qqs += self._qq_tiles(blk)
            kts += kts_b
            vts += vts_b
        tops = [None] * n_joint + ([TOP_ROWS] if n_joint > 1 else [None])
        outs, carries = _sb_blocks(qqs, kts, vts, [None] * len(qqs),
                                   self.suffix_ones, self.lane_lo, [self.causal] + [None] * n_joint, tops)
        for i, blk in enumerate(blks):
            pairs = slice(i * N_PAIRS, (i + 1) * N_PAIRS)
            self._store(blk, outs[pairs], carries[pairs], True)
        least = functools.reduce(jnp.minimum, carries)
        for blk in blks:
            blk.least_ref[...] = least

    def _single_block(self, blk, tiles, mask):
        kts, vts = tiles
        outs, carries = _sb_blocks(self._qq_tiles(blk), [[t] for t in kts], [[t] for t in vts],
                                   [blk.carry_ref[p] for p in range(N_PAIRS)],
                                   self.suffix_ones, self.lane_lo, [mask])
        self._store(blk, outs, carries, False)

    def _least_used(self, blk, rows=None):
        tq = self.tq

        def pick(x):
            return x if rows is None else jnp.concatenate([x[rows], x[tq + rows.start:tq + rows.stop]], axis=0)
        return jnp.min(functools.reduce(jnp.minimum, [pick(blk.carry_ref[p]) for p in range(N_PAIRS)]))

    def finish_partial_block(self, blk):
        @pl.when(self._least_used(blk, slice(TOP_ROWS, self.tq)) < USED_STICK_CUTOFF)
        def _():
            self._single_block(blk, blk.joint_tiles(JOINT_PAST_BLOCKS), self.row >= TOP_ROWS)

    def walk(self, blk, next_block):
        def body(state):
            j, _ = state
            self._single_block(blk, blk.walk_tiles(j), None)
            return j - 1, self._least_used(blk)

        lax.while_loop(lambda s: (s[0] >= 0) & (s[1] < USED_STICK_CUTOFF), body,
                       (next_block, self._least_used(blk)))

    def finish(self, blks, saw_partial_block, next_blocks):
        least_used = jnp.min(functools.reduce(jnp.minimum, [blk.least_ref[...] for blk in blks]))

        @pl.when(least_used < USED_STICK_CUTOFF)
        def _():
            for blk, partial in zip(blks, saw_partial_block):
                if partial is True:
                    self.finish_partial_block(blk)
                elif partial is not False:
                    pl.when(partial)(functools.partial(self.finish_partial_block, blk))
            for blk, next_block in zip(blks, next_blocks):
                self.walk(blk, next_block)


def _sb_prompt_kernel(q_ref, kc_ref, kp_ref, vc_ref, vp_ref, k_hbm, v_hbm, o_ref,
                      carry_ref, kbuf_ref, vbuf_ref, sem_ref, *, blocks_per_row):
    step = pl.program_id(1)
    first_block = pl.program_id(0) * blocks_per_row

    def window_tiles(r):
        k_ref, v_ref, at = (kc_ref, vc_ref, r) if r >= 0 else (kp_ref, vp_ref, Q_BLOCKS_PER_STEP + r)
        return ([k_ref[at, rows, :] for rows in _PAIR_SLICES],
                [v_ref[at, rows, :] for rows in _PAIR_SLICES])

    def older_tiles(j):
        copies = [pltpu.make_async_copy(hbm.at[first_block + j], buf, sem_ref.at[s])
                  for s, (hbm, buf) in enumerate(((k_hbm, kbuf_ref), (v_hbm, vbuf_ref)))]
        for c in copies:
            c.start()
        for c in copies:
            c.wait()
        return ([kbuf_ref[rows, :] for rows in _PAIR_SLICES],
                [vbuf_ref[rows, :] for rows in _PAIR_SLICES])

    sb = _StickBreaking(KEY_BLOCK, 0)
    blks = []
    for r in range(Q_BLOCKS_PER_STEP):
        i = step * Q_BLOCKS_PER_STEP + r
        rows = pl.ds(r * KEY_BLOCK, KEY_BLOCK)
        blks.append(_QueryBlock(q_ref.at[rows, :], o_ref.at[rows, :], carry_ref.at[r], i,
                                functools.partial(window_tiles, r),
                                lambda back, r=r: window_tiles(r - back), older_tiles))

    assert Q_BLOCKS_PER_STEP >= JOINT_PAST_BLOCKS > 1

    @pl.when(step > 0)
    def _():
        for first in range(0, Q_BLOCKS_PER_STEP, Q_BLOCKS_PER_PASS):
            sb.first_step(blks[first:first + Q_BLOCKS_PER_PASS], JOINT_PAST_BLOCKS)

    @pl.when(step == 0)
    def _():
        for r, blk in enumerate(blks):
            sb.first_step([blk], min(r, JOINT_PAST_BLOCKS))

    sb.finish(blks,
              [True if r >= JOINT_PAST_BLOCKS else step > 0 for r in range(Q_BLOCKS_PER_STEP)],
              [jnp.where(step > 0, blk.n_past - JOINT_PAST_BLOCKS, r - min(r, JOINT_PAST_BLOCKS)) - 1
               for r, blk in enumerate(blks)])


def _sb_prompt(q, kt3, vt3, batch, seq):
    nq = seq // (KEY_BLOCK * Q_BLOCKS_PER_STEP)
    blk = pl.BlockSpec((KEY_BLOCK * Q_BLOCKS_PER_STEP, SB_WIDTH), lambda b, i: (b * nq + i, 0))
    window = (Q_BLOCKS_PER_STEP, SB_WIDTH, KEY_BLOCK)
    current = pl.BlockSpec(window, lambda b, i: (b * nq + i, 0, 0))
    previous = pl.BlockSpec(window, lambda b, i: (b * nq + jnp.maximum(i - 1, 0), 0, 0))
    return pl.pallas_call(
        functools.partial(_sb_prompt_kernel, blocks_per_row=seq // KEY_BLOCK),
        grid=(batch, nq),
        in_specs=[blk, current, previous, current, previous, _HBM_SPEC, _HBM_SPEC],
        out_specs=blk,
        out_shape=jax.ShapeDtypeStruct((batch * seq, SB_WIDTH), F32),
        scratch_shapes=[pltpu.VMEM((Q_BLOCKS_PER_STEP, N_PAIRS + 1, 2 * KEY_BLOCK, LANES), F32),
                        pltpu.VMEM((SB_WIDTH, KEY_BLOCK), BF16), pltpu.VMEM((SB_WIDTH, KEY_BLOCK), BF16),
                        pltpu.SemaphoreType.DMA((2,))],
        compiler_params=_params(2),
        name="sb_prompt",
    )(q, kt3, kt3, vt3, vt3, kt3, vt3)


def _sb_sample_kernel(q_ref, kd_ref, vd_ref, kr_ref, vr_ref, kc_hbm, vc_hbm, o_ref,
                      carry_ref, kbuf_ref, vbuf_ref, sem_ref, *, past_len):
    tq = q_ref.shape[0]
    n_past = past_len // KEY_BLOCK
    n_recent = kr_ref.shape[2] // KEY_BLOCK
    b = pl.program_id(0)

    def diag_tiles():
        return ([kd_ref[0, rows, :] for rows in _PAIR_SLICES],
                [vd_ref[0, rows, :] for rows in _PAIR_SLICES])

    def recent_tiles(j):
        first = (j - (n_past - n_recent)) * KEY_BLOCK
        keys = slice(first, first + KEY_BLOCK)
        return ([kr_ref[0, rows, keys].astype(BF16) for rows in _PAIR_SLICES],
                [vr_ref[0, rows, keys].astype(BF16) for rows in _PAIR_SLICES])

    def cache_tiles(j):
        keys = pl.ds(pl.multiple_of(j * KEY_BLOCK, KEY_BLOCK), KEY_BLOCK)
        copies = [pltpu.make_async_copy(hbm.at[b, :, keys], buf, sem_ref.at[s])
                  for s, (hbm, buf) in enumerate(((kc_hbm, kbuf_ref), (vc_hbm, vbuf_ref)))]
        for c in copies:
            c.start()
        for c in copies:
            c.wait()
        return ([kbuf_ref[rows, :].astype(BF16) for rows in _PAIR_SLICES],
                [vbuf_ref[rows, :].astype(BF16) for rows in _PAIR_SLICES])

    assert n_recent == JOINT_PAST_BLOCKS > 1
    sb = _StickBreaking(tq, (b % (KEY_BLOCK // tq)) * tq)
    blk = _QueryBlock(q_ref, o_ref, carry_ref, n_past, diag_tiles,
                      lambda back: recent_tiles(n_past - back), cache_tiles)
    sb.first_step([blk], JOINT_PAST_BLOCKS)
    sb.finish([blk], [True], [n_past - JOINT_PAST_BLOCKS - 1])


def _sb_sample(q, kt3, vt3, cache_kt, cache_vt, t_new, first_row):
    batch, _, past_len = cache_kt.shape
    n_recent = JOINT_PAST_BLOCKS * KEY_BLOCK
    seqs_per_block = KEY_BLOCK // t_new
    new = pl.BlockSpec((1, SB_WIDTH, KEY_BLOCK),
                       lambda b: (first_row // KEY_BLOCK + b // seqs_per_block, 0, 0))
    recent = pl.BlockSpec((1, SB_WIDTH, n_recent), lambda b: (b, 0, past_len // n_recent - 1))
    hbm = pl.BlockSpec(memory_space=pl.ANY)
    return pl.pallas_call(
        functools.partial(_sb_sample_kernel, past_len=past_len),
        grid=(batch,),
        in_specs=[pl.BlockSpec((t_new, SB_WIDTH), lambda b: (first_row // t_new + b, 0)),
                  new, new, recent, recent, hbm, hbm],
        out_specs=pl.BlockSpec((t_new, SB_WIDTH), lambda b: (b, 0)),
        out_shape=jax.ShapeDtypeStruct((batch * t_new, SB_WIDTH), F32),
        scratch_shapes=[pltpu.VMEM((N_PAIRS + 1, 2 * t_new, LANES), F32),
                        pltpu.VMEM((SB_WIDTH, KEY_BLOCK), F32),
                        pltpu.VMEM((SB_WIDTH, KEY_BLOCK), F32),
                        pltpu.SemaphoreType.DMA((2,))],
        compiler_params=_params(1),
        name="sb_sample",
    )(q, kt3, vt3, cache_kt, cache_vt, cache_kt, cache_vt)


def _mix_rows(x_ref, op_ref, os_ref, u_ref, gn_ref, ws_ref, bias_ref, gsb_ref, gsgu_ref, wout_ref,
              post_ref, sgu_ref, rows):
    lane_lo = lax.broadcasted_iota(jnp.int32, (1, LANES), 1) < HEAD_DIM
    chunks = [slice(first, first + SGU_CHUNK) for first in range(rows.start, rows.stop, SGU_CHUNK)]
    for p in range(N_PAIRS):
        cols = slice(p * LANES, (p + 1) * LANES)
        gg = jnp.concatenate([_pair_rows(gn_ref[chunk, cols].astype(BF16), lane_lo)
                              for chunk in chunks], axis=1)
        mixed = jnp.dot(ws_ref[0, p], gg, preferred_element_type=F32)
        for c, chunk in enumerate(chunks):
            sgu_ref[chunk, cols] = u_ref[chunk, cols] * (mixed[:, c * LANES:(c + 1) * LANES]
                                                         + bias_ref[0, :, cols])
    o_sb = jnp.where(_is_sample_step(), os_ref[rows, :], op_ref[rows, :])
    merged = jnp.concatenate([_rms(o_sb, gsb_ref[...]).astype(BF16),
                              _rms(sgu_ref[rows, :], gsgu_ref[...]).astype(BF16)], axis=1)
    y = jnp.dot(merged, wout_ref[...], preferred_element_type=F32)
    return x_ref[rows, :] + _rms(y, post_ref[...])


def _mix_ffn_kernel(x_ref, op_ref, os_ref, u_ref, gn_ref, ws_ref, bias_ref, gsb_ref, gsgu_ref,
                    wout_ref, mix_post_ref, pre_ref, post_ref, wg_hbm, wu_hbm, wd_hbm,
                    yp_ref, ys_ref, sgu_ref,
                    wg_ref, wu_ref, wd_ref, stage_g_ref, stage_u_ref, stage_d_ref, sem_ref, act_ref):
    _load_ffn_weights(wg_hbm, wu_hbm, wd_hbm, wg_ref, wu_ref, wd_ref,
                      (stage_g_ref, stage_u_ref, stage_d_ref), sem_ref)

    def mix(rows):
        return _mix_rows(x_ref, op_ref, os_ref, u_ref, gn_ref, ws_ref, bias_ref, gsb_ref, gsgu_ref,
                         wout_ref, mix_post_ref, sgu_ref, rows)

    groups = FFN_ROW_GROUPS
    x = mix(groups[0])
    for g, rows in enumerate(groups):
        stages = _ffn_stages(x, pre_ref, post_ref, wg_ref, wu_ref, wd_ref, act_ref, rows)
        stages[0]()
        stages[1]()
        if g + 1 < len(groups):
            x = mix(groups[g + 1])
        for stage in stages[2:]:
            y = stage()
        yp_ref[rows, :] = y

    @pl.when(_is_sample_step())
    def _():
        ys_ref[...] = yp_ref[...]


def _mix_ffn(x, o_prompt, o_sample, u, gn, ws_pairs, bias, g_sb, g_sgu, w_out, mix_post_g,
             pre_g, post_g, wg, wu, wd, n_prompt_tiles):
    row = _merged_tile_spec(D_MODEL, n_prompt_tiles)
    half = _merged_tile_spec(SB_WIDTH, n_prompt_tiles)
    which = lambda s: jnp.where(s == 0, 1, 0)
    sds = lambda rows: jax.ShapeDtypeStruct((rows, D_MODEL), F32)
    return pl.pallas_call(
        _mix_ffn_kernel,
        grid=(n_prompt_tiles + 1,),
        in_specs=[row, _prompt_tile_spec(SB_WIDTH), _sample_tile_spec(SB_WIDTH), half, half,
                  pl.BlockSpec((1,) + ws_pairs.shape[1:], lambda s: (which(s), 0, 0, 0)),
                  pl.BlockSpec((1,) + bias.shape[1:], lambda s: (which(s), 0, 0)),
                  _const_spec((1, SB_WIDTH)), _const_spec((1, SGU_WIDTH)),
                  _const_spec((D_MODEL, D_MODEL)), _const_spec((1, D_MODEL))] + _FFN_WEIGHT_SPECS,
        out_specs=[_prompt_tile_spec(D_MODEL), _sample_tile_spec(D_MODEL)],
        out_shape=[sds(n_prompt_tiles * ROW_TILE), sds(ROW_TILE)],
        scratch_shapes=[pltpu.VMEM((ROW_TILE, SGU_WIDTH), F32)] + _FFN_SCRATCH,
        compiler_params=_params(1),
        name="mix_ffn",
    )(x, o_prompt, o_sample, u, gn, ws_pairs, bias, g_sb, g_sgu, w_out, mix_post_g,
      pre_g, post_g, wg, wu, wd)


def _sgu_operands(w_s, b_s, t_new):
    i = jnp.arange(SGU_CHUNK)
    mask = (i[None, :] // SGU_CAUSAL_CHUNK) <= (i[:, None] // SGU_CAUSAL_CHUNK)
    w_prompt = w_s * mask[None].astype(w_s.dtype)
    reps = SGU_CHUNK // t_new
    corner = w_prompt[:, :t_new, :t_new]
    eye = jnp.eye(reps, dtype=w_s.dtype)
    w_sample = jnp.einsum('ab,gij->gaibj', eye, corner).reshape(w_s.shape)
    groups = w_s.shape[0]

    def pairs(w):
        w = w.reshape(groups // 2, 2, SGU_CHUNK, SGU_CHUNK).transpose(0, 2, 1, 3)
        return w.reshape(groups // 2, SGU_CHUNK, 2 * SGU_CHUNK)

    def bias(b):
        return jnp.repeat(b.T, SGU_WIDTH // groups, axis=1)

    ws = jnp.stack([pairs(w_prompt), pairs(w_sample)]).astype(BF16)
    return ws, jnp.stack([bias(b_s), bias(jnp.tile(b_s[:, :t_new], (1, reps)))])


def kernel(x_prompt, x_sample, cache_k_sb, cache_v_sb, ffn1_pre_g, ffn1_post_g, ffn1_w_gate, ffn1_w_up, ffn1_w_down, mix_pre_g, mix_post_g, w_in, sgu_ln_g, sgu_ln_b, sgu_w_s, sgu_b_s, g_out_sb, g_out_sgu, w_out, ffn2_pre_g, ffn2_post_g, ffn2_w_gate, ffn2_w_up, ffn2_w_down):
    depth, batch, seq = w_in.shape[0], x_prompt.shape[0], x_prompt.shape[1]
    dec_batch, t_new, past_len = x_sample.shape[0], x_sample.shape[1], cache_k_sb.shape[2]
    assert depth == 1 and dec_batch * t_new == ROW_TILE and seq % ROW_TILE == 0
    assert SGU_CHUNK % t_new == 0 and KEY_BLOCK % t_new == 0
    assert past_len % (JOINT_PAST_BLOCKS * KEY_BLOCK) == 0
    n_prompt_tiles = batch * seq // ROW_TILE

    mat = lambda w: w[0].astype(BF16)
    vec = lambda g: g[0][None, :]

    x = _ffn(x_prompt.reshape(batch * seq, D_MODEL), x_sample.reshape(ROW_TILE, D_MODEL),
             vec(ffn1_pre_g), vec(ffn1_post_g), ffn1_w_gate[0], ffn1_w_up[0], ffn1_w_down[0],
             n_prompt_tiles)
    q, u, gn, kt3, vt3, kt_p, vt_p, kt_s, vt_s = _inproj(
        x, vec(mix_pre_g), w_in[0], vec(sgu_ln_g), vec(sgu_ln_b), n_prompt_tiles, seq // ROW_TILE)

    o_prompt = _sb_prompt(q, kt3, vt3, batch, seq)
    cache_t = lambda c: c[0].transpose(0, 2, 3, 1).reshape(dec_batch, SB_WIDTH, past_len)
    o_sample = _sb_sample(q, kt3, vt3, cache_t(cache_k_sb), cache_t(cache_v_sb), t_new, batch * seq)

    ws_pairs, bias = _sgu_operands(sgu_w_s[0], sgu_b_s[0], t_new)
    y_p, y_s = _mix_ffn(x, o_prompt, o_sample, u, gn, ws_pairs, bias, vec(g_out_sb), vec(g_out_sgu),
                        mat(w_out), vec(mix_post_g), vec(ffn2_pre_g), vec(ffn2_post_g),
                        ffn2_w_gate[0], ffn2_w_up[0], ffn2_w_down[0], n_prompt_tiles)

    def prompt_heads(t):
        return t.reshape(batch, N_HEADS, HEAD_DIM, seq).transpose(0, 3, 1, 2)[None]

    def sample_heads(t):
        return t.reshape(N_HEADS, HEAD_DIM, dec_batch, t_new).transpose(2, 3, 0, 1)[None]

    g_s = gn[batch * seq:].reshape(1, dec_batch, t_new, N_HEADS, HEAD_DIM)
    return (y_p.reshape(batch, seq, D_MODEL), y_s.reshape(dec_batch, t_new, D_MODEL),
            prompt_heads(kt_p), prompt_heads(vt_p), sample_heads(kt_s), sample_heads(vt_s), g_s)
```
